```python
import jax, jax.numpy as jnp
from jax import lax
import numpy as np

D_MODEL = 1024
BATCH = 4
SEQ = 4096
DEPTH = 1

MEM_LEN = 256
EPS = 1e-6
CONV_DIM = 512
CONV_WIDTH = 31
FOX_HEADS = 8
FOX_HEAD_DIM = 64
FOX_DIM = FOX_HEADS * FOX_HEAD_DIM
Q_BLOCK = 128
XA_HEADS = 4
XA_HEAD_DIM = 128
XA_DIM = XA_HEADS * XA_HEAD_DIM
N_GROUPS = 4
EXPERTS_PER_GROUP = 4
N_EXPERTS = N_GROUPS * EXPERTS_PER_GROUP
TOP_K = 2
EXPERT_FF = 256
SPLITS = (2 * CONV_DIM, FOX_DIM, FOX_DIM, FOX_DIM, FOX_HEADS, D_MODEL, D_MODEL)
IN_COLS = sum(SPLITS)

kernel_name = "hybrid_conv_fox_xattn_hmoe"


def rmsnorm(x, g):
    xf = x.astype(jnp.float32)
    y = xf * lax.rsqrt(jnp.mean(xf * xf, axis=-1, keepdims=True) + EPS)
    return (y * g.astype(jnp.float32)).astype(x.dtype)


def layernorm(x, g, b):
    xf = x.astype(jnp.float32)
    mu = jnp.mean(xf, axis=-1, keepdims=True)
    var = jnp.mean(jnp.square(xf - mu), axis=-1, keepdims=True)
    y = (xf - mu) * lax.rsqrt(var + EPS)
    return (y * g.astype(jnp.float32) + b.astype(jnp.float32)).astype(x.dtype)


def conformer_conv(u_glu, dw_w, dw_b, ln_g, ln_b, pw_w):
    a, gate = jnp.split(u_glu, 2, axis=-1)
    u = a * jax.nn.sigmoid(gate)
    u = lax.conv_general_dilated(
        u, dw_w[:, None, :].astype(u.dtype), window_strides=(1,),
        padding=[(CONV_WIDTH - 1, 0)],
        dimension_numbers=("NWC", "WIO", "NWC"),
        feature_group_count=CONV_DIM) + dw_b
    u = jax.nn.silu(layernorm(u, ln_g, ln_b))
    return u @ pw_w


def fox_attention(q, k, v, logf):
    B, T, H, dh = q.shape
    nb = T // Q_BLOCK
    scale = dh ** -0.5
    c = jnp.cumsum(logf, axis=1).transpose(0, 2, 1)
    qh = q.transpose(0, 2, 1, 3)
    kh = k.transpose(0, 2, 1, 3)
    vh = v.transpose(0, 2, 1, 3)
    qb = qh.reshape(B, H, nb, Q_BLOCK, dh).transpose(2, 0, 1, 3, 4)
    cb = c.reshape(B, H, nb, Q_BLOCK).transpose(2, 0, 1, 3)
    kpos = jnp.arange(T)

    def block(args):
        q_i, c_i, i = args
        s = jnp.einsum("bhqd,bhkd->bhqk", q_i, kh).astype(jnp.float32) * scale
        s = s + c_i[..., :, None] - c[:, :, None, :]
        qpos = i * Q_BLOCK + jnp.arange(Q_BLOCK)
        s = jnp.where(kpos[None, :] <= qpos[:, None], s, -jnp.inf)
        p = jax.nn.softmax(s, axis=-1)
        return jnp.einsum("bhqk,bhkd->bhqd", p.astype(vh.dtype), vh)

    o = lax.map(block, (qb, cb, jnp.arange(nb)))
    return o.transpose(1, 0, 3, 2, 4).reshape(B, T, H * dh)


def cross_attention(h, m, wq, wk, wv, wo):
    B, T, _ = h.shape
    M = m.shape[1]
    q = (h @ wq).reshape(B, T, XA_HEADS, XA_HEAD_DIM)
    k = (m @ wk).reshape(B, M, XA_HEADS, XA_HEAD_DIM)
    v = (m @ wv).reshape(B, M, XA_HEADS, XA_HEAD_DIM)
    s = jnp.einsum("bthd,bmhd->bhtm", q, k).astype(jnp.float32) * XA_HEAD_DIM ** -0.5
    p = jax.nn.softmax(s, axis=-1)
    o = jnp.einsum("bhtm,bmhd->bthd", p.astype(v.dtype), v).reshape(B, T, XA_DIM)
    return o @ wo


def hierarchical_moe(h, wg, bg, we, be, w_gate, w_up, w_down):
    B, T, D = h.shape
    hf = h.reshape(B * T, D)
    g_logits = (hf @ wg).astype(jnp.float32) + bg.astype(jnp.float32)
    p_group = jax.nn.softmax(g_logits, axis=-1)
    g_star = jnp.argmax(p_group, axis=-1)
    p_gsel = jnp.take_along_axis(p_group, g_star[:, None], axis=-1)[:, 0]
    e_logits = ((hf @ we).astype(jnp.float32) + be.astype(jnp.float32)).reshape(-1, N_GROUPS, EXPERTS_PER_GROUP)
    e_sel = jnp.take_along_axis(e_logits, g_star[:, None, None], axis=1)[:, 0]
    p_exp = jax.nn.softmax(e_sel, axis=-1)
    top_v, top_i = lax.top_k(p_exp, TOP_K)
    top_v = top_v / jnp.sum(top_v, axis=-1, keepdims=True)
    weights = p_gsel[:, None] * top_v
    expert_idx = g_star[:, None] * EXPERTS_PER_GROUP + top_i
    combine = jnp.sum(jax.nn.one_hot(expert_idx, N_EXPERTS, dtype=jnp.float32) * weights[..., None], axis=1)
    a = jnp.einsum("nd,edf->nef", hf, w_gate)
    b = jnp.einsum("nd,edf->nef", hf, w_up)
    u = jax.nn.silu(a) * b * combine.astype(hf.dtype)[..., None]
    y = jnp.einsum("nef,efd->nd", u, w_down)
    return y.reshape(B, T, D)


def setup_inputs(seed: int = 0) -> dict:
    key = jax.random.key(seed)
    ks = jax.random.split(key, 32)
    f32 = jnp.float32

    def nrm(k, shape, fan_in):
        return jax.random.normal(k, shape, f32) * (fan_in ** -0.5)

    def gain(k, shape):
        return 1.0 + 0.1 * jax.random.normal(k, shape, f32)

    def small(k, shape, s=0.01):
        return s * jax.random.normal(k, shape, f32)

    L = DEPTH
    return {
        "x": jax.random.normal(ks[0], (BATCH, SEQ, D_MODEL), f32),
        "mem": jax.random.normal(ks[1], (BATCH, MEM_LEN, D_MODEL), f32),
        "norm_mix_g": gain(ks[2], (L, D_MODEL)),
        "w_in": nrm(ks[3], (L, D_MODEL, IN_COLS), D_MODEL),
        "fox_bf": jax.random.uniform(ks[4], (L, FOX_HEADS), f32, 1.0, 4.0),
        "conv_dw_w": nrm(ks[5], (L, CONV_WIDTH, CONV_DIM), CONV_WIDTH),
        "conv_dw_b": small(ks[6], (L, CONV_DIM)),
        "conv_ln_g": gain(ks[7], (L, CONV_DIM)),
        "conv_ln_b": small(ks[8], (L, CONV_DIM)),
        "conv_pw_w": nrm(ks[9], (L, CONV_DIM, D_MODEL), CONV_DIM),
        "attn_branch_w": nrm(ks[10], (L, FOX_DIM, D_MODEL), FOX_DIM),
        "w_out": nrm(ks[11], (L, D_MODEL, D_MODEL), D_MODEL),
        "norm_xa_g": gain(ks[12], (L, D_MODEL)),
        "norm_mem_g": gain(ks[13], (L, D_MODEL)),
        "xa_wq": nrm(ks[14], (L, D_MODEL, XA_DIM), D_MODEL),
        "xa_wk": nrm(ks[15], (L, D_MODEL, XA_DIM), D_MODEL),
        "xa_wv": nrm(ks[16], (L, D_MODEL, XA_DIM), D_MODEL),
        "xa_wo": nrm(ks[17], (L, XA_DIM, D_MODEL), XA_DIM),
        "norm_moe_g": gain(ks[18], (L, D_MODEL)),
        "router_group_w": nrm(ks[19], (L, D_MODEL, N_GROUPS), D_MODEL),
        "router_group_b": small(ks[20], (L, N_GROUPS)),
        "router_expert_w": nrm(ks[21], (L, D_MODEL, N_EXPERTS), D_MODEL),
        "router_expert_b": small(ks[22], (L, N_EXPERTS)),
        "expert_w_gate": nrm(ks[23], (L, N_EXPERTS, D_MODEL, EXPERT_FF), D_MODEL),
        "expert_w_up": nrm(ks[24], (L, N_EXPERTS, D_MODEL, EXPERT_FF), D_MODEL),
        "expert_w_down": nrm(ks[25], (L, N_EXPERTS, EXPERT_FF, D_MODEL), EXPERT_FF),
        "norm_final_g": gain(ks[26], (D_MODEL,)),
    }


def reference(x, mem, norm_mix_g, w_in, fox_bf, conv_dw_w, conv_dw_b, conv_ln_g, conv_ln_b,
              conv_pw_w, attn_branch_w, w_out, norm_xa_g, norm_mem_g, xa_wq, xa_wk, xa_wv, xa_wo,
              norm_moe_g, router_group_w, router_group_b, router_expert_w, router_expert_b,
              expert_w_gate, expert_w_up, expert_w_down, norm_final_g):
    B, T, _ = x.shape
    cuts = np.cumsum(SPLITS)[:-1].tolist()
    for l in range(DEPTH):
        h = rmsnorm(x, norm_mix_g[l])
        proj = h @ w_in[l]
        u_glu, q, k, v, f_logit, gate_c, gate_a = jnp.split(proj, cuts, axis=-1)
        conv_out = conformer_conv(u_glu, conv_dw_w[l], conv_dw_b[l], conv_ln_g[l],
                                  conv_ln_b[l], conv_pw_w[l])
        logf = jax.nn.log_sigmoid(f_logit.astype(jnp.float32) + fox_bf[l].astype(jnp.float32))
        attn = fox_attention(q.reshape(B, T, FOX_HEADS, FOX_HEAD_DIM),
                             k.reshape(B, T, FOX_HEADS, FOX_HEAD_DIM),
                             v.reshape(B, T, FOX_HEADS, FOX_HEAD_DIM), logf)
        attn_out = attn @ attn_branch_w[l]
        merged = jax.nn.sigmoid(gate_c) * conv_out + jax.nn.sigmoid(gate_a) * attn_out
        x = x + merged @ w_out[l]
        x = x + cross_attention(rmsnorm(x, norm_xa_g[l]), rmsnorm(mem, norm_mem_g[l]),
                                xa_wq[l], xa_wk[l], xa_wv[l], xa_wo[l])
        x = x + hierarchical_moe(rmsnorm(x, norm_moe_g[l]), router_group_w[l], router_group_b[l],
                                 router_expert_w[l], router_expert_b[l], expert_w_gate[l],
                                 expert_w_up[l], expert_w_down[l])
    return rmsnorm(x, norm_final_g)
```

```python
import functools

import jax
import jax.numpy as jnp
from jax import lax
from jax.experimental import pallas as pl
from jax.experimental.pallas import tpu as pltpu

D_MODEL = 1024
MEM_LEN = 256
EPS = 1e-6
CONV_DIM = 512
CONV_WIDTH = 31
FOX_HEADS = 8
FOX_HEAD_DIM = 64
FOX_DIM = FOX_HEADS * FOX_HEAD_DIM
XA_HEADS = 4
XA_HEAD_DIM = 128
XA_DIM = XA_HEADS * XA_HEAD_DIM
N_GROUPS = 4
EXPERTS_PER_GROUP = 4
N_EXPERTS = N_GROUPS * EXPERTS_PER_GROUP
EXPERT_FF = 256

AUX_ROWS = 16
VMEM_LIMIT = 56 * 1024 * 1024

F32 = jnp.float32
BF16 = jnp.bfloat16


def _const_spec(shape):
    n = len(shape)
    return pl.BlockSpec(shape, lambda *_: (0,) * n)


def _rms(x, g):
    return x * lax.rsqrt(jnp.mean(x * x, axis=-1, keepdims=True) + EPS) * g


def _sigmoid(x):
    return 1.0 / (1.0 + jnp.exp(-x))


def _dot(a, b):
    return jnp.dot(a, b, preferred_element_type=F32)


def _dot_nt(a, b):
    return lax.dot_general(a, b, (((1,), (1,)), ((), ())), preferred_element_type=F32)


def _dot_tn(a, b):
    return lax.dot_general(a, b, (((0,), (0,)), ((), ())), preferred_element_type=F32)


def _proj_in_body(x_ref, g_ref, wglu_ref, wgc_ref, wga_ref, wqkv_t_ref, wf_t_ref,
                  u_ref, gc_ref, ga_ref, qkv_t_ref, f_t_ref):
    h = _rms(x_ref[...], g_ref[...]).astype(BF16)
    glu = _dot(h, wglu_ref[...])
    u_ref[...] = glu[:, :CONV_DIM] * _sigmoid(glu[:, CONV_DIM:])
    gc_ref[...] = _dot(h, wgc_ref[...]).astype(BF16)
    ga_ref[...] = _dot(h, wga_ref[...]).astype(BF16)
    qkv_t = _dot_nt(wqkv_t_ref[...], h)
    scale = FOX_HEAD_DIM ** -0.5
    qkv_t_ref[:FOX_DIM, :] = (qkv_t[:FOX_DIM] * scale).astype(BF16)
    qkv_t_ref[FOX_DIM:, :] = qkv_t[FOX_DIM:].astype(BF16)
    f_t_ref[...] = _dot_nt(wf_t_ref[...], h)


def _proj_in(x, g, wglu, wgc, wga, wqkv_t, wf_t, *, tm):
    B, T, D = x.shape
    grid = (B, T // tm)
    row = lambda b, i: (b, i, 0)
    col = lambda b, i: (b, 0, i)
    return pl.pallas_call(
        _proj_in_body,
        grid=grid,
        in_specs=[
            pl.BlockSpec((None, tm, D), row),
            _const_spec((1, D)),
            _const_spec(wglu.shape), _const_spec(wgc.shape), _const_spec(wga.shape),
            _const_spec(wqkv_t.shape), _const_spec(wf_t.shape),
        ],
        out_specs=[
            pl.BlockSpec((None, tm, CONV_DIM), row),
            pl.BlockSpec((None, tm, D), row),
            pl.BlockSpec((None, tm, D), row),
            pl.BlockSpec((None, 3 * FOX_DIM, tm), col),
            pl.BlockSpec((None, FOX_HEADS, tm), col),
        ],
        out_shape=[
            jax.ShapeDtypeStruct((B, T, CONV_DIM), F32),
            jax.ShapeDtypeStruct((B, T, D), BF16),
            jax.ShapeDtypeStruct((B, T, D), BF16),
            jax.ShapeDtypeStruct((B, 3 * FOX_DIM, T), BF16),
            jax.ShapeDtypeStruct((B, FOX_HEADS, T), F32),
        ],
        compiler_params=pltpu.CompilerParams(
            dimension_semantics=("parallel", "parallel"), vmem_limit_bytes=VMEM_LIMIT),
        name="proj_in",
    )(x, g, wglu, wgc, wga, wqkv_t, wf_t)


def _split3(c):
    hi = c.astype(BF16)
    r = c - hi.astype(F32)
    mid = r.astype(BF16)
    lo = (r - mid.astype(F32)).astype(BF16)
    return hi, mid, lo


def _fox_prep_body(f_t_ref, bf_ref, qaux_ref, kaux_ref, c_ref, *, chunk):
    T = f_t_ref.shape[-1]
    z = f_t_ref[...] + bf_ref[...]
    logf = jnp.minimum(z, 0.0) - jnp.log1p(jnp.exp(-jnp.abs(z)))
    r_i = lax.broadcasted_iota(jnp.int32, (chunk, chunk), 0)
    c_i = lax.broadcasted_iota(jnp.int32, (chunk, chunk), 1)
    tri = jnp.where(r_i <= c_i, 1.0, 0.0).astype(BF16)
    carry = jnp.zeros((FOX_HEADS, 1), F32)
    for j in range(T // chunk):
        blk = logf[:, j * chunk:(j + 1) * chunk]
        hi, mid, lo = _split3(blk)
        cs = (_dot(lo, tri) + _dot(mid, tri)) + _dot(hi, tri) + carry
        c_ref[:, j * chunk:(j + 1) * chunk] = cs
        carry = cs[:, chunk - 1:chunk]
    row = lax.broadcasted_iota(jnp.int32, (AUX_ROWS, T), 0)
    for h in range(FOX_HEADS):
        hi, mid, lo = _split3(c_ref[h:h + 1, :])
        pieces = [jnp.broadcast_to(p.astype(F32), (AUX_ROWS, T)) for p in (hi, mid, lo)]
        one = jnp.where(row < 3, 1.0, 0.0)
        val = jnp.where(row == 0, pieces[0], jnp.where(row == 1, pieces[1], pieces[2]))
        val_k = jnp.where(row == 3, pieces[0], jnp.where(row == 4, pieces[1], pieces[2]))
        qaux = jnp.where(row < 3, val, jnp.where(row < 6, 1.0, 0.0))
        kaux = jnp.where(row < 3, one, jnp.where(row < 6, -val_k, 0.0))
        qaux_ref[h] = qaux.astype(BF16)
        kaux_ref[h] = kaux.astype(BF16)


def _fox_prep(f_t, bf):
    B, H, T = f_t.shape
    aux = jax.ShapeDtypeStruct((B, H, AUX_ROWS, T), BF16)
    aux_spec = pl.BlockSpec((None, H, AUX_ROWS, T), lambda b: (b, 0, 0, 0))
    return pl.pallas_call(
        functools.partial(_fox_prep_body, chunk=512),
        grid=(B,),
        in_specs=[pl.BlockSpec((None, H, T), lambda b: (b, 0, 0)), _const_spec((H, 1))],
        out_specs=[aux_spec, aux_spec],
        out_shape=[aux, aux],
        scratch_shapes=[pltpu.VMEM((H, T), F32)],
        compiler_params=pltpu.CompilerParams(dimension_semantics=("parallel",)),
        name="fox_prep",
    )(f_t, bf)


def _fox_attn_body(q_ref, k_ref, v_ref, qaux_ref, kaux_ref, o_ref, *, tq):
    T = q_ref.shape[-1]
    kk = lax.broadcasted_iota(jnp.int32, (tq, tq), 0)
    qq = lax.broadcasted_iota(jnp.int32, (tq, tq), 1)
    causal = kk <= qq

    def keys(k0):
        ka = jnp.concatenate([k_ref[:, pl.ds(k0, tq)], kaux_ref[:, pl.ds(k0, tq)]], axis=0)
        return ka, v_ref[:, pl.ds(k0, tq)]

    def q_body(qi, _):
        q0 = pl.multiple_of(qi * tq, tq)
        qa = jnp.concatenate([q_ref[:, pl.ds(q0, tq)], qaux_ref[:, pl.ds(q0, tq)]], axis=0)
        ka, vt = keys(q0)
        s = jnp.where(causal, _dot_tn(ka, qa), -jnp.inf)
        m = jnp.max(s, axis=0, keepdims=True)
        p = jnp.exp(s - m)
        l = jnp.sum(p, axis=0, keepdims=True)
        acc = _dot(vt, p.astype(BF16))

        def kv_body(kj, carry):
            m, l, acc = carry
            ka, vt = keys(pl.multiple_of(kj * tq, tq))
            s = _dot_tn(ka, qa)
            m_new = jnp.maximum(m, jnp.max(s, axis=0, keepdims=True))
            alpha = jnp.exp(m - m_new)
            p = jnp.exp(s - m_new)
            l = alpha * l + jnp.sum(p, axis=0, keepdims=True)
            acc = alpha * acc + _dot(vt, p.astype(BF16))
            return m_new, l, acc

        m, l, acc = lax.fori_loop(0, qi, kv_body, (m, l, acc))
        o_ref[:, pl.ds(q0, tq)] = (acc / l).astype(o_ref.dtype)
        return 0

    lax.fori_loop(0, T // tq, q_body, 0)


def _fox_attn(qkv_t, qaux, kaux, *, tq):
    B, _, T = qkv_t.shape
    H, dh = FOX_HEADS, FOX_HEAD_DIM
    head = lambda off: pl.BlockSpec((None, dh, T), lambda b, h: (b, off + h, 0))
    aux = pl.BlockSpec((None, None, AUX_ROWS, T), lambda b, h: (b, h, 0, 0))
    return pl.pallas_call(
        functools.partial(_fox_attn_body, tq=tq),
        grid=(B, H),
        in_specs=[head(0), head(H), head(2 * H), aux, aux],
        out_specs=pl.BlockSpec((None, dh, T), lambda b, h: (b, h, 0)),
        out_shape=jax.ShapeDtypeStruct((B, H * dh, T), BF16),
        compiler_params=pltpu.CompilerParams(
            dimension_semantics=("parallel", "parallel"), vmem_limit_bytes=VMEM_LIMIT),
        name="fox_attn",
    )(qkv_t, qkv_t, qkv_t, qaux, kaux)


HALO = 32


def _mix_out_body(x_ref, u_ref, uprev_ref, gc_ref, ga_ref, attn_t_ref, dww_ref, dwb_ref,
                  lng_ref, lnb_ref, pw_ref, abw_ref, wout_ref, o_ref, ubuf_ref, *, tc):
    first = pl.program_id(1) == 0
    ubuf_ref[:HALO, :] = jnp.where(first, 0.0, uprev_ref[...])
    ubuf_ref[HALO:, :] = u_ref[...]
    acc = jnp.zeros((tc, CONV_DIM), F32)
    for k in range(CONV_WIDTH):
        off = HALO - (CONV_WIDTH - 1) + k
        acc = acc + ubuf_ref[off:off + tc, :] * dww_ref[k:k + 1, :]
    acc = acc + dwb_ref[...]
    mu = jnp.mean(acc, axis=-1, keepdims=True)
    d = acc - mu
    var = jnp.mean(d * d, axis=-1, keepdims=True)
    y = d * lax.rsqrt(var + EPS) * lng_ref[...] + lnb_ref[...]
    y = (y * _sigmoid(y)).astype(BF16)
    conv_out = _dot(y, pw_ref[...])
    attn_out = _dot_tn(attn_t_ref[...], abw_ref[...])
    merged = (_sigmoid(gc_ref[...].astype(F32)) * conv_out
              + _sigmoid(ga_ref[...].astype(F32)) * attn_out)
    o_ref[...] = x_ref[...] + _dot(merged.astype(BF16), wout_ref[...])


def _mix_out(x, u, gc, ga, attn_t, dww, dwb, lng, lnb, pw, abw, wout, *, tc):
    B, T, D = x.shape
    row = lambda b, i: (b, i, 0)
    halo_blocks = tc // HALO
    return pl.pallas_call(
        functools.partial(_mix_out_body, tc=tc),
        grid=(B, T // tc),
        in_specs=[
            pl.BlockSpec((None, tc, D), row),
            pl.BlockSpec((None, tc, CONV_DIM), row),
            pl.BlockSpec((None, HALO, CONV_DIM),
                         lambda b, i: (b, jnp.maximum(i * halo_blocks - 1, 0), 0)),
            pl.BlockSpec((None, tc, D), row),
            pl.BlockSpec((None, tc, D), row),
            pl.BlockSpec((None, FOX_DIM, tc), lambda b, i: (b, 0, i)),
            _const_spec(dww.shape), _const_spec(dwb.shape), _const_spec(lng.shape),
            _const_spec(lnb.shape), _const_spec(pw.shape), _const_spec(abw.shape),
            _const_spec(wout.shape),
        ],
        out_specs=pl.BlockSpec((None, tc, D), row),
        out_shape=jax.ShapeDtypeStruct((B, T, D), F32),
        scratch_shapes=[pltpu.VMEM((HALO + tc, CONV_DIM), F32)],
        compiler_params=pltpu.CompilerParams(
            dimension_semantics=("parallel", "parallel"), vmem_limit_bytes=VMEM_LIMIT),
        name="mix_out",
    )(x, u, u, gc, ga, attn_t, dww, dwb, lng, lnb, pw, abw, wout)


def _mem_kv_body(m_ref, g_ref, wk_ref, wv_ref, k_ref, v_ref):
    h = _rms(m_ref[...], g_ref[...]).astype(BF16)
    k_ref[...] = _dot(h, wk_ref[...]).astype(BF16)
    v_ref[...] = _dot(h, wv_ref[...]).astype(BF16)


def _mem_kv(mem, g, wk, wv):
    B, M, D = mem.shape
    spec = pl.BlockSpec((None, M, XA_DIM), lambda b: (b, 0, 0))
    out = jax.ShapeDtypeStruct((B, M, XA_DIM), BF16)
    return pl.pallas_call(
        _mem_kv_body,
        grid=(B,),
        in_specs=[pl.BlockSpec((None, M, D), lambda b: (b, 0, 0)), _const_spec((1, D)),
                  _const_spec(wk.shape), _const_spec(wv.shape)],
        out_specs=[spec, spec],
        out_shape=[out, out],
        compiler_params=pltpu.CompilerParams(dimension_semantics=("parallel",)),
        name="mem_kv",
    )(mem, g, wk, wv)


def _xattn_body(x_ref, g_ref, wq_ref, k_ref, v_ref, wo_ref, o_ref):
    x = x_ref[...]
    q = _dot(_rms(x, g_ref[...]).astype(BF16), wq_ref[...]).astype(BF16)
    scale = XA_HEAD_DIM ** -0.5
    heads = []
    for h in range(XA_HEADS):
        sl = slice(h * XA_HEAD_DIM, (h + 1) * XA_HEAD_DIM)
        s = _dot_nt(q[:, sl], k_ref[:, sl]) * scale
        p = jnp.exp(s - jnp.max(s, axis=-1, keepdims=True))
        p = p / jnp.sum(p, axis=-1, keepdims=True)
        heads.append(_dot(p.astype(BF16), v_ref[:, sl]).astype(BF16))
    o = jnp.concatenate(heads, axis=-1)
    o_ref[...] = x + _dot(o, wo_ref[...])


def _xattn(x, g, wq, kx, vx, wo, *, tx):
    B, T, D = x.shape
    row = lambda b, i: (b, i, 0)
    kv = pl.BlockSpec((None, MEM_LEN, XA_DIM), lambda b, i: (b, 0, 0))
    return pl.pallas_call(
        _xattn_body,
        grid=(B, T // tx),
        in_specs=[pl.BlockSpec((None, tx, D), row), _const_spec((1, D)),
                  _const_spec(wq.shape), kv, kv, _const_spec(wo.shape)],
        out_specs=pl.BlockSpec((None, tx, D), row),
        out_shape=jax.ShapeDtypeStruct((B, T, D), F32),
        compiler_params=pltpu.CompilerParams(
            dimension_semantics=("parallel", "parallel"), vmem_limit_bytes=VMEM_LIMIT),
        name="xattn",
    )(x, g, wq, kx, vx, wo)


ROUTE_LANES = 128


def _first_max_lane(v, valid, lane):
    neg = jnp.where(valid, v, -jnp.inf)
    mx = jnp.max(neg, axis=-1, keepdims=True)
    idx = jnp.min(jnp.where(valid & (neg == mx), lane, ROUTE_LANES), axis=-1, keepdims=True)
    return mx, idx


def _route(logits):
    lane = lax.broadcasted_iota(jnp.int32, logits.shape, 1)
    is_g = lane < N_GROUPS
    gmax, g_star = _first_max_lane(logits, is_g, lane)
    p_gsel = 1.0 / jnp.sum(jnp.where(is_g, jnp.exp(logits - gmax), 0.0), axis=-1, keepdims=True)
    e_lane0 = N_GROUPS + g_star * EXPERTS_PER_GROUP
    in_grp = (lane >= e_lane0) & (lane < e_lane0 + EXPERTS_PER_GROUP)
    emax, _ = _first_max_lane(logits, in_grp, lane)
    ex = jnp.where(in_grp, jnp.exp(logits - emax), 0.0)
    p_exp = ex / jnp.sum(ex, axis=-1, keepdims=True)
    v1, i1 = _first_max_lane(p_exp, in_grp, lane)
    v2, i2 = _first_max_lane(p_exp, in_grp & (lane != i1), lane)
    tot = v1 + v2
    return jnp.where(lane == i1, p_gsel * (v1 / tot), jnp.where(lane == i2, p_gsel * (v2 / tot), 0.0))


def _moe_body(x_ref, g_ref, wr_ref, br_ref, wg_ref, wu_ref, wd_ref, gf_ref, o_ref,
              hm_ref, comb_ref, acc_ref, *, sub):
    grp = pl.program_id(1)
    tm = x_ref.shape[0]

    @pl.when(grp == 0)
    def _():
        hm = _rms(x_ref[...], g_ref[...]).astype(BF16)
        hm_ref[...] = hm
        comb_ref[...] = _route(_dot(hm, wr_ref[...]) + br_ref[...])
        acc_ref[...] = jnp.zeros_like(acc_ref)

    lane = lax.broadcasted_iota(jnp.int32, (sub, ROUTE_LANES), 1)
    for r in range(tm // sub):
        rows = slice(r * sub, (r + 1) * sub)
        hm = hm_ref[rows, :]
        comb = comb_ref[rows, :]
        y = jnp.zeros((sub, D_MODEL), F32)
        for e in range(EXPERTS_PER_GROUP):
            w = jnp.sum(jnp.where(lane == N_GROUPS + grp * EXPERTS_PER_GROUP + e, comb, 0.0),
                        axis=-1, keepdims=True)
            a = _dot(hm, wg_ref[e])
            b = _dot(hm, wu_ref[e])
            u = (a * _sigmoid(a)) * b * w
            y = y + _dot(u.astype(BF16), wd_ref[e])
        acc_ref[rows, :] += y

    @pl.when(grp == N_GROUPS - 1)
    def _():
        o_ref[...] = _rms(x_ref[...] + acc_ref[...], gf_ref[...])


def _moe(x, g, wr, br, wg, wu, wd, gf, *, tm, sub):
    N, D = x.shape
    tok = lambda i, j: (i, 0)
    grp = lambda i, j: (j, 0, 0)
    E, FF = EXPERTS_PER_GROUP, EXPERT_FF
    return pl.pallas_call(
        functools.partial(_moe_body, sub=sub),
        grid=(N // tm, N_GROUPS),
        in_specs=[
            pl.BlockSpec((tm, D), tok), _const_spec((1, D)), _const_spec(wr.shape),
            _const_spec(br.shape),
            pl.BlockSpec((E, D, FF), grp), pl.BlockSpec((E, D, FF), grp),
            pl.BlockSpec((E, FF, D), grp), _const_spec((1, D)),
        ],
        out_specs=pl.BlockSpec((tm, D), tok),
        out_shape=jax.ShapeDtypeStruct((N, D), F32),
        scratch_shapes=[pltpu.VMEM((tm, D), BF16), pltpu.VMEM((tm, ROUTE_LANES), F32),
                        pltpu.VMEM((tm, D), F32)],
        compiler_params=pltpu.CompilerParams(
            dimension_semantics=("parallel", "arbitrary"), vmem_limit_bytes=VMEM_LIMIT),
        name="moe",
    )(x, g, wr, br, wg, wu, wd, gf)


def kernel(x, mem, norm_mix_g, w_in, fox_bf, conv_dw_w, conv_dw_b, conv_ln_g, conv_ln_b, conv_pw_w, attn_branch_w, w_out, norm_xa_g, norm_mem_g, xa_wq, xa_wk, xa_wv, xa_wo, norm_moe_g, router_group_w, router_group_b, router_expert_w, router_expert_b, expert_w_gate, expert_w_up, expert_w_down, norm_final_g):
    B, T, D = x.shape
    depth = w_in.shape[0]
    assert depth == 1, "the final norm is fused into the last layer's MoE kernel"
    row = lambda v: v.reshape(1, -1).astype(F32)
    c_glu = 2 * CONV_DIM
    c_q, c_f = c_glu, c_glu + 3 * FOX_DIM
    c_gc = c_f + FOX_HEADS
    c_ga = c_gc + D
    for l in range(depth):
        w = w_in[l]
        u, gc, ga, qkv_t, f_t = _proj_in(
            x, row(norm_mix_g[l]),
            w[:, :c_glu].astype(BF16), w[:, c_gc:c_ga].astype(BF16), w[:, c_ga:].astype(BF16),
            w[:, c_q:c_f].T.astype(BF16), w[:, c_f:c_gc].T.astype(BF16), tm=512)
        qaux, kaux = _fox_prep(f_t, fox_bf[l].reshape(FOX_HEADS, 1).astype(F32))
        attn_t = _fox_attn(qkv_t, qaux, kaux, tq=256)
        x = _mix_out(x, u, gc, ga, attn_t, conv_dw_w[l].astype(F32), row(conv_dw_b[l]),
                     row(conv_ln_g[l]), row(conv_ln_b[l]), conv_pw_w[l].astype(BF16),
                     attn_branch_w[l].astype(BF16), w_out[l].astype(BF16), tc=256)
        kx, vx = _mem_kv(mem, row(norm_mem_g[l]), xa_wk[l].astype(BF16), xa_wv[l].astype(BF16))
        x = _xattn(x, row(norm_xa_g[l]), xa_wq[l].astype(BF16), kx, vx, xa_wo[l].astype(BF16),
                   tx=256)
        wr = jnp.zeros((D, ROUTE_LANES), F32)
        wr = wr.at[:, :N_GROUPS].set(router_group_w[l]).at[:, N_GROUPS:N_GROUPS + N_EXPERTS].set(
            router_expert_w[l])
        br = jnp.zeros((1, ROUTE_LANES), F32)
        br = br.at[0, :N_GROUPS].set(router_group_b[l]).at[0, N_GROUPS:N_GROUPS + N_EXPERTS].set(
            router_expert_b[l])
        x = _moe(x.reshape(B * T, D), row(norm_moe_g[l]), wr.astype(BF16), br,
                 expert_w_gate[l].astype(BF16), expert_w_up[l].astype(BF16),
                 expert_w_down[l].astype(BF16), row(norm_final_g), tm=1024,
                 sub=256).reshape(B, T, D)
    return x
```

```python
import functools

import jax
import jax.numpy as jnp
from jax import lax
from jax.experimental import pallas as pl
from jax.experimental.pallas import tpu as pltpu

D_MODEL = 1024
MEM_LEN = 256
EPS = 1e-6
CONV_DIM = 512
CONV_WIDTH = 31
FOX_HEADS = 8
FOX_HEAD_DIM = 64
FOX_DIM = FOX_HEADS * FOX_HEAD_DIM
XA_HEADS = 4
XA_HEAD_DIM = 128
XA_DIM = XA_HEADS * XA_HEAD_DIM
N_GROUPS = 4
EXPERTS_PER_GROUP = 4
N_EXPERTS = N_GROUPS * EXPERTS_PER_GROUP
EXPERT_FF = 256

AUX_ROWS = 16
VMEM_LIMIT = 56 * 1024 * 1024

F32 = jnp.float32
BF16 = jnp.bfloat16


def _const_spec(shape):
    n = len(shape)
    return pl.BlockSpec(shape, lambda *_: (0,) * n)


def _rms(x, g):
    return x * lax.rsqrt(jnp.mean(x * x, axis=-1, keepdims=True) + EPS) * g


def _sigmoid(x):
    return 1.0 / (1.0 + jnp.exp(-x))


def _dot(a, b):
    return jnp.dot(a, b, preferred_element_type=F32)


def _dot_nt(a, b):
    return lax.dot_general(a, b, (((1,), (1,)), ((), ())), preferred_element_type=F32)


def _dot_tn(a, b):
    return lax.dot_general(a, b, (((0,), (0,)), ((), ())), preferred_element_type=F32)


def _proj_in_body(x_ref, g_ref, wglu_ref, wgc_ref, wga_ref, wqkv_t_ref, wf_t_ref,
                  u_ref, gc_ref, ga_ref, qkv_t_ref, f_t_ref):
    h = _rms(x_ref[...], g_ref[...]).astype(BF16)
    glu = _dot(h, wglu_ref[...])
    u_ref[...] = glu[:, :CONV_DIM] * _sigmoid(glu[:, CONV_DIM:])
    gc_ref[...] = _dot(h, wgc_ref[...]).astype(BF16)
    ga_ref[...] = _dot(h, wga_ref[...]).astype(BF16)
    qkv_t = _dot_nt(wqkv_t_ref[...], h)
    scale = FOX_HEAD_DIM ** -0.5
    qkv_t_ref[:FOX_DIM, :] = (qkv_t[:FOX_DIM] * scale).astype(BF16)
    qkv_t_ref[FOX_DIM:, :] = qkv_t[FOX_DIM:].astype(BF16)
    f_t_ref[...] = _dot_nt(wf_t_ref[...], h)


def _proj_in(x, g, wglu, wgc, wga, wqkv_t, wf_t, *, tm):
    B, T, D = x.shape
    grid = (B, T // tm)
    row = lambda b, i: (b, i, 0)
    col = lambda b, i: (b, 0, i)
    return pl.pallas_call(
        _proj_in_body,
        grid=grid,
        in_specs=[
            pl.BlockSpec((None, tm, D), row),
            _const_spec((1, D)),
            _const_spec(wglu.shape), _const_spec(wgc.shape), _const_spec(wga.shape),
            _const_spec(wqkv_t.shape), _const_spec(wf_t.shape),
        ],
        out_specs=[
            pl.BlockSpec((None, tm, CONV_DIM), row),
            pl.BlockSpec((None, tm, D), row),
            pl.BlockSpec((None, tm, D), row),
            pl.BlockSpec((None, 3 * FOX_DIM, tm), col),
            pl.BlockSpec((None, FOX_HEADS, tm), col),
        ],
        out_shape=[
            jax.ShapeDtypeStruct((B, T, CONV_DIM), F32),
            jax.ShapeDtypeStruct((B, T, D), BF16),
            jax.ShapeDtypeStruct((B, T, D), BF16),
            jax.ShapeDtypeStruct((B, 3 * FOX_DIM, T), BF16),
            jax.ShapeDtypeStruct((B, FOX_HEADS, T), F32),
        ],
        compiler_params=pltpu.CompilerParams(
            dimension_semantics=("parallel", "parallel"), vmem_limit_bytes=VMEM_LIMIT),
        name="proj_in",
    )(x, g, wglu, wgc, wga, wqkv_t, wf_t)


def _split3(c):
    hi = c.astype(BF16)
    r = c - hi.astype(F32)
    mid = r.astype(BF16)
    lo = (r - mid.astype(F32)).astype(BF16)
    return hi, mid, lo


def _fox_prep_body(f_t_ref, bf_ref, qaux_ref, kaux_ref, c_ref, *, chunk):
    T = f_t_ref.shape[-1]
    z = f_t_ref[...] + bf_ref[...]
    logf = jnp.minimum(z, 0.0) - jnp.log1p(jnp.exp(-jnp.abs(z)))
    r_i = lax.broadcasted_iota(jnp.int32, (chunk, chunk), 0)
    c_i = lax.broadcasted_iota(jnp.int32, (chunk, chunk), 1)
    tri = jnp.where(r_i <= c_i, 1.0, 0.0).astype(BF16)
    carry = jnp.zeros((FOX_HEADS, 1), F32)
    for j in range(T // chunk):
        blk = logf[:, j * chunk:(j + 1) * chunk]
        hi, mid, lo = _split3(blk)
        cs = (_dot(lo, tri) + _dot(mid, tri)) + _dot(hi, tri) + carry
        c_ref[:, j * chunk:(j + 1) * chunk] = cs
        carry = cs[:, chunk - 1:chunk]
    row = lax.broadcasted_iota(jnp.int32, (AUX_ROWS, T), 0)
    for h in range(FOX_HEADS):
        hi, mid, lo = _split3(c_ref[h:h + 1, :])
        pieces = [jnp.broadcast_to(p.astype(F32), (AUX_ROWS, T)) for p in (hi, mid, lo)]
        one = jnp.where(row < 3, 1.0, 0.0)
        val = jnp.where(row == 0, pieces[0], jnp.where(row == 1, pieces[1], pieces[2]))
        val_k = jnp.where(row == 3, pieces[0], jnp.where(row == 4, pieces[1], pieces[2]))
        qaux = jnp.where(row < 3, val, jnp.where(row < 6, 1.0, 0.0))
        kaux = jnp.where(row < 3, one, jnp.where(row < 6, -val_k, 0.0))
        qaux_ref[h] = qaux.astype(BF16)
        kaux_ref[h] = kaux.astype(BF16)


def _fox_prep(f_t, bf):
    B, H, T = f_t.shape
    aux = jax.ShapeDtypeStruct((B, H, AUX_ROWS, T), BF16)
    aux_spec = pl.BlockSpec((None, H, AUX_ROWS, T), lambda b: (b, 0, 0, 0))
    return pl.pallas_call(
        functools.partial(_fox_prep_body, chunk=512),
        grid=(B,),
        in_specs=[pl.BlockSpec((None, H, T), lambda b: (b, 0, 0)), _const_spec((H, 1))],
        out_specs=[aux_spec, aux_spec],
        out_shape=[aux, aux],
        scratch_shapes=[pltpu.VMEM((H, T), F32)],
        compiler_params=pltpu.CompilerParams(dimension_semantics=("parallel",)),
        name="fox_prep",
    )(f_t, bf)


MASK_BLOCK = 256


def _fox_attn_body(q_ref, k_ref, v_ref, qaux_ref, kaux_ref, o_ref, *, tq):
    T = q_ref.shape[-1]
    dh = FOX_HEAD_DIM
    mb = MASK_BLOCK
    nb = tq // mb
    kk = lax.broadcasted_iota(jnp.int32, (mb, mb), 0)
    qq = lax.broadcasted_iota(jnp.int32, (mb, mb), 1)
    causal = kk <= qq
    ones = jnp.where(lax.broadcasted_iota(jnp.int32, (AUX_ROWS, tq), 0) == 0, 1.0, 0.0).astype(BF16)

    def mask_diagonal(s):
        cols = []
        for bj in range(nb):
            blocks = []
            for bi in range(nb):
                blk = s[bi * mb:(bi + 1) * mb, bj * mb:(bj + 1) * mb]
                if bi == bj:
                    blk = jnp.where(causal, blk, -jnp.inf)
                elif bi > bj:
                    blk = jnp.full((mb, mb), -jnp.inf, F32)
                blocks.append(blk)
            cols.append(jnp.concatenate(blocks, axis=0))
        return jnp.concatenate(cols, axis=1)

    def attend(qa, k0, m, acc):
        ka = jnp.concatenate([k_ref[:, pl.ds(k0, tq)], kaux_ref[:, pl.ds(k0, tq)]], axis=0)
        va = jnp.concatenate([v_ref[:, pl.ds(k0, tq)], ones], axis=0)
        s = _dot_tn(ka, qa)
        if m is None:
            s = mask_diagonal(s)
        m_new = jnp.max(s, axis=0, keepdims=True)
        if m is not None:
            m_new = jnp.maximum(m, m_new)
        pv = _dot(va, jnp.exp(s - m_new).astype(BF16))
        if m is not None:
            pv = jnp.exp(m - m_new) * acc + pv
        return m_new, pv

    def q_body(qi, _):
        q0 = pl.multiple_of(qi * tq, tq)
        qa = jnp.concatenate([q_ref[:, pl.ds(q0, tq)], qaux_ref[:, pl.ds(q0, tq)]], axis=0)
        state = attend(qa, q0, None, None)
        m, acc = lax.fori_loop(
            0, qi, lambda c, st: attend(qa, pl.multiple_of(c * tq, tq), st[0], st[1]), state)
        o_ref[:, pl.ds(q0, tq)] = (acc[:dh] / acc[dh:dh + 1]).astype(o_ref.dtype)
        return 0

    lax.fori_loop(0, T // tq, q_body, 0)


def _fox_attn(qkv_t, qaux, kaux, *, tq):
    B, _, T = qkv_t.shape
    H, dh = FOX_HEADS, FOX_HEAD_DIM
    part = lambda off: pl.BlockSpec((None, dh, T), lambda b, h: (b, off * H + h, 0))
    aux = pl.BlockSpec((None, None, AUX_ROWS, T), lambda b, h: (b, h, 0, 0))
    return pl.pallas_call(
        functools.partial(_fox_attn_body, tq=tq),
        grid=(B, H),
        in_specs=[part(0), part(1), part(2), aux, aux],
        out_specs=pl.BlockSpec((None, dh, T), lambda b, h: (b, h, 0)),
        out_shape=jax.ShapeDtypeStruct((B, H * dh, T), BF16),
        compiler_params=pltpu.CompilerParams(
            dimension_semantics=("parallel", "parallel"), vmem_limit_bytes=VMEM_LIMIT),
        name="fox_attn",
    )(qkv_t, qkv_t, qkv_t, qaux, kaux)


HALO = 32


def _mix_out_body(x_ref, u_ref, uprev_ref, gc_ref, ga_ref, attn_t_ref, dww_ref, dwb_ref,
                  lng_ref, lnb_ref, pw_ref, abw_ref, wout_ref, o_ref, ubuf_ref, *, tc):
    first = pl.program_id(1) == 0
    ubuf_ref[:HALO, :] = jnp.where(first, 0.0, uprev_ref[...])
    ubuf_ref[HALO:, :] = u_ref[...]
    acc = jnp.zeros((tc, CONV_DIM), F32)
    for k in range(CONV_WIDTH):
        off = HALO - (CONV_WIDTH - 1) + k
        acc = acc + ubuf_ref[off:off + tc, :] * dww_ref[k:k + 1, :]
    acc = acc + dwb_ref[...]
    mu = jnp.mean(acc, axis=-1, keepdims=True)
    d = acc - mu
    var = jnp.mean(d * d, axis=-1, keepdims=True)
    y = d * lax.rsqrt(var + EPS) * lng_ref[...] + lnb_ref[...]
    y = (y * _sigmoid(y)).astype(BF16)
    conv_out = _dot(y, pw_ref[...])
    attn_out = _dot_tn(attn_t_ref[...], abw_ref[...])
    merged = (_sigmoid(gc_ref[...].astype(F32)) * conv_out
              + _sigmoid(ga_ref[...].astype(F32)) * attn_out)
    o_ref[...] = x_ref[...] + _dot(merged.astype(BF16), wout_ref[...])


def _mix_out(x, u, gc, ga, attn_t, dww, dwb, lng, lnb, pw, abw, wout, *, tc):
    B, T, D = x.shape
    row = lambda b, i: (b, i, 0)
    halo_blocks = tc // HALO
    return pl.pallas_call(
        functools.partial(_mix_out_body, tc=tc),
        grid=(B, T // tc),
        in_specs=[
            pl.BlockSpec((None, tc, D), row),
            pl.BlockSpec((None, tc, CONV_DIM), row),
            pl.BlockSpec((None, HALO, CONV_DIM),
                         lambda b, i: (b, jnp.maximum(i * halo_blocks - 1, 0), 0)),
            pl.BlockSpec((None, tc, D), row),
            pl.BlockSpec((None, tc, D), row),
            pl.BlockSpec((None, FOX_DIM, tc), lambda b, i: (b, 0, i)),
            _const_spec(dww.shape), _const_spec(dwb.shape), _const_spec(lng.shape),
            _const_spec(lnb.shape), _const_spec(pw.shape), _const_spec(abw.shape),
            _const_spec(wout.shape),
        ],
        out_specs=pl.BlockSpec((None, tc, D), row),
        out_shape=jax.ShapeDtypeStruct((B, T, D), F32),
        scratch_shapes=[pltpu.VMEM((HALO + tc, CONV_DIM), F32)],
        compiler_params=pltpu.CompilerParams(
            dimension_semantics=("parallel", "parallel"), vmem_limit_bytes=VMEM_LIMIT),
        name="mix_out",
    )(x, u, u, gc, ga, attn_t, dww, dwb, lng, lnb, pw, abw, wout)


def _mem_kv_body(m_ref, g_ref, wk_ref, wv_ref, k_ref, v_ref):
    h = _rms(m_ref[...], g_ref[...]).astype(BF16)
    k_ref[...] = _dot(h, wk_ref[...]).astype(BF16)
    v_ref[...] = _dot(h, wv_ref[...]).astype(BF16)


def _mem_kv(mem, g, wk, wv):
    B, M, D = mem.shape
    spec = pl.BlockSpec((None, M, XA_DIM), lambda b: (b, 0, 0))
    out = jax.ShapeDtypeStruct((B, M, XA_DIM), BF16)
    return pl.pallas_call(
        _mem_kv_body,
        grid=(B,),
        in_specs=[pl.BlockSpec((None, M, D), lambda b: (b, 0, 0)), _const_spec((1, D)),
                  _const_spec(wk.shape), _const_spec(wv.shape)],
        out_specs=[spec, spec],
        out_shape=[out, out],
        compiler_params=pltpu.CompilerParams(dimension_semantics=("parallel",)),
        name="mem_kv",
    )(mem, g, wk, wv)


def _xattn_body(x_ref, g_ref, wq_ref, k_ref, v_ref, wo_ref, o_ref):
    x = x_ref[...]
    q = _dot(_rms(x, g_ref[...]).astype(BF16), wq_ref[...]).astype(BF16)
    scale = XA_HEAD_DIM ** -0.5
    heads = []
    for h in range(XA_HEADS):
        sl = slice(h * XA_HEAD_DIM, (h + 1) * XA_HEAD_DIM)
        s = _dot_nt(q[:, sl], k_ref[:, sl]) * scale
        p = jnp.exp(s - jnp.max(s, axis=-1, keepdims=True))
        p = p / jnp.sum(p, axis=-1, keepdims=True)
        heads.append(_dot(p.astype(BF16), v_ref[:, sl]).astype(BF16))
    o = jnp.concatenate(heads, axis=-1)
    o_ref[...] = x + _dot(o, wo_ref[...])


def _xattn(x, g, wq, kx, vx, wo, *, tx):
    B, T, D = x.shape
    row = lambda b, i: (b, i, 0)
    kv = pl.BlockSpec((None, MEM_LEN, XA_DIM), lambda b, i: (b, 0, 0))
    return pl.pallas_call(
        _xattn_body,
        grid=(B, T // tx),
        in_specs=[pl.BlockSpec((None, tx, D), row), _const_spec((1, D)),
                  _const_spec(wq.shape), kv, kv, _const_spec(wo.shape)],
        out_specs=pl.BlockSpec((None, tx, D), row),
        out_shape=jax.ShapeDtypeStruct((B, T, D), F32),
        compiler_params=pltpu.CompilerParams(
            dimension_semantics=("parallel", "parallel"), vmem_limit_bytes=VMEM_LIMIT),
        name="xattn",
    )(x, g, wq, kx, vx, wo)


ROUTE_LANES = 128


def _first_max_lane(v, valid, lane):
    neg = jnp.where(valid, v, -jnp.inf)
    mx = jnp.max(neg, axis=-1, keepdims=True)
    idx = jnp.min(jnp.where(valid & (neg == mx), lane, ROUTE_LANES), axis=-1, keepdims=True)
    return mx, idx


def _route(logits):
    lane = lax.broadcasted_iota(jnp.int32, logits.shape, 1)
    is_g = lane < N_GROUPS
    gmax, g_star = _first_max_lane(logits, is_g, lane)
    p_gsel = 1.0 / jnp.sum(jnp.where(is_g, jnp.exp(logits - gmax), 0.0), axis=-1, keepdims=True)
    e_lane0 = N_GROUPS + g_star * EXPERTS_PER_GROUP
    in_grp = (lane >= e_lane0) & (lane < e_lane0 + EXPERTS_PER_GROUP)
    emax, _ = _first_max_lane(logits, in_grp, lane)
    ex = jnp.where(in_grp, jnp.exp(logits - emax), 0.0)
    p_exp = ex / jnp.sum(ex, axis=-1, keepdims=True)
    v1, i1 = _first_max_lane(p_exp, in_grp, lane)
    v2, i2 = _first_max_lane(p_exp, in_grp & (lane != i1), lane)
    tot = v1 + v2
    return jnp.where(lane == i1, p_gsel * (v1 / tot), jnp.where(lane == i2, p_gsel * (v2 / tot), 0.0))


def _moe_body(x_ref, g_ref, wr_ref, br_ref, wg_ref, wu_ref, wd_ref, gf_ref, o_ref,
              hm_ref, comb_ref, acc_ref, *, sub):
    grp = pl.program_id(1)
    tm = x_ref.shape[0]

    @pl.when(grp == 0)
    def _():
        hm = _rms(x_ref[...], g_ref[...]).astype(BF16)
        hm_ref[...] = hm
        comb_ref[...] = _route(_dot(hm, wr_ref[...]) + br_ref[...])
        acc_ref[...] = jnp.zeros_like(acc_ref)

    lane = lax.broadcasted_iota(jnp.int32, (sub, ROUTE_LANES), 1)
    for r in range(tm // sub):
        rows = slice(r * sub, (r + 1) * sub)
        hm = hm_ref[rows, :]
        comb = comb_ref[rows, :]
        y = jnp.zeros((sub, D_MODEL), F32)
        for e in range(EXPERTS_PER_GROUP):
            w = jnp.sum(jnp.where(lane == N_GROUPS + grp * EXPERTS_PER_GROUP + e, comb, 0.0),
                        axis=-1, keepdims=True)
            a = _dot(hm, wg_ref[e])
            b = _dot(hm, wu_ref[e])
            u = (a * _sigmoid(a)) * b * w
            y = y + _dot(u.astype(BF16), wd_ref[e])
        acc_ref[rows, :] += y

    @pl.when(grp == N_GROUPS - 1)
    def _():
        o_ref[...] = _rms(x_ref[...] + acc_ref[...], gf_ref[...])


def _moe(x, g, wr, br, wg, wu, wd, gf, *, tm, sub):
    N, D = x.shape
    tok = lambda i, j: (i, 0)
    grp = lambda i, j: (j, 0, 0)
    E, FF = EXPERTS_PER_GROUP, EXPERT_FF
    return pl.pallas_call(
        functools.partial(_moe_body, sub=sub),
        grid=(N // tm, N_GROUPS),
        in_specs=[
            pl.BlockSpec((tm, D), tok), _const_spec((1, D)), _const_spec(wr.shape),
            _const_spec(br.shape),
            pl.BlockSpec((E, D, FF), grp), pl.BlockSpec((E, D, FF), grp),
            pl.BlockSpec((E, FF, D), grp), _const_spec((1, D)),
        ],
        out_specs=pl.BlockSpec((tm, D), tok),
        out_shape=jax.ShapeDtypeStruct((N, D), F32),
        scratch_shapes=[pltpu.VMEM((tm, D), BF16), pltpu.VMEM((tm, ROUTE_LANES), F32),
                        pltpu.VMEM((tm, D), F32)],
        compiler_params=pltpu.CompilerParams(
            dimension_semantics=("parallel", "arbitrary"), vmem_limit_bytes=VMEM_LIMIT),
        name="moe",
    )(x, g, wr, br, wg, wu, wd, gf)


def kernel(x, mem, norm_mix_g, w_in, fox_bf, conv_dw_w, conv_dw_b, conv_ln_g, conv_ln_b, conv_pw_w, attn_branch_w, w_out, norm_xa_g, norm_mem_g, xa_wq, xa_wk, xa_wv, xa_wo, norm_moe_g, router_group_w, router_group_b, router_expert_w, router_expert_b, expert_w_gate, expert_w_up, expert_w_down, norm_final_g):
    B, T, D = x.shape
    depth = w_in.shape[0]
    assert depth == 1, "the final norm is fused into the last layer's MoE kernel"
    row = lambda v: v.reshape(1, -1).astype(F32)
    c_glu = 2 * CONV_DIM
    c_q, c_f = c_glu, c_glu + 3 * FOX_DIM
    c_gc = c_f + FOX_HEADS
    c_ga = c_gc + D
    for l in range(depth):
        w = w_in[l]
        u, gc, ga, qkv_t, f_t = _proj_in(
            x, row(norm_mix_g[l]),
            w[:, :c_glu].astype(BF16), w[:, c_gc:c_ga].astype(BF16), w[:, c_ga:].astype(BF16),
            w[:, c_q:c_f].T.astype(BF16), w[:, c_f:c_gc].T.astype(BF16), tm=512)
        qaux, kaux = _fox_prep(f_t, fox_bf[l].reshape(FOX_HEADS, 1).astype(F32))
        attn_t = _fox_attn(qkv_t, qaux, kaux, tq=1024)
        x = _mix_out(x, u, gc, ga, attn_t, conv_dw_w[l].astype(F32), row(conv_dw_b[l]),
                     row(conv_ln_g[l]), row(conv_ln_b[l]), conv_pw_w[l].astype(BF16),
                     attn_branch_w[l].astype(BF16), w_out[l].astype(BF16), tc=256)
        kx, vx = _mem_kv(mem, row(norm_mem_g[l]), xa_wk[l].astype(BF16), xa_wv[l].astype(BF16))
        x = _xattn(x, row(norm_xa_g[l]), xa_wq[l].astype(BF16), kx, vx, xa_wo[l].astype(BF16),
                   tx=256)
        wr = jnp.zeros((D, ROUTE_LANES), F32)
        wr = wr.at[:, :N_GROUPS].set(router_group_w[l]).at[:, N_GROUPS:N_GROUPS + N_EXPERTS].set(
            router_expert_w[l])
        br = jnp.zeros((1, ROUTE_LANES), F32)
        br = br.at[0, :N_GROUPS].set(router_group_b[l]).at[0, N_GROUPS:N_GROUPS + N_EXPERTS].set(
            router_expert_b[l])
        x = _moe(x.reshape(B * T, D), row(norm_moe_g[l]), wr.astype(BF16), br,
                 expert_w_gate[l].astype(BF16), expert_w_up[l].astype(BF16),
                 expert_w_down[l].astype(BF16), row(norm_final_g), tm=1024,
                 sub=256).reshape(B, T, D)
    return x
```

```python
import functools

import jax
import jax.numpy as jnp
from jax import lax
from jax.experimental import pallas as pl
from jax.experimental.pallas import tpu as pltpu

D_MODEL = 1024
MEM_LEN = 256
EPS = 1e-6
CONV_DIM = 512
CONV_WIDTH = 31
FOX_HEADS = 8
FOX_HEAD_DIM = 64
FOX_DIM = FOX_HEADS * FOX_HEAD_DIM
XA_HEADS = 4
XA_HEAD_DIM = 128
XA_DIM = XA_HEADS * XA_HEAD_DIM
N_GROUPS = 4
EXPERTS_PER_GROUP = 4
N_EXPERTS = N_GROUPS * EXPERTS_PER_GROUP
EXPERT_FF = 256

AUX_ROWS = 16
VMEM_LIMIT = 56 * 1024 * 1024

F32 = jnp.float32
BF16 = jnp.bfloat16


def _const_spec(shape):
    n = len(shape)
    return pl.BlockSpec(shape, lambda *_: (0,) * n)


def _rms(x, g):
    return x * lax.rsqrt(jnp.mean(x * x, axis=-1, keepdims=True) + EPS) * g


def _sigmoid(x):
    return 1.0 / (1.0 + jnp.exp(-x))


def _dot(a, b):
    return jnp.dot(a, b, preferred_element_type=F32)


def _dot_nt(a, b):
    return lax.dot_general(a, b, (((1,), (1,)), ((), ())), preferred_element_type=F32)


def _dot_tn(a, b):
    return lax.dot_general(a, b, (((0,), (0,)), ((), ())), preferred_element_type=F32)


def _proj_in_body(x_ref, g_ref, wglu_ref, wgc_ref, wga_ref, wqkv_t_ref, wf_t_ref,
                  u_ref, gc_ref, ga_ref, qkv_t_ref, f_t_ref):
    h = _rms(x_ref[...], g_ref[...]).astype(BF16)
    glu = _dot(h, wglu_ref[...])
    u_ref[...] = glu[:, :CONV_DIM] * _sigmoid(glu[:, CONV_DIM:])
    gc_ref[...] = _dot(h, wgc_ref[...]).astype(BF16)
    ga_ref[...] = _dot(h, wga_ref[...]).astype(BF16)
    qkv_t = _dot_nt(wqkv_t_ref[...], h)
    scale = FOX_HEAD_DIM ** -0.5
    qkv_t_ref[:FOX_DIM, :] = (qkv_t[:FOX_DIM] * scale).astype(BF16)
    qkv_t_ref[FOX_DIM:, :] = qkv_t[FOX_DIM:].astype(BF16)
    f_t_ref[...] = _dot_nt(wf_t_ref[...], h)


def _proj_in(x, g, wglu, wgc, wga, wqkv_t, wf_t, *, tm):
    B, T, D = x.shape
    grid = (B, T // tm)
    row = lambda b, i: (b, i, 0)
    col = lambda b, i: (b, 0, i)
    return pl.pallas_call(
        _proj_in_body,
        grid=grid,
        in_specs=[
            pl.BlockSpec((None, tm, D), row),
            _const_spec((1, D)),
            _const_spec(wglu.shape), _const_spec(wgc.shape), _const_spec(wga.shape),
            _const_spec(wqkv_t.shape), _const_spec(wf_t.shape),
        ],
        out_specs=[
            pl.BlockSpec((None, tm, CONV_DIM), row),
            pl.BlockSpec((None, tm, D), row),
            pl.BlockSpec((None, tm, D), row),
            pl.BlockSpec((None, 3 * FOX_DIM, tm), col),
            pl.BlockSpec((None, FOX_HEADS, tm), col),
        ],
        out_shape=[
            jax.ShapeDtypeStruct((B, T, CONV_DIM), F32),
            jax.ShapeDtypeStruct((B, T, D), BF16),
            jax.ShapeDtypeStruct((B, T, D), BF16),
            jax.ShapeDtypeStruct((B, 3 * FOX_DIM, T), BF16),
            jax.ShapeDtypeStruct((B, FOX_HEADS, T), F32),
        ],
        compiler_params=pltpu.CompilerParams(
            dimension_semantics=("parallel", "parallel"), vmem_limit_bytes=VMEM_LIMIT),
        name="proj_in",
    )(x, g, wglu, wgc, wga, wqkv_t, wf_t)


def _split3(c):
    hi = c.astype(BF16)
    r = c - hi.astype(F32)
    mid = r.astype(BF16)
    lo = (r - mid.astype(F32)).astype(BF16)
    return hi, mid, lo


def _fox_prep_body(f_t_ref, bf_ref, qaux_ref, kaux_ref, c_ref, *, chunk):
    T = f_t_ref.shape[-1]
    z = f_t_ref[...] + bf_ref[...]
    logf = jnp.minimum(z, 0.0) - jnp.log1p(jnp.exp(-jnp.abs(z)))
    r_i = lax.broadcasted_iota(jnp.int32, (chunk, chunk), 0)
    c_i = lax.broadcasted_iota(jnp.int32, (chunk, chunk), 1)
    tri = jnp.where(r_i <= c_i, 1.0, 0.0).astype(BF16)
    carry = jnp.zeros((FOX_HEADS, 1), F32)
    for j in range(T // chunk):
        blk = logf[:, j * chunk:(j + 1) * chunk]
        hi, mid, lo = _split3(blk)
        cs = (_dot(lo, tri) + _dot(mid, tri)) + _dot(hi, tri) + carry
        c_ref[:, j * chunk:(j + 1) * chunk] = cs
        carry = cs[:, chunk - 1:chunk]
    row = lax.broadcasted_iota(jnp.int32, (AUX_ROWS, T), 0)
    for h in range(FOX_HEADS):
        hi, mid, lo = _split3(c_ref[h:h + 1, :])
        pieces = [jnp.broadcast_to(p.astype(F32), (AUX_ROWS, T)) for p in (hi, mid, lo)]
        one = jnp.where(row < 3, 1.0, 0.0)
        val = jnp.where(row == 0, pieces[0], jnp.where(row == 1, pieces[1], pieces[2]))
        val_k = jnp.where(row == 3, pieces[0], jnp.where(row == 4, pieces[1], pieces[2]))
        qaux = jnp.where(row < 3, val, jnp.where(row < 6, 1.0, 0.0))
        kaux = jnp.where(row < 3, one, jnp.where(row < 6, -val_k, 0.0))
        qaux_ref[h] = qaux.astype(BF16)
        kaux_ref[h] = kaux.astype(BF16)


def _fox_prep(f_t, bf):
    B, H, T = f_t.shape
    aux = jax.ShapeDtypeStruct((B, H, AUX_ROWS, T), BF16)
    aux_spec = pl.BlockSpec((None, H, AUX_ROWS, T), lambda b: (b, 0, 0, 0))
    return pl.pallas_call(
        functools.partial(_fox_prep_body, chunk=512),
        grid=(B,),
        in_specs=[pl.BlockSpec((None, H, T), lambda b: (b, 0, 0)), _const_spec((H, 1))],
        out_specs=[aux_spec, aux_spec],
        out_shape=[aux, aux],
        scratch_shapes=[pltpu.VMEM((H, T), F32)],
        compiler_params=pltpu.CompilerParams(dimension_semantics=("parallel",)),
        name="fox_prep",
    )(f_t, bf)


MASK_BLOCK = 256


def _fox_attn_body(q_ref, k_ref, v_ref, qaux_ref, kaux_ref, o_ref, *, tq):
    T = q_ref.shape[-1]
    dh = FOX_HEAD_DIM
    mb = MASK_BLOCK
    nb = tq // mb
    kk = lax.broadcasted_iota(jnp.int32, (mb, mb), 0)
    qq = lax.broadcasted_iota(jnp.int32, (mb, mb), 1)
    causal = kk <= qq
    ones = jnp.where(lax.broadcasted_iota(jnp.int32, (AUX_ROWS, tq), 0) == 0, 1.0, 0.0).astype(BF16)

    def mask_diagonal(s):
        cols = []
        for bj in range(nb):
            blocks = []
            for bi in range(nb):
                blk = s[bi * mb:(bi + 1) * mb, bj * mb:(bj + 1) * mb]
                if bi == bj:
                    blk = jnp.where(causal, blk, -jnp.inf)
                elif bi > bj:
                    blk = jnp.full((mb, mb), -jnp.inf, F32)
                blocks.append(blk)
            cols.append(jnp.concatenate(blocks, axis=0))
        return jnp.concatenate(cols, axis=1)

    def attend(qa, k0, m, acc):
        ka = jnp.concatenate([k_ref[:, pl.ds(k0, tq)], kaux_ref[:, pl.ds(k0, tq)]], axis=0)
        va = jnp.concatenate([v_ref[:, pl.ds(k0, tq)], ones], axis=0)
        s = _dot_tn(ka, qa)
        if m is None:
            s = mask_diagonal(s)
        m_new = jnp.max(s, axis=0, keepdims=True)
        if m is not None:
            m_new = jnp.maximum(m, m_new)
        pv = _dot(va, jnp.exp(s - m_new).astype(BF16))
        if m is not None:
            pv = jnp.exp(m - m_new) * acc + pv
        return m_new, pv

    def q_body(qi, _):
        q0 = pl.multiple_of(qi * tq, tq)
        qa = jnp.concatenate([q_ref[:, pl.ds(q0, tq)], qaux_ref[:, pl.ds(q0, tq)]], axis=0)
        state = attend(qa, q0, None, None)
        m, acc = lax.fori_loop(
            0, qi, lambda c, st: attend(qa, pl.multiple_of(c * tq, tq), st[0], st[1]), state)
        o_ref[:, pl.ds(q0, tq)] = (acc[:dh] / acc[dh:dh + 1]).astype(o_ref.dtype)
        return 0

    lax.fori_loop(0, T // tq, q_body, 0)


def _fox_attn(qkv_t, qaux, kaux, *, tq):
    B, _, T = qkv_t.shape
    H, dh = FOX_HEADS, FOX_HEAD_DIM
    part = lambda off: pl.BlockSpec((None, dh, T), lambda b, h: (b, off * H + h, 0))
    aux = pl.BlockSpec((None, None, AUX_ROWS, T), lambda b, h: (b, h, 0, 0))
    return pl.pallas_call(
        functools.partial(_fox_attn_body, tq=tq),
        grid=(B, H),
        in_specs=[part(0), part(1), part(2), aux, aux],
        out_specs=pl.BlockSpec((None, dh, T), lambda b, h: (b, h, 0)),
        out_shape=jax.ShapeDtypeStruct((B, H * dh, T), BF16),
        compiler_params=pltpu.CompilerParams(
            dimension_semantics=("parallel", "parallel"), vmem_limit_bytes=VMEM_LIMIT),
        name="fox_attn",
    )(qkv_t, qkv_t, qkv_t, qaux, kaux)


HALO = 32
SUBLANES = 8


def _mix_out_body(x_ref, u_ref, uprev_ref, gc_ref, ga_ref, attn_t_ref, dww_ref, dwb_ref,
                  lng_ref, lnb_ref, pw_ref, abw_ref, wout_ref, o_ref, ubuf_ref, ushift_ref,
                  *, tc):
    first = pl.program_id(1) == 0
    ubuf_ref[:HALO, :] = jnp.where(first, 0.0, uprev_ref[...])
    ubuf_ref[HALO:, :] = u_ref[...]
    span = tc + HALO - SUBLANES
    for s in range(1, SUBLANES):
        ushift_ref[s - 1, :, :] = ubuf_ref[s:s + span, :]
    acc = jnp.zeros((tc, CONV_DIM), F32)
    for k in range(CONV_WIDTH):
        off = HALO - (CONV_WIDTH - 1) + k
        s, base = off % SUBLANES, off - off % SUBLANES
        rows = ubuf_ref[base:base + tc, :] if s == 0 else ushift_ref[s - 1, base:base + tc, :]
        acc = acc + rows * dww_ref[k:k + 1, :]
    acc = acc + dwb_ref[...]
    mu = jnp.mean(acc, axis=-1, keepdims=True)
    d = acc - mu
    var = jnp.mean(d * d, axis=-1, keepdims=True)
    y = d * lax.rsqrt(var + EPS) * lng_ref[...] + lnb_ref[...]
    y = (y * _sigmoid(y)).astype(BF16)
    conv_out = _dot(y, pw_ref[...])
    attn_out = _dot_tn(attn_t_ref[...], abw_ref[...])
    merged = (_sigmoid(gc_ref[...].astype(F32)) * conv_out
              + _sigmoid(ga_ref[...].astype(F32)) * attn_out)
    o_ref[...] = x_ref[...] + _dot(merged.astype(BF16), wout_ref[...])


def _mix_out(x, u, gc, ga, attn_t, dww, dwb, lng, lnb, pw, abw, wout, *, tc):
    B, T, D = x.shape
    row = lambda b, i: (b, i, 0)
    halo_blocks = tc // HALO
    return pl.pallas_call(
        functools.partial(_mix_out_body, tc=tc),
        grid=(B, T // tc),
        in_specs=[
            pl.BlockSpec((None, tc, D), row),
            pl.BlockSpec((None, tc, CONV_DIM), row),
            pl.BlockSpec((None, HALO, CONV_DIM),
                         lambda b, i: (b, jnp.maximum(i * halo_blocks - 1, 0), 0)),
            pl.BlockSpec((None, tc, D), row),
            pl.BlockSpec((None, tc, D), row),
            pl.BlockSpec((None, FOX_DIM, tc), lambda b, i: (b, 0, i)),
            _const_spec(dww.shape), _const_spec(dwb.shape), _const_spec(lng.shape),
            _const_spec(lnb.shape), _const_spec(pw.shape), _const_spec(abw.shape),
            _const_spec(wout.shape),
        ],
        out_specs=pl.BlockSpec((None, tc, D), row),
        out_shape=jax.ShapeDtypeStruct((B, T, D), F32),
        scratch_shapes=[pltpu.VMEM((HALO + tc, CONV_DIM), F32),
                        pltpu.VMEM((SUBLANES - 1, HALO + tc - SUBLANES, CONV_DIM), F32)],
        compiler_params=pltpu.CompilerParams(
            dimension_semantics=("parallel", "parallel"), vmem_limit_bytes=VMEM_LIMIT),
        name="mix_out",
    )(x, u, u, gc, ga, attn_t, dww, dwb, lng, lnb, pw, abw, wout)


def _mem_kv_body(m_ref, g_ref, wk_ref, wv_ref, k_ref, v_ref):
    h = _rms(m_ref[...], g_ref[...]).astype(BF16)
    k_ref[...] = _dot(h, wk_ref[...]).astype(BF16)
    v_ref[...] = _dot(h, wv_ref[...]).astype(BF16)


def _mem_kv(mem, g, wk, wv):
    B, M, D = mem.shape
    spec = pl.BlockSpec((None, M, XA_DIM), lambda b: (b, 0, 0))
    out = jax.ShapeDtypeStruct((B, M, XA_DIM), BF16)
    return pl.pallas_call(
        _mem_kv_body,
        grid=(B,),
        in_specs=[pl.BlockSpec((None, M, D), lambda b: (b, 0, 0)), _const_spec((1, D)),
                  _const_spec(wk.shape), _const_spec(wv.shape)],
        out_specs=[spec, spec],
        out_shape=[out, out],
        compiler_params=pltpu.CompilerParams(dimension_semantics=("parallel",)),
        name="mem_kv",
    )(mem, g, wk, wv)


def _xattn_body(x_ref, g_ref, wq_ref, k_ref, v_ref, wo_ref, o_ref):
    x = x_ref[...]
    q = _dot(_rms(x, g_ref[...]).astype(BF16), wq_ref[...]).astype(BF16)
    scale = XA_HEAD_DIM ** -0.5
    heads = []
    for h in range(XA_HEADS):
        sl = slice(h * XA_HEAD_DIM, (h + 1) * XA_HEAD_DIM)
        s = _dot_nt(q[:, sl], k_ref[:, sl]) * scale
        p = jnp.exp(s - jnp.max(s, axis=-1, keepdims=True))
        p = p / jnp.sum(p, axis=-1, keepdims=True)
        heads.append(_dot(p.astype(BF16), v_ref[:, sl]).astype(BF16))
    o = jnp.concatenate(heads, axis=-1)
    o_ref[...] = x + _dot(o, wo_ref[...])


def _xattn(x, g, wq, kx, vx, wo, *, tx):
    B, T, D = x.shape
    row = lambda b, i: (b, i, 0)
    kv = pl.BlockSpec((None, MEM_LEN, XA_DIM), lambda b, i: (b, 0, 0))
    return pl.pallas_call(
        _xattn_body,
        grid=(B, T // tx),
        in_specs=[pl.BlockSpec((None, tx, D), row), _const_spec((1, D)),
                  _const_spec(wq.shape), kv, kv, _const_spec(wo.shape)],
        out_specs=pl.BlockSpec((None, tx, D), row),
        out_shape=jax.ShapeDtypeStruct((B, T, D), F32),
        compiler_params=pltpu.CompilerParams(
            dimension_semantics=("parallel", "parallel"), vmem_limit_bytes=VMEM_LIMIT),
        name="xattn",
    )(x, g, wq, kx, vx, wo)


ROUTE_LANES = 128


def _first_max_lane(v, valid, lane):
    neg = jnp.where(valid, v, -jnp.inf)
    mx = jnp.max(neg, axis=-1, keepdims=True)
    idx = jnp.min(jnp.where(valid & (neg == mx), lane, ROUTE_LANES), axis=-1, keepdims=True)
    return mx, idx


def _route(logits):
    lane = lax.broadcasted_iota(jnp.int32, logits.shape, 1)
    is_g = lane < N_GROUPS
    gmax, g_star = _first_max_lane(logits, is_g, lane)
    p_gsel = 1.0 / jnp.sum(jnp.where(is_g, jnp.exp(logits - gmax), 0.0), axis=-1, keepdims=True)
    e_lane0 = N_GROUPS + g_star * EXPERTS_PER_GROUP
    in_grp = (lane >= e_lane0) & (lane < e_lane0 + EXPERTS_PER_GROUP)
    emax, _ = _first_max_lane(logits, in_grp, lane)
    ex = jnp.where(in_grp, jnp.exp(logits - emax), 0.0)
    p_exp = ex / jnp.sum(ex, axis=-1, keepdims=True)
    v1, i1 = _first_max_lane(p_exp, in_grp, lane)
    v2, i2 = _first_max_lane(p_exp, in_grp & (lane != i1), lane)
    tot = v1 + v2
    return jnp.where(lane == i1, p_gsel * (v1 / tot), jnp.where(lane == i2, p_gsel * (v2 / tot), 0.0))


def _moe_body(x_ref, g_ref, wr_ref, br_ref, wg_ref, wu_ref, wd_ref, gf_ref, o_ref,
              hm_ref, comb_ref, acc_ref, *, sub):
    grp = pl.program_id(1)
    tm = x_ref.shape[0]

    @pl.when(grp == 0)
    def _():
        hm = _rms(x_ref[...], g_ref[...]).astype(BF16)
        hm_ref[...] = hm
        comb_ref[...] = _route(_dot(hm, wr_ref[...]) + br_ref[...])
        acc_ref[...] = jnp.zeros_like(acc_ref)

    lane = lax.broadcasted_iota(jnp.int32, (sub, ROUTE_LANES), 1)
    for r in range(tm // sub):
        rows = slice(r * sub, (r + 1) * sub)
        hm = hm_ref[rows, :]
        comb = comb_ref[rows, :]
        y = jnp.zeros((sub, D_MODEL), F32)
        for e in range(EXPERTS_PER_GROUP):
            w = jnp.sum(jnp.where(lane == N_GROUPS + grp * EXPERTS_PER_GROUP + e, comb, 0.0),
                        axis=-1, keepdims=True)
            a = _dot(hm, wg_ref[e])
            b = _dot(hm, wu_ref[e])
            u = (a * _sigmoid(a)) * b * w
            y = y + _dot(u.astype(BF16), wd_ref[e])
        acc_ref[rows, :] += y

    @pl.when(grp == N_GROUPS - 1)
    def _():
        o_ref[...] = _rms(x_ref[...] + acc_ref[...], gf_ref[...])


def _moe(x, g, wr, br, wg, wu, wd, gf, *, tm, sub):
    N, D = x.shape
    tok = lambda i, j: (i, 0)
    grp = lambda i, j: (j, 0, 0)
    E, FF = EXPERTS_PER_GROUP, EXPERT_FF
    return pl.pallas_call(
        functools.partial(_moe_body, sub=sub),
        grid=(N // tm, N_GROUPS),
        in_specs=[
            pl.BlockSpec((tm, D), tok), _const_spec((1, D)), _const_spec(wr.shape),
            _const_spec(br.shape),
            pl.BlockSpec((E, D, FF), grp), pl.BlockSpec((E, D, FF), grp),
            pl.BlockSpec((E, FF, D), grp), _const_spec((1, D)),
        ],
        out_specs=pl.BlockSpec((tm, D), tok),
        out_shape=jax.ShapeDtypeStruct((N, D), F32),
        scratch_shapes=[pltpu.VMEM((tm, D), BF16), pltpu.VMEM((tm, ROUTE_LANES), F32),
                        pltpu.VMEM((tm, D), F32)],
        compiler_params=pltpu.CompilerParams(
            dimension_semantics=("parallel", "arbitrary"), vmem_limit_bytes=VMEM_LIMIT),
        name="moe",
    )(x, g, wr, br, wg, wu, wd, gf)


def kernel(x, mem, norm_mix_g, w_in, fox_bf, conv_dw_w, conv_dw_b, conv_ln_g, conv_ln_b, conv_pw_w, attn_branch_w, w_out, norm_xa_g, norm_mem_g, xa_wq, xa_wk, xa_wv, xa_wo, norm_moe_g, router_group_w, router_group_b, router_expert_w, router_expert_b, expert_w_gate, expert_w_up, expert_w_down, norm_final_g):
    B, T, D = x.shape
    depth = w_in.shape[0]
    assert depth == 1, "the final norm is fused into the last layer's MoE kernel"
    row = lambda v: v.reshape(1, -1).astype(F32)
    c_glu = 2 * CONV_DIM
    c_q, c_f = c_glu, c_glu + 3 * FOX_DIM
    c_gc = c_f + FOX_HEADS
    c_ga = c_gc + D
    for l in range(depth):
        w = w_in[l]
        u, gc, ga, qkv_t, f_t = _proj_in(
            x, row(norm_mix_g[l]),
            w[:, :c_glu].astype(BF16), w[:, c_gc:c_ga].astype(BF16), w[:, c_ga:].astype(BF16),
            w[:, c_q:c_f].T.astype(BF16), w[:, c_f:c_gc].T.astype(BF16), tm=512)
        qaux, kaux = _fox_prep(f_t, fox_bf[l].reshape(FOX_HEADS, 1).astype(F32))
        attn_t = _fox_attn(qkv_t, qaux, kaux, tq=1024)
        x = _mix_out(x, u, gc, ga, attn_t, conv_dw_w[l].astype(F32), row(conv_dw_b[l]),
                     row(conv_ln_g[l]), row(conv_ln_b[l]), conv_pw_w[l].astype(BF16),
                     attn_branch_w[l].astype(BF16), w_out[l].astype(BF16), tc=256)
        kx, vx = _mem_kv(mem, row(norm_mem_g[l]), xa_wk[l].astype(BF16), xa_wv[l].astype(BF16))
        x = _xattn(x, row(norm_xa_g[l]), xa_wq[l].astype(BF16), kx, vx, xa_wo[l].astype(BF16),
                   tx=512)
        wr = jnp.zeros((D, ROUTE_LANES), F32)
        wr = wr.at[:, :N_GROUPS].set(router_group_w[l]).at[:, N_GROUPS:N_GROUPS + N_EXPERTS].set(
            router_expert_w[l])
        br = jnp.zeros((1, ROUTE_LANES), F32)
        br = br.at[0, :N_GROUPS].set(router_group_b[l]).at[0, N_GROUPS:N_GROUPS + N_EXPERTS].set(
            router_expert_b[l])
        x = _moe(x.reshape(B * T, D), row(norm_moe_g[l]), wr.astype(BF16), br,
                 expert_w_gate[l].astype(BF16), expert_w_up[l].astype(BF16),
                 expert_w_down[l].astype(BF16), row(norm_final_g), tm=1024,
                 sub=256).reshape(B, T, D)
    return x
```

```python
import functools

import jax
import jax.numpy as jnp
from jax import lax
from jax.experimental import pallas as pl
from jax.experimental.pallas import tpu as pltpu

D_MODEL = 1024
MEM_LEN = 256
EPS = 1e-6
CONV_DIM = 512
CONV_WIDTH = 31
FOX_HEADS = 8
FOX_HEAD_DIM = 64
FOX_DIM = FOX_HEADS * FOX_HEAD_DIM
XA_HEADS = 4
XA_HEAD_DIM = 128
XA_DIM = XA_HEADS * XA_HEAD_DIM
N_GROUPS = 4
EXPERTS_PER_GROUP = 4
N_EXPERTS = N_GROUPS * EXPERTS_PER_GROUP
EXPERT_FF = 256

AUX_ROWS = 16
VMEM_LIMIT = 56 * 1024 * 1024

F32 = jnp.float32
BF16 = jnp.bfloat16


def _const_spec(shape):
    n = len(shape)
    return pl.BlockSpec(shape, lambda *_: (0,) * n)


def _rms(x, g):
    return x * lax.rsqrt(jnp.mean(x * x, axis=-1, keepdims=True) + EPS) * g


def _sigmoid(x):
    return 1.0 / (1.0 + jnp.exp(-x))


def _dot(a, b):
    return jnp.dot(a, b, preferred_element_type=F32)


def _dot_nt(a, b):
    return lax.dot_general(a, b, (((1,), (1,)), ((), ())), preferred_element_type=F32)


def _dot_tn(a, b):
    return lax.dot_general(a, b, (((0,), (0,)), ((), ())), preferred_element_type=F32)


def _proj_in_body(x_ref, g_ref, wglu_ref, wgc_ref, wga_ref, wqkv_t_ref, wf_t_ref,
                  u_ref, gc_ref, ga_ref, qkv_t_ref, f_t_ref):
    h = _rms(x_ref[...], g_ref[...]).astype(BF16)
    glu = _dot(h, wglu_ref[...])
    u_ref[...] = glu[:, :CONV_DIM] * _sigmoid(glu[:, CONV_DIM:])
    gc_ref[...] = _dot(h, wgc_ref[...]).astype(BF16)
    ga_ref[...] = _dot(h, wga_ref[...]).astype(BF16)
    qkv_t = _dot_nt(wqkv_t_ref[...], h)
    scale = FOX_HEAD_DIM ** -0.5
    qkv_t_ref[:FOX_DIM, :] = (qkv_t[:FOX_DIM] * scale).astype(BF16)
    qkv_t_ref[FOX_DIM:, :] = qkv_t[FOX_DIM:].astype(BF16)
    f_t_ref[...] = _dot_nt(wf_t_ref[...], h)


def _proj_in(x, g, wglu, wgc, wga, wqkv_t, wf_t, *, tm):
    B, T, D = x.shape
    grid = (B, T // tm)
    row = lambda b, i: (b, i, 0)
    col = lambda b, i: (b, 0, i)
    return pl.pallas_call(
        _proj_in_body,
        grid=grid,
        in_specs=[
            pl.BlockSpec((None, tm, D), row),
            _const_spec((1, D)),
            _const_spec(wglu.shape), _const_spec(wgc.shape), _const_spec(wga.shape),
            _const_spec(wqkv_t.shape), _const_spec(wf_t.shape),
        ],
        out_specs=[
            pl.BlockSpec((None, tm, CONV_DIM), row),
            pl.BlockSpec((None, tm, D), row),
            pl.BlockSpec((None, tm, D), row),
            pl.BlockSpec((None, 3 * FOX_DIM, tm), col),
            pl.BlockSpec((None, FOX_HEADS, tm), col),
        ],
        out_shape=[
            jax.ShapeDtypeStruct((B, T, CONV_DIM), F32),
            jax.ShapeDtypeStruct((B, T, D), BF16),
            jax.ShapeDtypeStruct((B, T, D), BF16),
            jax.ShapeDtypeStruct((B, 3 * FOX_DIM, T), BF16),
            jax.ShapeDtypeStruct((B, FOX_HEADS, T), F32),
        ],
        compiler_params=pltpu.CompilerParams(
            dimension_semantics=("parallel", "parallel"), vmem_limit_bytes=VMEM_LIMIT),
        name="proj_in",
    )(x, g, wglu, wgc, wga, wqkv_t, wf_t)


def _split3(c):
    hi = c.astype(BF16)
    r = c - hi.astype(F32)
    mid = r.astype(BF16)
    lo = (r - mid.astype(F32)).astype(BF16)
    return hi, mid, lo


def _fox_prep_body(f_t_ref, bf_ref, qaux_ref, kaux_ref, c_ref, *, chunk):
    T = f_t_ref.shape[-1]
    z = f_t_ref[...] + bf_ref[...]
    logf = jnp.minimum(z, 0.0) - jnp.log1p(jnp.exp(-jnp.abs(z)))
    r_i = lax.broadcasted_iota(jnp.int32, (chunk, chunk), 0)
    c_i = lax.broadcasted_iota(jnp.int32, (chunk, chunk), 1)
    tri = jnp.where(r_i <= c_i, 1.0, 0.0).astype(BF16)
    carry = jnp.zeros((FOX_HEADS, 1), F32)
    for j in range(T // chunk):
        blk = logf[:, j * chunk:(j + 1) * chunk]
        hi, mid, lo = _split3(blk)
        cs = (_dot(lo, tri) + _dot(mid, tri)) + _dot(hi, tri) + carry
        c_ref[:, j * chunk:(j + 1) * chunk] = cs
        carry = cs[:, chunk - 1:chunk]
    row = lax.broadcasted_iota(jnp.int32, (AUX_ROWS, T), 0)
    for h in range(FOX_HEADS):
        hi, mid, lo = _split3(c_ref[h:h + 1, :])
        pieces = [jnp.broadcast_to(p.astype(F32), (AUX_ROWS, T)) for p in (hi, mid, lo)]
        one = jnp.where(row < 3, 1.0, 0.0)
        val = jnp.where(row == 0, pieces[0], jnp.where(row == 1, pieces[1], pieces[2]))
        val_k = jnp.where(row == 3, pieces[0], jnp.where(row == 4, pieces[1], pieces[2]))
        qaux = jnp.where(row < 3, val, jnp.where(row < 6, 1.0, 0.0))
        kaux = jnp.where(row < 3, one, jnp.where(row < 6, -val_k, 0.0))
        qaux_ref[h] = qaux.astype(BF16)
        kaux_ref[h] = kaux.astype(BF16)


def _fox_prep(f_t, bf):
    B, H, T = f_t.shape
    aux = jax.ShapeDtypeStruct((B, H, AUX_ROWS, T), BF16)
    aux_spec = pl.BlockSpec((None, H, AUX_ROWS, T), lambda b: (b, 0, 0, 0))
    return pl.pallas_call(
        functools.partial(_fox_prep_body, chunk=512),
        grid=(B,),
        in_specs=[pl.BlockSpec((None, H, T), lambda b: (b, 0, 0)), _const_spec((H, 1))],
        out_specs=[aux_spec, aux_spec],
        out_shape=[aux, aux],
        scratch_shapes=[pltpu.VMEM((H, T), F32)],
        compiler_params=pltpu.CompilerParams(dimension_semantics=("parallel",)),
        name="fox_prep",
    )(f_t, bf)


MASK_BLOCK = 256


def _fox_attn_body(q_ref, k_ref, v_ref, qaux_ref, kaux_ref, o_ref, *, tq):
    T = q_ref.shape[-1]
    dh = FOX_HEAD_DIM
    mb = MASK_BLOCK
    nb = tq // mb
    kk = lax.broadcasted_iota(jnp.int32, (mb, mb), 0)
    qq = lax.broadcasted_iota(jnp.int32, (mb, mb), 1)
    causal = kk <= qq
    ones = jnp.where(lax.broadcasted_iota(jnp.int32, (AUX_ROWS, tq), 0) == 0, 1.0, 0.0).astype(BF16)

    def mask_diagonal(s):
        cols = []
        for bj in range(nb):
            blocks = []
            for bi in range(nb):
                blk = s[bi * mb:(bi + 1) * mb, bj * mb:(bj + 1) * mb]
                if bi == bj:
                    blk = jnp.where(causal, blk, -jnp.inf)
                elif bi > bj:
                    blk = jnp.full((mb, mb), -jnp.inf, F32)
                blocks.append(blk)
            cols.append(jnp.concatenate(blocks, axis=0))
        return jnp.concatenate(cols, axis=1)

    def attend(qa, k0, m, acc):
        ka = jnp.concatenate([k_ref[:, pl.ds(k0, tq)], kaux_ref[:, pl.ds(k0, tq)]], axis=0)
        va = jnp.concatenate([v_ref[:, pl.ds(k0, tq)], ones], axis=0)
        s = _dot_tn(ka, qa)
        if m is None:
            s = mask_diagonal(s)
        m_new = jnp.max(s, axis=0, keepdims=True)
        if m is not None:
            m_new = jnp.maximum(m, m_new)
        pv = _dot(va, jnp.exp(s - m_new).astype(BF16))
        if m is not None:
            pv = jnp.exp(m - m_new) * acc + pv
        return m_new, pv

    def q_body(qi, _):
        q0 = pl.multiple_of(qi * tq, tq)
        qa = jnp.concatenate([q_ref[:, pl.ds(q0, tq)], qaux_ref[:, pl.ds(q0, tq)]], axis=0)
        state = attend(qa, q0, None, None)
        m, acc = lax.fori_loop(
            0, qi, lambda c, st: attend(qa, pl.multiple_of(c * tq, tq), st[0], st[1]), state)
        o_ref[:, pl.ds(q0, tq)] = (acc[:dh] / acc[dh:dh + 1]).astype(o_ref.dtype)
        return 0

    lax.fori_loop(0, T // tq, q_body, 0)


def _fox_attn(qkv_t, qaux, kaux, *, tq):
    B, _, T = qkv_t.shape
    H, dh = FOX_HEADS, FOX_HEAD_DIM
    part = lambda off: pl.BlockSpec((None, dh, T), lambda b, h: (b, off * H + h, 0))
    aux = pl.BlockSpec((None, None, AUX_ROWS, T), lambda b, h: (b, h, 0, 0))
    return pl.pallas_call(
        functools.partial(_fox_attn_body, tq=tq),
        grid=(B, H),
        in_specs=[part(0), part(1), part(2), aux, aux],
        out_specs=pl.BlockSpec((None, dh, T), lambda b, h: (b, h, 0)),
        out_shape=jax.ShapeDtypeStruct((B, H * dh, T), BF16),
        compiler_params=pltpu.CompilerParams(
            dimension_semantics=("parallel", "parallel"), vmem_limit_bytes=VMEM_LIMIT),
        name="fox_attn",
    )(qkv_t, qkv_t, qkv_t, qaux, kaux)


HALO = 32
SUBLANES = 8


def _mix_out_body(x_ref, u_ref, uprev_ref, gc_ref, ga_ref, attn_t_ref, dww_ref, dwb_ref,
                  lng_ref, lnb_ref, pw_ref, abw_ref, wout_ref, o_ref, ubuf_ref, ushift_ref,
                  *, tc):
    first = pl.program_id(1) == 0
    ubuf_ref[:HALO, :] = jnp.where(first, 0.0, uprev_ref[...])
    ubuf_ref[HALO:, :] = u_ref[...]
    span = tc + HALO - SUBLANES
    for s in range(1, SUBLANES):
        ushift_ref[s - 1, :, :] = ubuf_ref[s:s + span, :]
    acc = jnp.zeros((tc, CONV_DIM), F32)
    for k in range(CONV_WIDTH):
        off = HALO - (CONV_WIDTH - 1) + k
        s, base = off % SUBLANES, off - off % SUBLANES
        rows = ubuf_ref[base:base + tc, :] if s == 0 else ushift_ref[s - 1, base:base + tc, :]
        acc = acc + rows * dww_ref[k:k + 1, :]
    acc = acc + dwb_ref[...]
    mu = jnp.mean(acc, axis=-1, keepdims=True)
    d = acc - mu
    var = jnp.mean(d * d, axis=-1, keepdims=True)
    y = d * lax.rsqrt(var + EPS) * lng_ref[...] + lnb_ref[...]
    y = (y * _sigmoid(y)).astype(BF16)
    conv_out = _dot(y, pw_ref[...])
    attn_out = _dot_tn(attn_t_ref[...], abw_ref[...])
    merged = (_sigmoid(gc_ref[...].astype(F32)) * conv_out
              + _sigmoid(ga_ref[...].astype(F32)) * attn_out)
    o_ref[...] = x_ref[...] + _dot(merged.astype(BF16), wout_ref[...])


def _mix_out(x, u, gc, ga, attn_t, dww, dwb, lng, lnb, pw, abw, wout, *, tc):
    B, T, D = x.shape
    row = lambda b, i: (b, i, 0)
    halo_blocks = tc // HALO
    return pl.pallas_call(
        functools.partial(_mix_out_body, tc=tc),
        grid=(B, T // tc),
        in_specs=[
            pl.BlockSpec((None, tc, D), row),
            pl.BlockSpec((None, tc, CONV_DIM), row),
            pl.BlockSpec((None, HALO, CONV_DIM),
                         lambda b, i: (b, jnp.maximum(i * halo_blocks - 1, 0), 0)),
            pl.BlockSpec((None, tc, D), row),
            pl.BlockSpec((None, tc, D), row),
            pl.BlockSpec((None, FOX_DIM, tc), lambda b, i: (b, 0, i)),
            _const_spec(dww.shape), _const_spec(dwb.shape), _const_spec(lng.shape),
            _const_spec(lnb.shape), _const_spec(pw.shape), _const_spec(abw.shape),
            _const_spec(wout.shape),
        ],
        out_specs=pl.BlockSpec((None, tc, D), row),
        out_shape=jax.ShapeDtypeStruct((B, T, D), F32),
        scratch_shapes=[pltpu.VMEM((HALO + tc, CONV_DIM), F32),
                        pltpu.VMEM((SUBLANES - 1, HALO + tc - SUBLANES, CONV_DIM), F32)],
        compiler_params=pltpu.CompilerParams(
            dimension_semantics=("parallel", "parallel"), vmem_limit_bytes=VMEM_LIMIT),
        name="mix_out",
    )(x, u, u, gc, ga, attn_t, dww, dwb, lng, lnb, pw, abw, wout)


def _mem_kv_body(m_ref, g_ref, wk_ref, wv_ref, k_ref, v_ref):
    h = _rms(m_ref[...], g_ref[...]).astype(BF16)
    k_ref[...] = _dot(h, wk_ref[...]).astype(BF16)
    v_ref[...] = _dot(h, wv_ref[...]).astype(BF16)


def _mem_kv(mem, g, wk, wv):
    B, M, D = mem.shape
    spec = pl.BlockSpec((None, M, XA_DIM), lambda b: (b, 0, 0))
    out = jax.ShapeDtypeStruct((B, M, XA_DIM), BF16)
    return pl.pallas_call(
        _mem_kv_body,
        grid=(B,),
        in_specs=[pl.BlockSpec((None, M, D), lambda b: (b, 0, 0)), _const_spec((1, D)),
                  _const_spec(wk.shape), _const_spec(wv.shape)],
        out_specs=[spec, spec],
        out_shape=[out, out],
        compiler_params=pltpu.CompilerParams(dimension_semantics=("parallel",)),
        name="mem_kv",
    )(mem, g, wk, wv)


ROUTE_LANES = 128
META_LANES = 128
XA_WIDTH = D_MODEL + META_LANES
LANE_GROUP = EXPERTS_PER_GROUP
LANE_RANK = EXPERTS_PER_GROUP + 1


def _first_max_lane(v, valid, lane):
    neg = jnp.where(valid, v, -jnp.inf)
    mx = jnp.max(neg, axis=-1, keepdims=True)
    idx = jnp.min(jnp.where(valid & (neg == mx), lane, ROUTE_LANES), axis=-1, keepdims=True)
    return mx, idx


def _route(logits, run):
    n = logits.shape[0]
    lane = lax.broadcasted_iota(jnp.int32, logits.shape, 1)
    is_g = lane < N_GROUPS
    gmax, g_star = _first_max_lane(logits, is_g, lane)
    p_gsel = 1.0 / jnp.sum(jnp.where(is_g, jnp.exp(logits - gmax), 0.0), axis=-1, keepdims=True)
    e_lane0 = N_GROUPS + g_star * EXPERTS_PER_GROUP
    in_grp = (lane >= e_lane0) & (lane < e_lane0 + EXPERTS_PER_GROUP)
    emax, _ = _first_max_lane(logits, in_grp, lane)
    ex = jnp.where(in_grp, jnp.exp(logits - emax), 0.0)
    p_exp = ex / jnp.sum(ex, axis=-1, keepdims=True)
    v1, i1 = _first_max_lane(p_exp, in_grp, lane)
    v2, i2 = _first_max_lane(p_exp, in_grp & (lane != i1), lane)
    tot = v1 + v2
    w = jnp.where(lane == i1 - e_lane0, p_gsel * (v1 / tot),
                  jnp.where(lane == i2 - e_lane0, p_gsel * (v2 / tot), 0.0))
    onehot = jnp.where(lane == g_star, 1.0, 0.0).astype(BF16)
    r_i = lax.broadcasted_iota(jnp.int32, (n, n), 0)
    c_i = lax.broadcasted_iota(jnp.int32, (n, n), 1)
    cum = _dot(jnp.where(r_i >= c_i, 1.0, 0.0).astype(BF16), onehot) + run
    rank = jnp.sum(jnp.where(lane == g_star, cum, 0.0), axis=-1, keepdims=True) - 1.0
    meta = (w + jnp.where(lane == LANE_GROUP, g_star.astype(F32), 0.0)
            + jnp.where(lane == LANE_RANK, rank, 0.0))
    return meta, cum[n - 1:n, :]


def _xattn_body(x_ref, g_ref, wq_ref, k_ref, v_ref, wo_ref, gm_ref, wr_ref, br_ref,
                xa_ref, cnt_ref, run_ref):
    @pl.when((pl.program_id(0) == 0) & (pl.program_id(1) == 0))
    def _():
        run_ref[...] = jnp.zeros_like(run_ref)

    x = x_ref[...]
    q = _dot(_rms(x, g_ref[...]).astype(BF16), wq_ref[...]).astype(BF16)
    scale = XA_HEAD_DIM ** -0.5
    heads = []
    for h in range(XA_HEADS):
        sl = slice(h * XA_HEAD_DIM, (h + 1) * XA_HEAD_DIM)
        s = _dot_nt(q[:, sl], k_ref[:, sl]) * scale
        p = jnp.exp(s - jnp.max(s, axis=-1, keepdims=True))
        p = p / jnp.sum(p, axis=-1, keepdims=True)
        heads.append(_dot(p.astype(BF16), v_ref[:, sl]).astype(BF16))
    o = jnp.concatenate(heads, axis=-1)
    x2 = x + _dot(o, wo_ref[...])
    hm = _rms(x2, gm_ref[...]).astype(BF16)
    meta, run = _route(_dot(hm, wr_ref[...]) + br_ref[...], run_ref[...])
    run_ref[...] = run
    cnt_ref[...] = run
    xa_ref[:, :D_MODEL] = x2
    xa_ref[:, D_MODEL:] = meta


def _xattn_route(x, g, wq, kx, vx, wo, gm, wr, br, *, tx):
    B, T, D = x.shape
    row = lambda b, i: (b, i, 0)
    kv = pl.BlockSpec((None, MEM_LEN, XA_DIM), lambda b, i: (b, 0, 0))
    return pl.pallas_call(
        _xattn_body,
        grid=(B, T // tx),
        in_specs=[pl.BlockSpec((None, tx, D), row), _const_spec((1, D)),
                  _const_spec(wq.shape), kv, kv, _const_spec(wo.shape), _const_spec((1, D)),
                  _const_spec(wr.shape), _const_spec(br.shape)],
        out_specs=[pl.BlockSpec((None, tx, XA_WIDTH), row), _const_spec((1, ROUTE_LANES))],
        out_shape=[jax.ShapeDtypeStruct((B, T, XA_WIDTH), F32),
                   jax.ShapeDtypeStruct((1, ROUTE_LANES), F32)],
        scratch_shapes=[pltpu.VMEM((1, ROUTE_LANES), F32)],
        compiler_params=pltpu.CompilerParams(
            dimension_semantics=("arbitrary", "arbitrary"), vmem_limit_bytes=VMEM_LIMIT),
        name="xattn_route",
    )(x, g, wq, kx, vx, wo, gm, wr, br)


VISIT_FIRST, VISIT_LAST, VISIT_VALID = 1, 2, 4


def _moe_plan(group, rank, counts, *, ts):
    n = group.shape[0]
    nt = n // ts
    nv = nt + N_GROUPS - 1
    ends = jnp.cumsum(counts)
    pos = (ends - counts)[group] + rank
    inv = jnp.zeros((n,), jnp.int32).at[pos].set(jnp.arange(n, dtype=jnp.int32))
    tile0 = jnp.arange(nt, dtype=jnp.int32) * ts
    g_lo = jnp.sum(ends[None, :] <= tile0[:, None], axis=1).astype(jnp.int32)
    g_hi = jnp.sum(ends[None, :] <= tile0[:, None] + (ts - 1), axis=1).astype(jnp.int32)
    n_vis = g_hi - g_lo + 1
    start = jnp.cumsum(n_vis) - n_vis
    v = jnp.arange(nv, dtype=jnp.int32)
    vt = jnp.minimum(jnp.sum(start[None, :] <= v[:, None], axis=1) - 1, nt - 1).astype(jnp.int32)
    k = v - start[vt]
    valid = v < start[-1] + n_vis[-1]
    vg = jnp.where(valid, g_lo[vt] + k, g_hi[nt - 1]).astype(jnp.int32)
    vf = (jnp.where(valid & (k == 0), VISIT_FIRST, 0)
          + jnp.where(valid & (k == n_vis[vt] - 1), VISIT_LAST, 0)
          + jnp.where(valid, VISIT_VALID, 0)).astype(jnp.int32)
    return inv.reshape(nt, 1, ts), vt, vg, vf


def _moe_body(vt_ref, vg_ref, vf_ref, inv_cur_ref, inv_nxt_ref, xa_hbm, gm_ref, wg_ref, wu_ref,
              wd_ref, gf_ref, out_hbm, xbuf, obuf, hm_ref, yacc_ref, gsem, ssem, *, ts, nt, sub):
    v = pl.program_id(0)
    t, grp, flags = vt_ref[v], vg_ref[v], vf_ref[v]
    slot = lax.rem(t, 2)

    def gather_start(idx_ref, dst_slot):
        for r in range(ts):
            pltpu.make_async_copy(xa_hbm.at[pl.ds(idx_ref[0, r], 1)],
                                  xbuf.at[dst_slot, pl.ds(r, 1)], gsem.at[dst_slot]).start()

    def gather_wait(dst_slot):
        pltpu.make_async_copy(xa_hbm.at[pl.ds(0, ts)], xbuf.at[dst_slot], gsem.at[dst_slot]).wait()

    def scatter_start(src_slot):
        for r in range(ts):
            pltpu.make_async_copy(obuf.at[src_slot, pl.ds(r, 1)],
                                  out_hbm.at[pl.ds(inv_cur_ref[0, r], 1)], ssem.at[src_slot]).start()

    def scatter_wait(src_slot):
        pltpu.make_async_copy(obuf.at[src_slot], out_hbm.at[pl.ds(0, ts)], ssem.at[src_slot]).wait()

    @pl.when(v == 0)
    def _():
        gather_start(inv_cur_ref, 0)

    @pl.when((flags & VISIT_FIRST) != 0)
    def _():
        gather_wait(slot)

        @pl.when(t + 1 < nt)
        def _():
            gather_start(inv_nxt_ref, 1 - slot)

        hm_ref[...] = _rms(xbuf[slot, :, :D_MODEL], gm_ref[...]).astype(BF16)
        yacc_ref[...] = jnp.zeros_like(yacc_ref)

    @pl.when((flags & VISIT_VALID) != 0)
    def _():
        lane = lax.broadcasted_iota(jnp.int32, (sub, META_LANES), 1)
        for r in range(ts // sub):
            rows = slice(r * sub, (r + 1) * sub)
            meta = xbuf[slot, rows, D_MODEL:]
            row_grp = jnp.sum(jnp.where(lane == LANE_GROUP, meta, 0.0), axis=-1, keepdims=True)
            member = row_grp == grp.astype(F32)
            hm = hm_ref[rows, :]
            y = jnp.zeros((sub, D_MODEL), F32)
            for e in range(EXPERTS_PER_GROUP):
                w = jnp.sum(jnp.where(lane == e, meta, 0.0), axis=-1, keepdims=True)
                w = jnp.where(member, w, 0.0)
                a = _dot(hm, wg_ref[e])
                b = _dot(hm, wu_ref[e])
                u = (a * _sigmoid(a)) * b * w
                y = y + _dot(u.astype(BF16), wd_ref[e])
            yacc_ref[rows, :] += y

    @pl.when((flags & VISIT_LAST) != 0)
    def _():
        @pl.when(t >= 2)
        def _():
            scatter_wait(slot)

        obuf[slot] = _rms(xbuf[slot, :, :D_MODEL] + yacc_ref[...], gf_ref[...])
        scatter_start(slot)

    @pl.when(v == pl.num_programs(0) - 1)
    def _():
        scatter_wait(0)
        scatter_wait(1)


def _moe_sorted(xa, inv, vt, vg, vf, gm, wg, wu, wd, gf, *, ts, sub):
    N = xa.shape[0]
    nt = N // ts
    assert nt >= 2
    E, D, FF = EXPERTS_PER_GROUP, D_MODEL, EXPERT_FF
    grp = lambda v, vt, vg, vf: (vg[v], 0, 0)
    const = lambda v, vt, vg, vf: (0, 0)
    smem_row = lambda fn: pl.BlockSpec((None, 1, ts), fn, memory_space=pltpu.SMEM)
    grid_spec = pltpu.PrefetchScalarGridSpec(
        num_scalar_prefetch=3,
        grid=(vt.shape[0],),
        in_specs=[
            smem_row(lambda v, vt, vg, vf: (vt[v], 0, 0)),
            smem_row(lambda v, vt, vg, vf: (jnp.minimum(vt[v] + 1, nt - 1), 0, 0)),
            pl.BlockSpec(memory_space=pl.ANY),
            pl.BlockSpec((1, D), const),
            pl.BlockSpec((E, D, FF), grp), pl.BlockSpec((E, D, FF), grp),
            pl.BlockSpec((E, FF, D), grp),
            pl.BlockSpec((1, D), const),
        ],
        out_specs=pl.BlockSpec(memory_space=pl.ANY),
        scratch_shapes=[
            pltpu.VMEM((2, ts, XA_WIDTH), F32), pltpu.VMEM((2, ts, D), F32),
            pltpu.VMEM((ts, D), BF16), pltpu.VMEM((ts, D), F32),
            pltpu.SemaphoreType.DMA((2,)), pltpu.SemaphoreType.DMA((2,)),
        ],
    )
    return pl.pallas_call(
        functools.partial(_moe_body, ts=ts, nt=nt, sub=sub),
        grid_spec=grid_spec,
        out_shape=jax.ShapeDtypeStruct((N, D), F32),
        compiler_params=pltpu.CompilerParams(
            dimension_semantics=("arbitrary",), vmem_limit_bytes=VMEM_LIMIT),
        name="moe_sorted",
    )(vt, vg, vf, inv, inv, xa, gm, wg, wu, wd, gf)


def kernel(x, mem, norm_mix_g, w_in, fox_bf, conv_dw_w, conv_dw_b, conv_ln_g, conv_ln_b, conv_pw_w, attn_branch_w, w_out, norm_xa_g, norm_mem_g, xa_wq, xa_wk, xa_wv, xa_wo, norm_moe_g, router_group_w, router_group_b, router_expert_w, router_expert_b, expert_w_gate, expert_w_up, expert_w_down, norm_final_g):
    B, T, D = x.shape
    depth = w_in.shape[0]
    assert depth == 1, "the final norm is fused into the last layer's MoE kernel"
    row = lambda v: v.reshape(1, -1).astype(F32)
    c_glu = 2 * CONV_DIM
    c_q, c_f = c_glu, c_glu + 3 * FOX_DIM
    c_gc = c_f + FOX_HEADS
    c_ga = c_gc + D
    for l in range(depth):
        w = w_in[l]
        u, gc, ga, qkv_t, f_t = _proj_in(
            x, row(norm_mix_g[l]),
            w[:, :c_glu].astype(BF16), w[:, c_gc:c_ga].astype(BF16), w[:, c_ga:].astype(BF16),
            w[:, c_q:c_f].T.astype(BF16), w[:, c_f:c_gc].T.astype(BF16), tm=512)
        qaux, kaux = _fox_prep(f_t, fox_bf[l].reshape(FOX_HEADS, 1).astype(F32))
        attn_t = _fox_attn(qkv_t, qaux, kaux, tq=1024)
        x = _mix_out(x, u, gc, ga, attn_t, conv_dw_w[l].astype(F32), row(conv_dw_b[l]),
                     row(conv_ln_g[l]), row(conv_ln_b[l]), conv_pw_w[l].astype(BF16),
                     attn_branch_w[l].astype(BF16), w_out[l].astype(BF16), tc=256)
        kx, vx = _mem_kv(mem, row(norm_mem_g[l]), xa_wk[l].astype(BF16), xa_wv[l].astype(BF16))
        wr = jnp.zeros((D, ROUTE_LANES), F32)
        wr = wr.at[:, :N_GROUPS].set(router_group_w[l]).at[:, N_GROUPS:N_GROUPS + N_EXPERTS].set(
            router_expert_w[l])
        br = jnp.zeros((1, ROUTE_LANES), F32)
        br = br.at[0, :N_GROUPS].set(router_group_b[l]).at[0, N_GROUPS:N_GROUPS + N_EXPERTS].set(
            router_expert_b[l])
        xa, cnt = _xattn_route(x, row(norm_xa_g[l]), xa_wq[l].astype(BF16), kx, vx,
                               xa_wo[l].astype(BF16), row(norm_moe_g[l]), wr.astype(BF16), br,
                               tx=512)
        xa = xa.reshape(B * T, XA_WIDTH)
        inv, vt, vg, vf = _moe_plan(xa[:, D + LANE_GROUP].astype(jnp.int32),
                                    xa[:, D + LANE_RANK].astype(jnp.int32),
                                    cnt[0, :N_GROUPS].astype(jnp.int32), ts=512)
        x = _moe_sorted(xa, inv, vt, vg, vf, row(norm_moe_g[l]), expert_w_gate[l].astype(BF16),
                        expert_w_up[l].astype(BF16), expert_w_down[l].astype(BF16),
                        row(norm_final_g), ts=512, sub=256).reshape(B, T, D)
    return x
```

```python
import functools

import jax
import jax.numpy as jnp
from jax import lax
from jax.experimental import pallas as pl
from jax.experimental.pallas import tpu as pltpu

D_MODEL = 1024
MEM_LEN = 256
EPS = 1e-6
CONV_DIM = 512
CONV_WIDTH = 31
FOX_HEADS = 8
FOX_HEAD_DIM = 64
FOX_DIM = FOX_HEADS * FOX_HEAD_DIM
XA_HEADS = 4
XA_HEAD_DIM = 128
XA_DIM = XA_HEADS * XA_HEAD_DIM
N_GROUPS = 4
EXPERTS_PER_GROUP = 4
N_EXPERTS = N_GROUPS * EXPERTS_PER_GROUP
EXPERT_FF = 256

AUX_ROWS = 16
VMEM_LIMIT = 56 * 1024 * 1024

F32 = jnp.float32
BF16 = jnp.bfloat16


def _const_spec(shape):
    n = len(shape)
    return pl.BlockSpec(shape, lambda *_: (0,) * n)


def _rms(x, g):
    return x * lax.rsqrt(jnp.mean(x * x, axis=-1, keepdims=True) + EPS) * g


def _sigmoid(x):
    return 1.0 / (1.0 + jnp.exp(-x))


def _dot(a, b):
    return jnp.dot(a, b, preferred_element_type=F32)


def _dot_nt(a, b):
    return lax.dot_general(a, b, (((1,), (1,)), ((), ())), preferred_element_type=F32)


def _dot_tn(a, b):
    return lax.dot_general(a, b, (((0,), (0,)), ((), ())), preferred_element_type=F32)


def _proj_in_body(x_ref, g_ref, wglu_ref, wgc_ref, wga_ref, wqkv_t_ref, wf_t_ref,
                  u_ref, gc_ref, ga_ref, qkv_t_ref, f_t_ref):
    h = _rms(x_ref[...], g_ref[...]).astype(BF16)
    glu = _dot(h, wglu_ref[...])
    u_ref[...] = glu[:, :CONV_DIM] * _sigmoid(glu[:, CONV_DIM:])
    gc_ref[...] = _dot(h, wgc_ref[...]).astype(BF16)
    ga_ref[...] = _dot(h, wga_ref[...]).astype(BF16)
    qkv_t = _dot_nt(wqkv_t_ref[...], h)
    scale = FOX_HEAD_DIM ** -0.5
    qkv_t_ref[:FOX_DIM, :] = (qkv_t[:FOX_DIM] * scale).astype(BF16)
    qkv_t_ref[FOX_DIM:, :] = qkv_t[FOX_DIM:].astype(BF16)
    f_t_ref[...] = _dot_nt(wf_t_ref[...], h)


def _proj_in(x, g, wglu, wgc, wga, wqkv_t, wf_t, *, tm):
    B, T, D = x.shape
    grid = (B, T // tm)
    row = lambda b, i: (b, i, 0)
    col = lambda b, i: (b, 0, i)
    return pl.pallas_call(
        _proj_in_body,
        grid=grid,
        in_specs=[
            pl.BlockSpec((None, tm, D), row),
            _const_spec((1, D)),
            _const_spec(wglu.shape), _const_spec(wgc.shape), _const_spec(wga.shape),
            _const_spec(wqkv_t.shape), _const_spec(wf_t.shape),
        ],
        out_specs=[
            pl.BlockSpec((None, tm, CONV_DIM), row),
            pl.BlockSpec((None, tm, D), row),
            pl.BlockSpec((None, tm, D), row),
            pl.BlockSpec((None, 3 * FOX_DIM, tm), col),
            pl.BlockSpec((None, FOX_HEADS, tm), col),
        ],
        out_shape=[
            jax.ShapeDtypeStruct((B, T, CONV_DIM), F32),
            jax.ShapeDtypeStruct((B, T, D), BF16),
            jax.ShapeDtypeStruct((B, T, D), BF16),
            jax.ShapeDtypeStruct((B, 3 * FOX_DIM, T), BF16),
            jax.ShapeDtypeStruct((B, FOX_HEADS, T), F32),
        ],
        compiler_params=pltpu.CompilerParams(
            dimension_semantics=("parallel", "parallel"), vmem_limit_bytes=VMEM_LIMIT),
        name="proj_in",
    )(x, g, wglu, wgc, wga, wqkv_t, wf_t)


def _split3(c):
    hi = c.astype(BF16)
    r = c - hi.astype(F32)
    mid = r.astype(BF16)
    lo = (r - mid.astype(F32)).astype(BF16)
    return hi, mid, lo


def _fox_prep_body(f_t_ref, bf_ref, qaux_ref, kaux_ref, c_ref, *, chunk):
    T = f_t_ref.shape[-1]
    z = f_t_ref[...] + bf_ref[...]
    logf = jnp.minimum(z, 0.0) - jnp.log1p(jnp.exp(-jnp.abs(z)))
    r_i = lax.broadcasted_iota(jnp.int32, (chunk, chunk), 0)
    c_i = lax.broadcasted_iota(jnp.int32, (chunk, chunk), 1)
    tri = jnp.where(r_i <= c_i, 1.0, 0.0).astype(BF16)
    carry = jnp.zeros((FOX_HEADS, 1), F32)
    for j in range(T // chunk):
        blk = logf[:, j * chunk:(j + 1) * chunk]
        hi, mid, lo = _split3(blk)
        cs = (_dot(lo, tri) + _dot(mid, tri)) + _dot(hi, tri) + carry
        c_ref[:, j * chunk:(j + 1) * chunk] = cs
        carry = cs[:, chunk - 1:chunk]
    row = lax.broadcasted_iota(jnp.int32, (AUX_ROWS, T), 0)
    for h in range(FOX_HEADS):
        hi, mid, lo = _split3(c_ref[h:h + 1, :])
        pieces = [jnp.broadcast_to(p.astype(F32), (AUX_ROWS, T)) for p in (hi, mid, lo)]
        one = jnp.where(row < 3, 1.0, 0.0)
        val = jnp.where(row == 0, pieces[0], jnp.where(row == 1, pieces[1], pieces[2]))
        val_k = jnp.where(row == 3, pieces[0], jnp.where(row == 4, pieces[1], pieces[2]))
        qaux = jnp.where(row < 3, val, jnp.where(row < 6, 1.0, 0.0))
        kaux = jnp.where(row < 3, one, jnp.where(row < 6, -val_k, 0.0))
        qaux_ref[h] = qaux.astype(BF16)
        kaux_ref[h] = kaux.astype(BF16)


def _fox_prep(f_t, bf):
    B, H, T = f_t.shape
    aux = jax.ShapeDtypeStruct((B, H, AUX_ROWS, T), BF16)
    aux_spec = pl.BlockSpec((None, H, AUX_ROWS, T), lambda b: (b, 0, 0, 0))
    return pl.pallas_call(
        functools.partial(_fox_prep_body, chunk=512),
        grid=(B,),
        in_specs=[pl.BlockSpec((None, H, T), lambda b: (b, 0, 0)), _const_spec((H, 1))],
        out_specs=[aux_spec, aux_spec],
        out_shape=[aux, aux],
        scratch_shapes=[pltpu.VMEM((H, T), F32)],
        compiler_params=pltpu.CompilerParams(dimension_semantics=("parallel",)),
        name="fox_prep",
    )(f_t, bf)


MASK_BLOCK = 256


def _fox_attn_body(q_ref, k_ref, v_ref, qaux_ref, kaux_ref, o_ref, *, tq):
    T = q_ref.shape[-1]
    dh = FOX_HEAD_DIM
    mb = MASK_BLOCK
    nb = tq // mb
    kk = lax.broadcasted_iota(jnp.int32, (mb, mb), 0)
    qq = lax.broadcasted_iota(jnp.int32, (mb, mb), 1)
    causal = kk <= qq
    ones = jnp.where(lax.broadcasted_iota(jnp.int32, (AUX_ROWS, tq), 0) == 0, 1.0, 0.0).astype(BF16)

    def mask_diagonal(s):
        cols = []
        for bj in range(nb):
            blocks = []
            for bi in range(nb):
                blk = s[bi * mb:(bi + 1) * mb, bj * mb:(bj + 1) * mb]
                if bi == bj:
                    blk = jnp.where(causal, blk, -jnp.inf)
                elif bi > bj:
                    blk = jnp.full((mb, mb), -jnp.inf, F32)
                blocks.append(blk)
            cols.append(jnp.concatenate(blocks, axis=0))
        return jnp.concatenate(cols, axis=1)

    def attend(qa, k0, m, acc):
        ka = jnp.concatenate([k_ref[:, pl.ds(k0, tq)], kaux_ref[:, pl.ds(k0, tq)]], axis=0)
        va = jnp.concatenate([v_ref[:, pl.ds(k0, tq)], ones], axis=0)
        s = _dot_tn(ka, qa)
        if m is None:
            s = mask_diagonal(s)
        m_new = jnp.max(s, axis=0, keepdims=True)
        if m is not None:
            m_new = jnp.maximum(m, m_new)
        pv = _dot(va, jnp.exp(s - m_new).astype(BF16))
        if m is not None:
            pv = jnp.exp(m - m_new) * acc + pv
        return m_new, pv

    def q_body(qi, _):
        q0 = pl.multiple_of(qi * tq, tq)
        qa = jnp.concatenate([q_ref[:, pl.ds(q0, tq)], qaux_ref[:, pl.ds(q0, tq)]], axis=0)
        state = attend(qa, q0, None, None)
        m, acc = lax.fori_loop(
            0, qi, lambda c, st: attend(qa, pl.multiple_of(c * tq, tq), st[0], st[1]), state)
        o_ref[:, pl.ds(q0, tq)] = (acc[:dh] / acc[dh:dh + 1]).astype(o_ref.dtype)
        return 0

    lax.fori_loop(0, T // tq, q_body, 0)


def _fox_attn(qkv_t, qaux, kaux, *, tq):
    B, _, T = qkv_t.shape
    H, dh = FOX_HEADS, FOX_HEAD_DIM
    part = lambda off: pl.BlockSpec((None, dh, T), lambda b, h: (b, off * H + h, 0))
    aux = pl.BlockSpec((None, None, AUX_ROWS, T), lambda b, h: (b, h, 0, 0))
    return pl.pallas_call(
        functools.partial(_fox_attn_body, tq=tq),
        grid=(B, H),
        in_specs=[part(0), part(1), part(2), aux, aux],
        out_specs=pl.BlockSpec((None, dh, T), lambda b, h: (b, h, 0)),
        out_shape=jax.ShapeDtypeStruct((B, H * dh, T), BF16),
        compiler_params=pltpu.CompilerParams(
            dimension_semantics=("parallel", "parallel"), vmem_limit_bytes=VMEM_LIMIT),
        name="fox_attn",
    )(qkv_t, qkv_t, qkv_t, qaux, kaux)


HALO = 32
SUBLANES = 8


def _mix_out_body(x_ref, u_ref, uprev_ref, gc_ref, ga_ref, attn_t_ref, dww_ref, dwb_ref,
                  lng_ref, lnb_ref, pw_ref, abw_ref, wout_ref, o_ref, ubuf_ref, ushift_ref,
                  *, tc):
    first = pl.program_id(1) == 0
    ubuf_ref[:HALO, :] = jnp.where(first, 0.0, uprev_ref[...])
    ubuf_ref[HALO:, :] = u_ref[...]
    span = tc + HALO - SUBLANES
    for s in range(1, SUBLANES):
        ushift_ref[s - 1, :, :] = ubuf_ref[s:s + span, :]
    acc = jnp.zeros((tc, CONV_DIM), F32)
    for k in range(CONV_WIDTH):
        off = HALO - (CONV_WIDTH - 1) + k
        s, base = off % SUBLANES, off - off % SUBLANES
        rows = ubuf_ref[base:base + tc, :] if s == 0 else ushift_ref[s - 1, base:base + tc, :]
        acc = acc + rows * dww_ref[k:k + 1, :]
    acc = acc + dwb_ref[...]
    mu = jnp.mean(acc, axis=-1, keepdims=True)
    d = acc - mu
    var = jnp.mean(d * d, axis=-1, keepdims=True)
    y = d * lax.rsqrt(var + EPS) * lng_ref[...] + lnb_ref[...]
    y = (y * _sigmoid(y)).astype(BF16)
    conv_out = _dot(y, pw_ref[...])
    attn_out = _dot_tn(attn_t_ref[...], abw_ref[...])
    merged = (_sigmoid(gc_ref[...].astype(F32)) * conv_out
              + _sigmoid(ga_ref[...].astype(F32)) * attn_out)
    o_ref[...] = x_ref[...] + _dot(merged.astype(BF16), wout_ref[...])


def _mix_out(x, u, gc, ga, attn_t, dww, dwb, lng, lnb, pw, abw, wout, *, tc):
    B, T, D = x.shape
    row = lambda b, i: (b, i, 0)
    halo_blocks = tc // HALO
    return pl.pallas_call(
        functools.partial(_mix_out_body, tc=tc),
        grid=(B, T // tc),
        in_specs=[
            pl.BlockSpec((None, tc, D), row),
            pl.BlockSpec((None, tc, CONV_DIM), row),
            pl.BlockSpec((None, HALO, CONV_DIM),
                         lambda b, i: (b, jnp.maximum(i * halo_blocks - 1, 0), 0)),
            pl.BlockSpec((None, tc, D), row),
            pl.BlockSpec((None, tc, D), row),
            pl.BlockSpec((None, FOX_DIM, tc), lambda b, i: (b, 0, i)),
            _const_spec(dww.shape), _const_spec(dwb.shape), _const_spec(lng.shape),
            _const_spec(lnb.shape), _const_spec(pw.shape), _const_spec(abw.shape),
            _const_spec(wout.shape),
        ],
        out_specs=pl.BlockSpec((None, tc, D), row),
        out_shape=jax.ShapeDtypeStruct((B, T, D), F32),
        scratch_shapes=[pltpu.VMEM((HALO + tc, CONV_DIM), F32),
                        pltpu.VMEM((SUBLANES - 1, HALO + tc - SUBLANES, CONV_DIM), F32)],
        compiler_params=pltpu.CompilerParams(
            dimension_semantics=("parallel", "parallel"), vmem_limit_bytes=VMEM_LIMIT),
        name="mix_out",
    )(x, u, u, gc, ga, attn_t, dww, dwb, lng, lnb, pw, abw, wout)


def _mem_kv_body(m_ref, g_ref, wk_ref, wv_ref, k_ref, v_ref):
    h = _rms(m_ref[...], g_ref[...]).astype(BF16)
    k_ref[...] = _dot(h, wk_ref[...]).astype(BF16)
    v_ref[...] = _dot(h, wv_ref[...]).astype(BF16)


def _mem_kv(mem, g, wk, wv):
    B, M, D = mem.shape
    spec = pl.BlockSpec((None, M, XA_DIM), lambda b: (b, 0, 0))
    out = jax.ShapeDtypeStruct((B, M, XA_DIM), BF16)
    return pl.pallas_call(
        _mem_kv_body,
        grid=(B,),
        in_specs=[pl.BlockSpec((None, M, D), lambda b: (b, 0, 0)), _const_spec((1, D)),
                  _const_spec(wk.shape), _const_spec(wv.shape)],
        out_specs=[spec, spec],
        out_shape=[out, out],
        compiler_params=pltpu.CompilerParams(dimension_semantics=("parallel",)),
        name="mem_kv",
    )(mem, g, wk, wv)


ROUTE_LANES = 128
META_LANES = 128
XA_WIDTH = D_MODEL + META_LANES
LANE_GROUP = EXPERTS_PER_GROUP
LANE_RANK = EXPERTS_PER_GROUP + 1


def _first_max_lane(v, valid, lane):
    neg = jnp.where(valid, v, -jnp.inf)
    mx = jnp.max(neg, axis=-1, keepdims=True)
    idx = jnp.min(jnp.where(valid & (neg == mx), lane, ROUTE_LANES), axis=-1, keepdims=True)
    return mx, idx


def _route(logits, run):
    n = logits.shape[0]
    lane = lax.broadcasted_iota(jnp.int32, logits.shape, 1)
    is_g = lane < N_GROUPS
    gmax, g_star = _first_max_lane(logits, is_g, lane)
    p_gsel = 1.0 / jnp.sum(jnp.where(is_g, jnp.exp(logits - gmax), 0.0), axis=-1, keepdims=True)
    e_lane0 = N_GROUPS + g_star * EXPERTS_PER_GROUP
    in_grp = (lane >= e_lane0) & (lane < e_lane0 + EXPERTS_PER_GROUP)
    emax, _ = _first_max_lane(logits, in_grp, lane)
    ex = jnp.where(in_grp, jnp.exp(logits - emax), 0.0)
    p_exp = ex / jnp.sum(ex, axis=-1, keepdims=True)
    v1, i1 = _first_max_lane(p_exp, in_grp, lane)
    v2, i2 = _first_max_lane(p_exp, in_grp & (lane != i1), lane)
    tot = v1 + v2
    w = jnp.where(lane == i1 - e_lane0, p_gsel * (v1 / tot),
                  jnp.where(lane == i2 - e_lane0, p_gsel * (v2 / tot), 0.0))
    onehot = jnp.where(lane == g_star, 1.0, 0.0).astype(BF16)
    r_i = lax.broadcasted_iota(jnp.int32, (n, n), 0)
    c_i = lax.broadcasted_iota(jnp.int32, (n, n), 1)
    cum = _dot(jnp.where(r_i >= c_i, 1.0, 0.0).astype(BF16), onehot) + run
    rank = jnp.sum(jnp.where(lane == g_star, cum, 0.0), axis=-1, keepdims=True) - 1.0
    meta = (w + jnp.where(lane == LANE_GROUP, g_star.astype(F32), 0.0)
            + jnp.where(lane == LANE_RANK, rank, 0.0))
    return meta, cum[n - 1:n, :]


def _xattn_body(x_ref, g_ref, wq_ref, k_ref, v_ref, wo_ref, gm_ref, wr_ref, br_ref,
                xa_ref, cnt_ref, run_ref):
    @pl.when((pl.program_id(0) == 0) & (pl.program_id(1) == 0))
    def _():
        run_ref[...] = jnp.zeros_like(run_ref)

    x = x_ref[...]
    q = _dot(_rms(x, g_ref[...]).astype(BF16), wq_ref[...]).astype(BF16)
    scale = XA_HEAD_DIM ** -0.5
    heads = []
    for h in range(XA_HEADS):
        sl = slice(h * XA_HEAD_DIM, (h + 1) * XA_HEAD_DIM)
        s = _dot_nt(q[:, sl], k_ref[:, sl]) * scale
        p = jnp.exp(s - jnp.max(s, axis=-1, keepdims=True))
        p = p / jnp.sum(p, axis=-1, keepdims=True)
        heads.append(_dot(p.astype(BF16), v_ref[:, sl]).astype(BF16))
    o = jnp.concatenate(heads, axis=-1)
    x2 = x + _dot(o, wo_ref[...])
    hm = _rms(x2, gm_ref[...]).astype(BF16)
    meta, run = _route(_dot(hm, wr_ref[...]) + br_ref[...], run_ref[...])
    run_ref[...] = run
    cnt_ref[...] = run
    xa_ref[:, :D_MODEL] = x2
    xa_ref[:, D_MODEL:] = meta


def _xattn_route(x, g, wq, kx, vx, wo, gm, wr, br, *, tx):
    B, T, D = x.shape
    row = lambda b, i: (b, i, 0)
    kv = pl.BlockSpec((None, MEM_LEN, XA_DIM), lambda b, i: (b, 0, 0))
    return pl.pallas_call(
        _xattn_body,
        grid=(B, T // tx),
        in_specs=[pl.BlockSpec((None, tx, D), row), _const_spec((1, D)),
                  _const_spec(wq.shape), kv, kv, _const_spec(wo.shape), _const_spec((1, D)),
                  _const_spec(wr.shape), _const_spec(br.shape)],
        out_specs=[pl.BlockSpec((None, tx, XA_WIDTH), row), _const_spec((1, ROUTE_LANES))],
        out_shape=[jax.ShapeDtypeStruct((B, T, XA_WIDTH), F32),
                   jax.ShapeDtypeStruct((1, ROUTE_LANES), F32)],
        scratch_shapes=[pltpu.VMEM((1, ROUTE_LANES), F32)],
        compiler_params=pltpu.CompilerParams(
            dimension_semantics=("arbitrary", "arbitrary"), vmem_limit_bytes=VMEM_LIMIT),
        name="xattn_route",
    )(x, g, wq, kx, vx, wo, gm, wr, br)


VISIT_FIRST, VISIT_LAST, VISIT_VALID = 1, 2, 4


def _moe_plan(group, rank, counts, *, ts):
    n = group.shape[0]
    nt = n // ts
    nv = nt + N_GROUPS - 1
    ends = jnp.cumsum(counts)
    pos = (ends - counts)[group] + rank
    inv = jnp.zeros((n,), jnp.int32).at[pos].set(jnp.arange(n, dtype=jnp.int32))
    tile0 = jnp.arange(nt, dtype=jnp.int32) * ts
    g_lo = jnp.sum(ends[None, :] <= tile0[:, None], axis=1).astype(jnp.int32)
    g_hi = jnp.sum(ends[None, :] <= tile0[:, None] + (ts - 1), axis=1).astype(jnp.int32)
    n_vis = g_hi - g_lo + 1
    start = jnp.cumsum(n_vis) - n_vis
    v = jnp.arange(nv, dtype=jnp.int32)
    vt = jnp.minimum(jnp.sum(start[None, :] <= v[:, None], axis=1) - 1, nt - 1).astype(jnp.int32)
    k = v - start[vt]
    valid = v < start[-1] + n_vis[-1]
    vg = jnp.where(valid, g_lo[vt] + k, g_hi[nt - 1]).astype(jnp.int32)
    vf = (jnp.where(valid & (k == 0), VISIT_FIRST, 0)
          + jnp.where(valid & (k == n_vis[vt] - 1), VISIT_LAST, 0)
          + jnp.where(valid, VISIT_VALID, 0)).astype(jnp.int32)
    return inv.reshape(nt, 1, ts), vt, vg, vf


def _moe_body(vt_ref, vg_ref, vf_ref, inv_cur_ref, inv_nxt_ref, inv_prv_ref, xa_hbm, gm_ref,
              wg_ref, wu_ref, wd_ref, gf_ref, out_hbm, xbuf, obuf, hm_ref, yacc_ref, gsem, ssem,
              *, ts, nt, sub):
    v = pl.program_id(0)
    t, grp, flags = vt_ref[v], vg_ref[v], vf_ref[v]
    slot = lax.rem(t, 2)
    first = (flags & VISIT_FIRST) != 0
    overlapped = first & (t >= 1)

    def gather_rows(idx_ref, dst_slot, lo, hi):
        for r in range(lo, min(hi, ts)):
            pltpu.make_async_copy(xa_hbm.at[pl.ds(idx_ref[0, r], 1)],
                                  xbuf.at[dst_slot, pl.ds(r, 1)], gsem.at[dst_slot]).start()

    def gather_wait(dst_slot):
        pltpu.make_async_copy(xa_hbm.at[pl.ds(0, ts)], xbuf.at[dst_slot], gsem.at[dst_slot]).wait()

    def scatter_rows(idx_ref, src_slot, lo, hi):
        for r in range(lo, min(hi, ts)):
            pltpu.make_async_copy(obuf.at[src_slot, pl.ds(r, 1)],
                                  out_hbm.at[pl.ds(idx_ref[0, r], 1)], ssem.at[src_slot]).start()

    def scatter_wait(src_slot):
        pltpu.make_async_copy(obuf.at[src_slot], out_hbm.at[pl.ds(0, ts)], ssem.at[src_slot]).wait()

    def experts(issue_dma):
        lane = lax.broadcasted_iota(jnp.int32, (sub, META_LANES), 1)
        n_sub = ts // sub
        dots_per_sub = 3 * EXPERTS_PER_GROUP
        batch = -(-ts // (dots_per_sub * max(n_sub // 2, 1)))
        done = [0, 0]

        def after_dot(r):
            if not issue_dma:
                return
            kind = 0 if r < max(n_sub // 2, 1) else 1
            lo = done[kind]
            if kind == 0:
                gather_rows(inv_nxt_ref, 1 - slot, lo, lo + batch)
            else:
                scatter_rows(inv_prv_ref, 1 - slot, lo, lo + batch)
            done[kind] = min(lo + batch, ts)

        for r in range(n_sub):
            rows = slice(r * sub, (r + 1) * sub)
            meta = xbuf[slot, rows, D_MODEL:]
            row_grp = jnp.sum(jnp.where(lane == LANE_GROUP, meta, 0.0), axis=-1, keepdims=True)
            member = row_grp == grp.astype(F32)
            hm = hm_ref[rows, :]
            y = jnp.zeros((sub, D_MODEL), F32)
            for e in range(EXPERTS_PER_GROUP):
                w = jnp.sum(jnp.where(lane == e, meta, 0.0), axis=-1, keepdims=True)
                w = jnp.where(member, w, 0.0)
                a = _dot(hm, wg_ref[e])
                after_dot(r)
                b = _dot(hm, wu_ref[e])
                after_dot(r)
                u = (a * _sigmoid(a)) * b * w
                y = y + _dot(u.astype(BF16), wd_ref[e])
                after_dot(r)
            yacc_ref[rows, :] += y
        if issue_dma:
            assert done == [ts, ts], done

    @pl.when(v == 0)
    def _():
        gather_rows(inv_cur_ref, 0, 0, ts)

    @pl.when(first)
    def _():
        gather_wait(slot)

        @pl.when(t == 0)
        def _():
            gather_rows(inv_nxt_ref, 1, 0, ts)

        hm_ref[...] = _rms(xbuf[slot, :, :D_MODEL], gm_ref[...]).astype(BF16)
        yacc_ref[...] = jnp.zeros_like(yacc_ref)

    @pl.when(overlapped)
    def _():
        experts(True)

    @pl.when(((flags & VISIT_VALID) != 0) & jnp.logical_not(overlapped))
    def _():
        experts(False)

    @pl.when((flags & VISIT_LAST) != 0)
    def _():
        @pl.when(t >= 2)
        def _():
            scatter_wait(slot)

        obuf[slot] = _rms(xbuf[slot, :, :D_MODEL] + yacc_ref[...], gf_ref[...])

    @pl.when(v == pl.num_programs(0) - 1)
    def _():
        last_slot = (nt - 1) % 2
        scatter_rows(inv_cur_ref, last_slot, 0, ts)
        scatter_wait(0)
        scatter_wait(1)
        gather_wait(1 - last_slot)


def _moe_sorted(xa, inv, vt, vg, vf, gm, wg, wu, wd, gf, *, ts, sub):
    N = xa.shape[0]
    nt = N // ts
    assert nt >= 2
    E, D, FF = EXPERTS_PER_GROUP, D_MODEL, EXPERT_FF
    grp = lambda v, vt, vg, vf: (vg[v], 0, 0)
    const = lambda v, vt, vg, vf: (0, 0)
    smem_row = lambda fn: pl.BlockSpec((None, 1, ts), fn, memory_space=pltpu.SMEM)
    grid_spec = pltpu.PrefetchScalarGridSpec(
        num_scalar_prefetch=3,
        grid=(vt.shape[0],),
        in_specs=[
            smem_row(lambda v, vt, vg, vf: (vt[v], 0, 0)),
            smem_row(lambda v, vt, vg, vf: (jnp.minimum(vt[v] + 1, nt - 1), 0, 0)),
            smem_row(lambda v, vt, vg, vf: (jnp.maximum(vt[v] - 1, 0), 0, 0)),
            pl.BlockSpec(memory_space=pl.ANY),
            pl.BlockSpec((1, D), const),
            pl.BlockSpec((E, D, FF), grp), pl.BlockSpec((E, D, FF), grp),
            pl.BlockSpec((E, FF, D), grp),
            pl.BlockSpec((1, D), const),
        ],
        out_specs=pl.BlockSpec(memory_space=pl.ANY),
        scratch_shapes=[
            pltpu.VMEM((2, ts, XA_WIDTH), F32), pltpu.VMEM((2, ts, D), F32),
            pltpu.VMEM((ts, D), BF16), pltpu.VMEM((ts, D), F32),
            pltpu.SemaphoreType.DMA((2,)), pltpu.SemaphoreType.DMA((2,)),
        ],
    )
    return pl.pallas_call(
        functools.partial(_moe_body, ts=ts, nt=nt, sub=sub),
        grid_spec=grid_spec,
        out_shape=jax.ShapeDtypeStruct((N, D), F32),
        compiler_params=pltpu.CompilerParams(
            dimension_semantics=("arbitrary",), vmem_limit_bytes=VMEM_LIMIT),
        name="moe_sorted",
    )(vt, vg, vf, inv, inv, inv, xa, gm, wg, wu, wd, gf)


def kernel(x, mem, norm_mix_g, w_in, fox_bf, conv_dw_w, conv_dw_b, conv_ln_g, conv_ln_b, conv_pw_w, attn_branch_w, w_out, norm_xa_g, norm_mem_g, xa_wq, xa_wk, xa_wv, xa_wo, norm_moe_g, router_group_w, router_group_b, router_expert_w, router_expert_b, expert_w_gate, expert_w_up, expert_w_down, norm_final_g):
    B, T, D = x.shape
    depth = w_in.shape[0]
    assert depth == 1, "the final norm is fused into the last layer's MoE kernel"
    row = lambda v: v.reshape(1, -1).astype(F32)
    c_glu = 2 * CONV_DIM
    c_q, c_f = c_glu, c_glu + 3 * FOX_DIM
    c_gc = c_f + FOX_HEADS
    c_ga = c_gc + D
    for l in range(depth):
        w = w_in[l]
        u, gc, ga, qkv_t, f_t = _proj_in(
            x, row(norm_mix_g[l]),
            w[:, :c_glu].astype(BF16), w[:, c_gc:c_ga].astype(BF16), w[:, c_ga:].astype(BF16),
            w[:, c_q:c_f].T.astype(BF16), w[:, c_f:c_gc].T.astype(BF16), tm=512)
        qaux, kaux = _fox_prep(f_t, fox_bf[l].reshape(FOX_HEADS, 1).astype(F32))
        attn_t = _fox_attn(qkv_t, qaux, kaux, tq=1024)
        x = _mix_out(x, u, gc, ga, attn_t, conv_dw_w[l].astype(F32), row(conv_dw_b[l]),
                     row(conv_ln_g[l]), row(conv_ln_b[l]), conv_pw_w[l].astype(BF16),
                     attn_branch_w[l].astype(BF16), w_out[l].astype(BF16), tc=256)
        kx, vx = _mem_kv(mem, row(norm_mem_g[l]), xa_wk[l].astype(BF16), xa_wv[l].astype(BF16))
        wr = jnp.zeros((D, ROUTE_LANES), F32)
        wr = wr.at[:, :N_GROUPS].set(router_group_w[l]).at[:, N_GROUPS:N_GROUPS + N_EXPERTS].set(
            router_expert_w[l])
        br = jnp.zeros((1, ROUTE_LANES), F32)
        br = br.at[0, :N_GROUPS].set(router_group_b[l]).at[0, N_GROUPS:N_GROUPS + N_EXPERTS].set(
            router_expert_b[l])
        xa, cnt = _xattn_route(x, row(norm_xa_g[l]), xa_wq[l].astype(BF16), kx, vx,
                               xa_wo[l].astype(BF16), row(norm_moe_g[l]), wr.astype(BF16), br,
                               tx=512)
        xa = xa.reshape(B * T, XA_WIDTH)
        inv, vt, vg, vf = _moe_plan(xa[:, D + LANE_GROUP].astype(jnp.int32),
                                    xa[:, D + LANE_RANK].astype(jnp.int32),
                                    cnt[0, :N_GROUPS].astype(jnp.int32), ts=512)
        x = _moe_sorted(xa, inv, vt, vg, vf, row(norm_moe_g[l]), expert_w_gate[l].astype(BF16),
                        expert_w_up[l].astype(BF16), expert_w_down[l].astype(BF16),
                        row(norm_final_g), ts=512, sub=256).reshape(B, T, D)
    return x
```

```python
import functools

import jax
import jax.numpy as jnp
from jax import lax
from jax.experimental import pallas as pl
from jax.experimental.pallas import tpu as pltpu

D_MODEL = 1024
MEM_LEN = 256
EPS = 1e-6
CONV_DIM = 512
CONV_WIDTH = 31
FOX_HEADS = 8
FOX_HEAD_DIM = 64
FOX_DIM = FOX_HEADS * FOX_HEAD_DIM
XA_HEADS = 4
XA_HEAD_DIM = 128
XA_DIM = XA_HEADS * XA_HEAD_DIM
N_GROUPS = 4
EXPERTS_PER_GROUP = 4
N_EXPERTS = N_GROUPS * EXPERTS_PER_GROUP
EXPERT_FF = 256

AUX_ROWS = 16
LOG2E = 1.4426950408889634
VMEM_LIMIT = 56 * 1024 * 1024

F32 = jnp.float32
BF16 = jnp.bfloat16


def _const_spec(shape):
    n = len(shape)
    return pl.BlockSpec(shape, lambda *_: (0,) * n)


def _rms(x, g):
    return x * lax.rsqrt(jnp.mean(x * x, axis=-1, keepdims=True) + EPS) * g


def _sigmoid(x):
    return 1.0 / (1.0 + jnp.exp(-x))


def _dot(a, b):
    return jnp.dot(a, b, preferred_element_type=F32)


def _dot_nt(a, b):
    return lax.dot_general(a, b, (((1,), (1,)), ((), ())), preferred_element_type=F32)


def _dot_tn(a, b):
    return lax.dot_general(a, b, (((0,), (0,)), ((), ())), preferred_element_type=F32)


def _proj_in_body(x_ref, g_ref, wglu_ref, wgc_ref, wga_ref, wqkv_t_ref, wf_t_ref,
                  u_ref, gc_ref, ga_ref, qkv_t_ref, f_t_ref):
    h = _rms(x_ref[...], g_ref[...]).astype(BF16)
    glu = _dot(h, wglu_ref[...])
    u_ref[...] = glu[:, :CONV_DIM] * _sigmoid(glu[:, CONV_DIM:])
    gc_ref[...] = _dot(h, wgc_ref[...]).astype(BF16)
    ga_ref[...] = _dot(h, wga_ref[...]).astype(BF16)
    qkv_t = _dot_nt(wqkv_t_ref[...], h)
    scale = FOX_HEAD_DIM ** -0.5 * LOG2E
    qkv_t_ref[:FOX_DIM, :] = (qkv_t[:FOX_DIM] * scale).astype(BF16)
    qkv_t_ref[FOX_DIM:, :] = qkv_t[FOX_DIM:].astype(BF16)
    f_t_ref[...] = _dot_nt(wf_t_ref[...], h)


def _proj_in(x, g, wglu, wgc, wga, wqkv_t, wf_t, *, tm):
    B, T, D = x.shape
    grid = (B, T // tm)
    row = lambda b, i: (b, i, 0)
    col = lambda b, i: (b, 0, i)
    return pl.pallas_call(
        _proj_in_body,
        grid=grid,
        in_specs=[
            pl.BlockSpec((None, tm, D), row),
            _const_spec((1, D)),
            _const_spec(wglu.shape), _const_spec(wgc.shape), _const_spec(wga.shape),
            _const_spec(wqkv_t.shape), _const_spec(wf_t.shape),
        ],
        out_specs=[
            pl.BlockSpec((None, tm, CONV_DIM), row),
            pl.BlockSpec((None, tm, D), row),
            pl.BlockSpec((None, tm, D), row),
            pl.BlockSpec((None, 3 * FOX_DIM, tm), col),
            pl.BlockSpec((None, FOX_HEADS, tm), col),
        ],
        out_shape=[
            jax.ShapeDtypeStruct((B, T, CONV_DIM), F32),
            jax.ShapeDtypeStruct((B, T, D), BF16),
            jax.ShapeDtypeStruct((B, T, D), BF16),
            jax.ShapeDtypeStruct((B, 3 * FOX_DIM, T), BF16),
            jax.ShapeDtypeStruct((B, FOX_HEADS, T), F32),
        ],
        compiler_params=pltpu.CompilerParams(
            dimension_semantics=("parallel", "parallel"), vmem_limit_bytes=VMEM_LIMIT),
        name="proj_in",
    )(x, g, wglu, wgc, wga, wqkv_t, wf_t)


def _split3(c):
    hi = c.astype(BF16)
    r = c - hi.astype(F32)
    mid = r.astype(BF16)
    lo = (r - mid.astype(F32)).astype(BF16)
    return hi, mid, lo


def _fox_prep_body(f_t_ref, bf_ref, qaux_ref, kaux_ref, c_ref, *, chunk):
    T = f_t_ref.shape[-1]
    z = f_t_ref[...] + bf_ref[...]
    logf = jnp.minimum(z, 0.0) - jnp.log1p(jnp.exp(-jnp.abs(z)))
    r_i = lax.broadcasted_iota(jnp.int32, (chunk, chunk), 0)
    c_i = lax.broadcasted_iota(jnp.int32, (chunk, chunk), 1)
    tri = jnp.where(r_i <= c_i, 1.0, 0.0).astype(BF16)
    carry = jnp.zeros((FOX_HEADS, 1), F32)
    for j in range(T // chunk):
        blk = logf[:, j * chunk:(j + 1) * chunk]
        hi, mid, lo = _split3(blk)
        cs = (_dot(lo, tri) + _dot(mid, tri)) + _dot(hi, tri) + carry
        c_ref[:, j * chunk:(j + 1) * chunk] = cs
        carry = cs[:, chunk - 1:chunk]
    row = lax.broadcasted_iota(jnp.int32, (AUX_ROWS, T), 0)
    for h in range(FOX_HEADS):
        hi, mid, lo = _split3(c_ref[h:h + 1, :] * LOG2E)
        pieces = [jnp.broadcast_to(p.astype(F32), (AUX_ROWS, T)) for p in (hi, mid, lo)]
        one = jnp.where(row < 3, 1.0, 0.0)
        val = jnp.where(row == 0, pieces[0], jnp.where(row == 1, pieces[1], pieces[2]))
        val_k = jnp.where(row == 3, pieces[0], jnp.where(row == 4, pieces[1], pieces[2]))
        qaux = jnp.where(row < 3, val, jnp.where(row < 6, 1.0, 0.0))
        kaux = jnp.where(row < 3, one, jnp.where(row < 6, -val_k, 0.0))
        qaux_ref[h] = qaux.astype(BF16)
        kaux_ref[h] = kaux.astype(BF16)


def _fox_prep(f_t, bf):
    B, H, T = f_t.shape
    aux = jax.ShapeDtypeStruct((B, H, AUX_ROWS, T), BF16)
    aux_spec = pl.BlockSpec((None, H, AUX_ROWS, T), lambda b: (b, 0, 0, 0))
    return pl.pallas_call(
        functools.partial(_fox_prep_body, chunk=512),
        grid=(B,),
        in_specs=[pl.BlockSpec((None, H, T), lambda b: (b, 0, 0)), _const_spec((H, 1))],
        out_specs=[aux_spec, aux_spec],
        out_shape=[aux, aux],
        scratch_shapes=[pltpu.VMEM((H, T), F32)],
        compiler_params=pltpu.CompilerParams(dimension_semantics=("parallel",)),
        name="fox_prep",
    )(f_t, bf)


MASK_BLOCK = 256


def _fox_attn_body(q_ref, k_ref, v_ref, qaux_ref, kaux_ref, o_ref, sb0_ref, sb1_ref, p_ref, *, tq):
    T = q_ref.shape[-1]
    dh = FOX_HEAD_DIM
    mb = MASK_BLOCK
    nj = tq // mb
    kk = lax.broadcasted_iota(jnp.int32, (mb, mb), 0)
    qq = lax.broadcasted_iota(jnp.int32, (mb, mb), 1)
    causal = kk <= qq
    ones = jnp.where(lax.broadcasted_iota(jnp.int32, (AUX_ROWS, tq), 0) == 0, 1.0, 0.0).astype(BF16)
    sbufs = (sb0_ref, sb1_ref)

    tasks = []
    for qi in range(T // tq):
        tasks.append((qi, qi, True))
        tasks += [(qi, c, False) for c in range(qi)]

    def scores(task, j, sb, mc):
        qi, c, diag = task
        c0 = j * mb if diag else 0
        r0 = c * tq + j * mb
        ka = jnp.concatenate([k_ref[:, r0:r0 + mb], kaux_ref[:, r0:r0 + mb]], axis=0)
        qa = jnp.concatenate([q_ref[:, qi * tq + c0:(qi + 1) * tq],
                              qaux_ref[:, qi * tq + c0:(qi + 1) * tq]], axis=0)
        s = _dot_tn(ka, qa)
        if diag:
            head = jnp.where(causal, s[:, :mb], -jnp.inf)
            s = head if tq - c0 == mb else jnp.concatenate([head, s[:, mb:]], axis=1)
        sb[j * mb:(j + 1) * mb, c0:] = s
        cm = jnp.max(s, axis=0, keepdims=True)
        if c0:
            cm = jnp.concatenate([jnp.full((1, c0), -jnp.inf, F32), cm], axis=1)
        return cm if mc is None else jnp.maximum(mc, cm)

    def probs(task, j, sb, m_new):
        _, _, diag = task
        c0 = j * mb if diag else 0
        rows = slice(j * mb, (j + 1) * mb)
        p_ref[rows, c0:] = jnp.exp2(sb[rows, c0:] - m_new[:, c0:]).astype(BF16)

    def values(task):
        _, c, diag = task
        va = jnp.concatenate([v_ref[:, c * tq:(c + 1) * tq], ones], axis=0)
        if not diag:
            return _dot(va, p_ref[...])
        return jnp.concatenate(
            [_dot(va[:, :(bj + 1) * mb], p_ref[:(bj + 1) * mb, bj * mb:(bj + 1) * mb])
             for bj in range(nj)], axis=1)

    mc = None
    for j in range(nj):
        mc = scores(tasks[0], j, sbufs[0], mc)
    m_run = acc = None
    for k, task in enumerate(tasks):
        qi, c, diag = task
        nxt = tasks[k + 1] if k + 1 < len(tasks) else None
        m_new = mc if diag else jnp.maximum(m_run, mc)
        mc = None
        for j in range(nj):
            if nxt is not None:
                mc = scores(nxt, j, sbufs[(k + 1) % 2], mc)
            probs(task, j, sbufs[k % 2], m_new)
        pv = values(task)
        acc = pv if diag else jnp.exp2(m_run - m_new) * acc + pv
        m_run = m_new
        if nxt is None or nxt[0] != qi:
            o_ref[:, qi * tq:(qi + 1) * tq] = (acc[:dh] / acc[dh:dh + 1]).astype(o_ref.dtype)


def _fox_attn(qkv_t, qaux, kaux, *, tq):
    B, _, T = qkv_t.shape
    H, dh = FOX_HEADS, FOX_HEAD_DIM
    part = lambda off: pl.BlockSpec((None, dh, T), lambda b, h: (b, off * H + h, 0))
    aux = pl.BlockSpec((None, None, AUX_ROWS, T), lambda b, h: (b, h, 0, 0))
    return pl.pallas_call(
        functools.partial(_fox_attn_body, tq=tq),
        grid=(B, H),
        in_specs=[part(0), part(1), part(2), aux, aux],
        out_specs=pl.BlockSpec((None, dh, T), lambda b, h: (b, h, 0)),
        out_shape=jax.ShapeDtypeStruct((B, H * dh, T), BF16),
        scratch_shapes=[pltpu.VMEM((tq, tq), F32), pltpu.VMEM((tq, tq), F32),
                        pltpu.VMEM((tq, tq), BF16)],
        compiler_params=pltpu.CompilerParams(
            dimension_semantics=("parallel", "parallel"), vmem_limit_bytes=VMEM_LIMIT),
        name="fox_attn",
    )(qkv_t, qkv_t, qkv_t, qaux, kaux)


HALO = 32
SUBLANES = 8


def _mix_out_body(x_ref, u_ref, uprev_ref, gc_ref, ga_ref, attn_t_ref, dww_ref, dwb_ref,
                  lng_ref, lnb_ref, pw_ref, abw_ref, wout_ref, o_ref, ubuf_ref, ushift_ref,
                  *, tc):
    first = pl.program_id(1) == 0
    ubuf_ref[:HALO, :] = jnp.where(first, 0.0, uprev_ref[...])
    ubuf_ref[HALO:, :] = u_ref[...]
    span = tc + HALO - SUBLANES
    for s in range(1, SUBLANES):
        ushift_ref[s - 1, :, :] = ubuf_ref[s:s + span, :]
    acc = jnp.zeros((tc, CONV_DIM), F32)
    for k in range(CONV_WIDTH):
        off = HALO - (CONV_WIDTH - 1) + k
        s, base = off % SUBLANES, off - off % SUBLANES
        rows = ubuf_ref[base:base + tc, :] if s == 0 else ushift_ref[s - 1, base:base + tc, :]
        acc = acc + rows * dww_ref[k:k + 1, :]
    acc = acc + dwb_ref[...]
    mu = jnp.mean(acc, axis=-1, keepdims=True)
    d = acc - mu
    var = jnp.mean(d * d, axis=-1, keepdims=True)
    y = d * lax.rsqrt(var + EPS) * lng_ref[...] + lnb_ref[...]
    y = (y * _sigmoid(y)).astype(BF16)
    conv_out = _dot(y, pw_ref[...])
    attn_out = _dot_tn(attn_t_ref[...], abw_ref[...])
    merged = (_sigmoid(gc_ref[...].astype(F32)) * conv_out
              + _sigmoid(ga_ref[...].astype(F32)) * attn_out)
    o_ref[...] = x_ref[...] + _dot(merged.astype(BF16), wout_ref[...])


def _mix_out(x, u, gc, ga, attn_t, dww, dwb, lng, lnb, pw, abw, wout, *, tc):
    B, T, D = x.shape
    row = lambda b, i: (b, i, 0)
    halo_blocks = tc // HALO
    return pl.pallas_call(
        functools.partial(_mix_out_body, tc=tc),
        grid=(B, T // tc),
        in_specs=[
            pl.BlockSpec((None, tc, D), row),
            pl.BlockSpec((None, tc, CONV_DIM), row),
            pl.BlockSpec((None, HALO, CONV_DIM),
                         lambda b, i: (b, jnp.maximum(i * halo_blocks - 1, 0), 0)),
            pl.BlockSpec((None, tc, D), row),
            pl.BlockSpec((None, tc, D), row),
            pl.BlockSpec((None, FOX_DIM, tc), lambda b, i: (b, 0, i)),
            _const_spec(dww.shape), _const_spec(dwb.shape), _const_spec(lng.shape),
            _const_spec(lnb.shape), _const_spec(pw.shape), _const_spec(abw.shape),
            _const_spec(wout.shape),
        ],
        out_specs=pl.BlockSpec((None, tc, D), row),
        out_shape=jax.ShapeDtypeStruct((B, T, D), F32),
        scratch_shapes=[pltpu.VMEM((HALO + tc, CONV_DIM), F32),
                        pltpu.VMEM((SUBLANES - 1, HALO + tc - SUBLANES, CONV_DIM), F32)],
        compiler_params=pltpu.CompilerParams(
            dimension_semantics=("parallel", "parallel"), vmem_limit_bytes=VMEM_LIMIT),
        name="mix_out",
    )(x, u, u, gc, ga, attn_t, dww, dwb, lng, lnb, pw, abw, wout)


def _mem_kv_body(m_ref, g_ref, wk_ref, wv_ref, k_ref, v_ref):
    h = _rms(m_ref[...], g_ref[...]).astype(BF16)
    k_ref[...] = _dot(h, wk_ref[...]).astype(BF16)
    v_ref[...] = _dot(h, wv_ref[...]).astype(BF16)


def _mem_kv(mem, g, wk, wv):
    B, M, D = mem.shape
    spec = pl.BlockSpec((None, M, XA_DIM), lambda b: (b, 0, 0))
    out = jax.ShapeDtypeStruct((B, M, XA_DIM), BF16)
    return pl.pallas_call(
        _mem_kv_body,
        grid=(B,),
        in_specs=[pl.BlockSpec((None, M, D), lambda b: (b, 0, 0)), _const_spec((1, D)),
                  _const_spec(wk.shape), _const_spec(wv.shape)],
        out_specs=[spec, spec],
        out_shape=[out, out],
        compiler_params=pltpu.CompilerParams(dimension_semantics=("parallel",)),
        name="mem_kv",
    )(mem, g, wk, wv)


ROUTE_LANES = 128
META_LANES = 128
XA_WIDTH = D_MODEL + META_LANES
LANE_GROUP = EXPERTS_PER_GROUP
LANE_RANK = EXPERTS_PER_GROUP + 1


def _first_max_lane(v, valid, lane):
    neg = jnp.where(valid, v, -jnp.inf)
    mx = jnp.max(neg, axis=-1, keepdims=True)
    idx = jnp.min(jnp.where(valid & (neg == mx), lane, ROUTE_LANES), axis=-1, keepdims=True)
    return mx, idx


def _route(logits, run):
    n = logits.shape[0]
    lane = lax.broadcasted_iota(jnp.int32, logits.shape, 1)
    is_g = lane < N_GROUPS
    gmax, g_star = _first_max_lane(logits, is_g, lane)
    p_gsel = 1.0 / jnp.sum(jnp.where(is_g, jnp.exp(logits - gmax), 0.0), axis=-1, keepdims=True)
    e_lane0 = N_GROUPS + g_star * EXPERTS_PER_GROUP
    in_grp = (lane >= e_lane0) & (lane < e_lane0 + EXPERTS_PER_GROUP)
    emax, _ = _first_max_lane(logits, in_grp, lane)
    ex = jnp.where(in_grp, jnp.exp(logits - emax), 0.0)
    p_exp = ex / jnp.sum(ex, axis=-1, keepdims=True)
    v1, i1 = _first_max_lane(p_exp, in_grp, lane)
    v2, i2 = _first_max_lane(p_exp, in_grp & (lane != i1), lane)
    tot = v1 + v2
    w = jnp.where(lane == i1 - e_lane0, p_gsel * (v1 / tot),
                  jnp.where(lane == i2 - e_lane0, p_gsel * (v2 / tot), 0.0))
    onehot = jnp.where(lane == g_star, 1.0, 0.0).astype(BF16)
    r_i = lax.broadcasted_iota(jnp.int32, (n, n), 0)
    c_i = lax.broadcasted_iota(jnp.int32, (n, n), 1)
    cum = _dot(jnp.where(r_i >= c_i, 1.0, 0.0).astype(BF16), onehot) + run
    rank = jnp.sum(jnp.where(lane == g_star, cum, 0.0), axis=-1, keepdims=True) - 1.0
    meta = (w + jnp.where(lane == LANE_GROUP, g_star.astype(F32), 0.0)
            + jnp.where(lane == LANE_RANK, rank, 0.0))
    return meta, cum[n - 1:n, :]


def _xattn_body(x_ref, g_ref, wq_ref, k_ref, v_ref, wo_ref, gm_ref, wr_ref, br_ref,
                xa_ref, cnt_ref, run_ref):
    @pl.when((pl.program_id(0) == 0) & (pl.program_id(1) == 0))
    def _():
        run_ref[...] = jnp.zeros_like(run_ref)

    x = x_ref[...]
    q = _dot(_rms(x, g_ref[...]).astype(BF16), wq_ref[...]).astype(BF16)
    scale = XA_HEAD_DIM ** -0.5
    heads = []
    for h in range(XA_HEADS):
        sl = slice(h * XA_HEAD_DIM, (h + 1) * XA_HEAD_DIM)
        s = _dot_nt(q[:, sl], k_ref[:, sl]) * scale
        p = jnp.exp(s - jnp.max(s, axis=-1, keepdims=True))
        p = p / jnp.sum(p, axis=-1, keepdims=True)
        heads.append(_dot(p.astype(BF16), v_ref[:, sl]).astype(BF16))
    o = jnp.concatenate(heads, axis=-1)
    x2 = x + _dot(o, wo_ref[...])
    hm = _rms(x2, gm_ref[...]).astype(BF16)
    meta, run = _route(_dot(hm, wr_ref[...]) + br_ref[...], run_ref[...])
    run_ref[...] = run
    cnt_ref[...] = run
    xa_ref[:, :D_MODEL] = x2
    xa_ref[:, D_MODEL:] = meta


def _xattn_route(x, g, wq, kx, vx, wo, gm, wr, br, *, tx):
    B, T, D = x.shape
    row = lambda b, i: (b, i, 0)
    kv = pl.BlockSpec((None, MEM_LEN, XA_DIM), lambda b, i: (b, 0, 0))
    return pl.pallas_call(
        _xattn_body,
        grid=(B, T // tx),
        in_specs=[pl.BlockSpec((None, tx, D), row), _const_spec((1, D)),
                  _const_spec(wq.shape), kv, kv, _const_spec(wo.shape), _const_spec((1, D)),
                  _const_spec(wr.shape), _const_spec(br.shape)],
        out_specs=[pl.BlockSpec((None, tx, XA_WIDTH), row), _const_spec((1, ROUTE_LANES))],
        out_shape=[jax.ShapeDtypeStruct((B, T, XA_WIDTH), F32),
                   jax.ShapeDtypeStruct((1, ROUTE_LANES), F32)],
        scratch_shapes=[pltpu.VMEM((1, ROUTE_LANES), F32)],
        compiler_params=pltpu.CompilerParams(
            dimension_semantics=("arbitrary", "arbitrary"), vmem_limit_bytes=VMEM_LIMIT),
        name="xattn_route",
    )(x, g, wq, kx, vx, wo, gm, wr, br)


VISIT_FIRST, VISIT_LAST, VISIT_VALID = 1, 2, 4


def _moe_plan(group, rank, counts, *, ts):
    n = group.shape[0]
    nt = n // ts
    nv = nt + N_GROUPS - 1
    ends = jnp.cumsum(counts)
    pos = (ends - counts)[group] + rank
    inv = jnp.zeros((n,), jnp.int32).at[pos].set(jnp.arange(n, dtype=jnp.int32))
    tile0 = jnp.arange(nt, dtype=jnp.int32) * ts
    g_lo = jnp.sum(ends[None, :] <= tile0[:, None], axis=1).astype(jnp.int32)
    g_hi = jnp.sum(ends[None, :] <= tile0[:, None] + (ts - 1), axis=1).astype(jnp.int32)
    n_vis = g_hi - g_lo + 1
    start = jnp.cumsum(n_vis) - n_vis
    v = jnp.arange(nv, dtype=jnp.int32)
    vt = jnp.minimum(jnp.sum(start[None, :] <= v[:, None], axis=1) - 1, nt - 1).astype(jnp.int32)
    k = v - start[vt]
    valid = v < start[-1] + n_vis[-1]
    vg = jnp.where(valid, g_lo[vt] + k, g_hi[nt - 1]).astype(jnp.int32)
    vf = (jnp.where(valid & (k == 0), VISIT_FIRST, 0)
          + jnp.where(valid & (k == n_vis[vt] - 1), VISIT_LAST, 0)
          + jnp.where(valid, VISIT_VALID, 0)).astype(jnp.int32)
    return inv.reshape(nt, 1, ts), vt, vg, vf


def _moe_body(vt_ref, vg_ref, vf_ref, inv_cur_ref, inv_nxt_ref, inv_prv_ref, xa_hbm, gm_ref,
              wg_ref, wu_ref, wd_ref, gf_ref, out_hbm, xbuf, obuf, hm_ref, yacc_ref, gsem, ssem,
              *, ts, nt, sub):
    v = pl.program_id(0)
    t, grp, flags = vt_ref[v], vg_ref[v], vf_ref[v]
    slot = lax.rem(t, 2)
    first = (flags & VISIT_FIRST) != 0
    overlapped = first & (t >= 1)

    def gather_rows(idx_ref, dst_slot, lo, hi):
        for r in range(lo, min(hi, ts)):
            pltpu.make_async_copy(xa_hbm.at[pl.ds(idx_ref[0, r], 1)],
                                  xbuf.at[dst_slot, pl.ds(r, 1)], gsem.at[dst_slot]).start()

    def gather_wait(dst_slot):
        pltpu.make_async_copy(xa_hbm.at[pl.ds(0, ts)], xbuf.at[dst_slot], gsem.at[dst_slot]).wait()

    def scatter_rows(idx_ref, src_slot, lo, hi):
        for r in range(lo, min(hi, ts)):
            pltpu.make_async_copy(obuf.at[src_slot, pl.ds(r, 1)],
                                  out_hbm.at[pl.ds(idx_ref[0, r], 1)], ssem.at[src_slot]).start()

    def scatter_wait(src_slot):
        pltpu.make_async_copy(obuf.at[src_slot], out_hbm.at[pl.ds(0, ts)], ssem.at[src_slot]).wait()

    def experts(issue_dma):
        lane = lax.broadcasted_iota(jnp.int32, (sub, META_LANES), 1)
        n_sub = ts // sub
        dots_per_sub = 3 * EXPERTS_PER_GROUP
        batch = -(-ts // (dots_per_sub * max(n_sub // 2, 1)))
        done = [0, 0]

        def after_dot(r):
            if not issue_dma:
                return
            kind = 0 if r < max(n_sub // 2, 1) else 1
            lo = done[kind]
            if kind == 0:
                gather_rows(inv_nxt_ref, 1 - slot, lo, lo + batch)
            else:
                scatter_rows(inv_prv_ref, 1 - slot, lo, lo + batch)
            done[kind] = min(lo + batch, ts)

        for r in range(n_sub):
            rows = slice(r * sub, (r + 1) * sub)
            meta = xbuf[slot, rows, D_MODEL:]
            row_grp = jnp.sum(jnp.where(lane == LANE_GROUP, meta, 0.0), axis=-1, keepdims=True)
            member = row_grp == grp.astype(F32)
            hm = hm_ref[rows, :]
            y = jnp.zeros((sub, D_MODEL), F32)
            for e in range(EXPERTS_PER_GROUP):
                w = jnp.sum(jnp.where(lane == e, meta, 0.0), axis=-1, keepdims=True)
                w = jnp.where(member, w, 0.0)
                a = _dot(hm, wg_ref[e])
                after_dot(r)
                b = _dot(hm, wu_ref[e])
                after_dot(r)
                u = (a * _sigmoid(a)) * b * w
                y = y + _dot(u.astype(BF16), wd_ref[e])
                after_dot(r)
            yacc_ref[rows, :] += y
        if issue_dma:
            assert done == [ts, ts], done

    @pl.when(v == 0)
    def _():
        gather_rows(inv_cur_ref, 0, 0, ts)

    @pl.when(first)
    def _():
        gather_wait(slot)

        @pl.when(t == 0)
        def _():
            gather_rows(inv_nxt_ref, 1, 0, ts)

        hm_ref[...] = _rms(xbuf[slot, :, :D_MODEL], gm_ref[...]).astype(BF16)
        yacc_ref[...] = jnp.zeros_like(yacc_ref)

    @pl.when(overlapped)
    def _():
        experts(True)

    @pl.when(((flags & VISIT_VALID) != 0) & jnp.logical_not(overlapped))
    def _():
        experts(False)

    @pl.when((flags & VISIT_LAST) != 0)
    def _():
        @pl.when(t >= 2)
        def _():
            scatter_wait(slot)

        obuf[slot] = _rms(xbuf[slot, :, :D_MODEL] + yacc_ref[...], gf_ref[...])

    @pl.when(v == pl.num_programs(0) - 1)
    def _():
        last_slot = (nt - 1) % 2
        scatter_rows(inv_cur_ref, last_slot, 0, ts)
        scatter_wait(0)
        scatter_wait(1)
        gather_wait(1 - last_slot)


def _moe_sorted(xa, inv, vt, vg, vf, gm, wg, wu, wd, gf, *, ts, sub):
    N = xa.shape[0]
    nt = N // ts
    assert nt >= 2
    E, D, FF = EXPERTS_PER_GROUP, D_MODEL, EXPERT_FF
    grp = lambda v, vt, vg, vf: (vg[v], 0, 0)
    const = lambda v, vt, vg, vf: (0, 0)
    smem_row = lambda fn: pl.BlockSpec((None, 1, ts), fn, memory_space=pltpu.SMEM)
    grid_spec = pltpu.PrefetchScalarGridSpec(
        num_scalar_prefetch=3,
        grid=(vt.shape[0],),
        in_specs=[
            smem_row(lambda v, vt, vg, vf: (vt[v], 0, 0)),
            smem_row(lambda v, vt, vg, vf: (jnp.minimum(vt[v] + 1, nt - 1), 0, 0)),
            smem_row(lambda v, vt, vg, vf: (jnp.maximum(vt[v] - 1, 0), 0, 0)),
            pl.BlockSpec(memory_space=pl.ANY),
            pl.BlockSpec((1, D), const),
            pl.BlockSpec((E, D, FF), grp), pl.BlockSpec((E, D, FF), grp),
            pl.BlockSpec((E, FF, D), grp),
            pl.BlockSpec((1, D), const),
        ],
        out_specs=pl.BlockSpec(memory_space=pl.ANY),
        scratch_shapes=[
            pltpu.VMEM((2, ts, XA_WIDTH), F32), pltpu.VMEM((2, ts, D), F32),
            pltpu.VMEM((ts, D), BF16), pltpu.VMEM((ts, D), F32),
            pltpu.SemaphoreType.DMA((2,)), pltpu.SemaphoreType.DMA((2,)),
        ],
    )
    return pl.pallas_call(
        functools.partial(_moe_body, ts=ts, nt=nt, sub=sub),
        grid_spec=grid_spec,
        out_shape=jax.ShapeDtypeStruct((N, D), F32),
        compiler_params=pltpu.CompilerParams(
            dimension_semantics=("arbitrary",), vmem_limit_bytes=VMEM_LIMIT),
        name="moe_sorted",
    )(vt, vg, vf, inv, inv, inv, xa, gm, wg, wu, wd, gf)


def kernel(x, mem, norm_mix_g, w_in, fox_bf, conv_dw_w, conv_dw_b, conv_ln_g, conv_ln_b, conv_pw_w, attn_branch_w, w_out, norm_xa_g, norm_mem_g, xa_wq, xa_wk, xa_wv, xa_wo, norm_moe_g, router_group_w, router_group_b, router_expert_w, router_expert_b, expert_w_gate, expert_w_up, expert_w_down, norm_final_g):
    B, T, D = x.shape
    depth = w_in.shape[0]
    assert depth == 1, "the final norm is fused into the last layer's MoE kernel"
    row = lambda v: v.reshape(1, -1).astype(F32)
    c_glu = 2 * CONV_DIM
    c_q, c_f = c_glu, c_glu + 3 * FOX_DIM
    c_gc = c_f + FOX_HEADS
    c_ga = c_gc + D
    for l in range(depth):
        w = w_in[l]
        u, gc, ga, qkv_t, f_t = _proj_in(
            x, row(norm_mix_g[l]),
            w[:, :c_glu].astype(BF16), w[:, c_gc:c_ga].astype(BF16), w[:, c_ga:].astype(BF16),
            w[:, c_q:c_f].T.astype(BF16), w[:, c_f:c_gc].T.astype(BF16), tm=512)
        qaux, kaux = _fox_prep(f_t, fox_bf[l].reshape(FOX_HEADS, 1).astype(F32))
        attn_t = _fox_attn(qkv_t, qaux, kaux, tq=1024)
        x = _mix_out(x, u, gc, ga, attn_t, conv_dw_w[l].astype(F32), row(conv_dw_b[l]),
                     row(conv_ln_g[l]), row(conv_ln_b[l]), conv_pw_w[l].astype(BF16),
                     attn_branch_w[l].astype(BF16), w_out[l].astype(BF16), tc=256)
        kx, vx = _mem_kv(mem, row(norm_mem_g[l]), xa_wk[l].astype(BF16), xa_wv[l].astype(BF16))
        wr = jnp.zeros((D, ROUTE_LANES), F32)
        wr = wr.at[:, :N_GROUPS].set(router_group_w[l]).at[:, N_GROUPS:N_GROUPS + N_EXPERTS].set(
            router_expert_w[l])
        br = jnp.zeros((1, ROUTE_LANES), F32)
        br = br.at[0, :N_GROUPS].set(router_group_b[l]).at[0, N_GROUPS:N_GROUPS + N_EXPERTS].set(
            router_expert_b[l])
        xa, cnt = _xattn_route(x, row(norm_xa_g[l]), xa_wq[l].astype(BF16), kx, vx,
                               xa_wo[l].astype(BF16), row(norm_moe_g[l]), wr.astype(BF16), br,
                               tx=512)
        xa = xa.reshape(B * T, XA_WIDTH)
        inv, vt, vg, vf = _moe_plan(xa[:, D + LANE_GROUP].astype(jnp.int32),
                                    xa[:, D + LANE_RANK].astype(jnp.int32),
                                    cnt[0, :N_GROUPS].astype(jnp.int32), ts=512)
        x = _moe_sorted(xa, inv, vt, vg, vf, row(norm_moe_g[l]), expert_w_gate[l].astype(BF16),
                        expert_w_up[l].astype(BF16), expert_w_down[l].astype(BF16),
                        row(norm_final_g), ts=512, sub=256).reshape(B, T, D)
    return x
```

```python
import functools

import jax
import jax.numpy as jnp
from jax import lax
from jax.experimental import pallas as pl
from jax.experimental.pallas import tpu as pltpu

D_MODEL = 1024
MEM_LEN = 256
EPS = 1e-6
CONV_DIM = 512
CONV_WIDTH = 31
FOX_HEADS = 8
FOX_HEAD_DIM = 64
FOX_DIM = FOX_HEADS * FOX_HEAD_DIM
XA_HEADS = 4
XA_HEAD_DIM = 128
XA_DIM = XA_HEADS * XA_HEAD_DIM
N_GROUPS = 4
EXPERTS_PER_GROUP = 4
N_EXPERTS = N_GROUPS * EXPERTS_PER_GROUP
EXPERT_FF = 256

AUX_ROWS = 16
LOG2E = 1.4426950408889634
VMEM_LIMIT = 56 * 1024 * 1024

F32 = jnp.float32
BF16 = jnp.bfloat16


def _const_spec(shape):
    n = len(shape)
    return pl.BlockSpec(shape, lambda *_: (0,) * n)


def _rms(x, g):
    return x * lax.rsqrt(jnp.mean(x * x, axis=-1, keepdims=True) + EPS) * g


def _sigmoid(x):
    return 1.0 / (1.0 + jnp.exp(-x))


def _dot(a, b):
    return jnp.dot(a, b, preferred_element_type=F32)


def _dot_nt(a, b):
    return lax.dot_general(a, b, (((1,), (1,)), ((), ())), preferred_element_type=F32)


def _dot_tn(a, b):
    return lax.dot_general(a, b, (((0,), (0,)), ((), ())), preferred_element_type=F32)


def _proj_in_body(x_ref, g_ref, wglu_ref, wgc_ref, wga_ref, wqkv_t_ref, wf_t_ref,
                  u_ref, gc_ref, ga_ref, qkv_t_ref, f_t_ref):
    h = _rms(x_ref[...], g_ref[...]).astype(BF16)
    glu = _dot(h, wglu_ref[...])
    u_ref[...] = glu[:, :CONV_DIM] * _sigmoid(glu[:, CONV_DIM:])
    gc_ref[...] = _dot(h, wgc_ref[...]).astype(BF16)
    ga_ref[...] = _dot(h, wga_ref[...]).astype(BF16)
    qkv_t = _dot_nt(wqkv_t_ref[...], h)
    scale = FOX_HEAD_DIM ** -0.5 * LOG2E
    qkv_t_ref[:FOX_DIM, :] = (qkv_t[:FOX_DIM] * scale).astype(BF16)
    qkv_t_ref[FOX_DIM:, :] = qkv_t[FOX_DIM:].astype(BF16)
    f_t_ref[...] = _dot_nt(wf_t_ref[...], h)


def _proj_in(x, g, wglu, wgc, wga, wqkv_t, wf_t, *, tm):
    B, T, D = x.shape
    grid = (B, T // tm)
    row = lambda b, i: (b, i, 0)
    col = lambda b, i: (b, 0, i)
    return pl.pallas_call(
        _proj_in_body,
        grid=grid,
        in_specs=[
            pl.BlockSpec((None, tm, D), row),
            _const_spec((1, D)),
            _const_spec(wglu.shape), _const_spec(wgc.shape), _const_spec(wga.shape),
            _const_spec(wqkv_t.shape), _const_spec(wf_t.shape),
        ],
        out_specs=[
            pl.BlockSpec((None, tm, CONV_DIM), row),
            pl.BlockSpec((None, tm, D), row),
            pl.BlockSpec((None, tm, D), row),
            pl.BlockSpec((None, 3 * FOX_DIM, tm), col),
            pl.BlockSpec((None, FOX_HEADS, tm), col),
        ],
        out_shape=[
            jax.ShapeDtypeStruct((B, T, CONV_DIM), F32),
            jax.ShapeDtypeStruct((B, T, D), BF16),
            jax.ShapeDtypeStruct((B, T, D), BF16),
            jax.ShapeDtypeStruct((B, 3 * FOX_DIM, T), BF16),
            jax.ShapeDtypeStruct((B, FOX_HEADS, T), F32),
        ],
        compiler_params=pltpu.CompilerParams(
            dimension_semantics=("parallel", "parallel"), vmem_limit_bytes=VMEM_LIMIT),
        name="proj_in",
    )(x, g, wglu, wgc, wga, wqkv_t, wf_t)


def _split3(c):
    hi = c.astype(BF16)
    r = c - hi.astype(F32)
    mid = r.astype(BF16)
    lo = (r - mid.astype(F32)).astype(BF16)
    return hi, mid, lo


def _fox_prep_body(f_t_ref, bf_ref, qaux_ref, kaux_ref, c_ref, *, chunk):
    T = f_t_ref.shape[-1]
    z = f_t_ref[...] + bf_ref[...]
    logf = jnp.minimum(z, 0.0) - jnp.log1p(jnp.exp(-jnp.abs(z)))
    r_i = lax.broadcasted_iota(jnp.int32, (chunk, chunk), 0)
    c_i = lax.broadcasted_iota(jnp.int32, (chunk, chunk), 1)
    tri = jnp.where(r_i <= c_i, 1.0, 0.0).astype(BF16)
    carry = jnp.zeros((FOX_HEADS, 1), F32)
    for j in range(T // chunk):
        blk = logf[:, j * chunk:(j + 1) * chunk]
        hi, mid, lo = _split3(blk)
        cs = (_dot(lo, tri) + _dot(mid, tri)) + _dot(hi, tri) + carry
        c_ref[:, j * chunk:(j + 1) * chunk] = cs
        carry = cs[:, chunk - 1:chunk]
    row = lax.broadcasted_iota(jnp.int32, (AUX_ROWS, T), 0)
    for h in range(FOX_HEADS):
        hi, mid, lo = _split3(c_ref[h:h + 1, :] * LOG2E)
        pieces = [jnp.broadcast_to(p.astype(F32), (AUX_ROWS, T)) for p in (hi, mid, lo)]
        one = jnp.where(row < 3, 1.0, 0.0)
        val = jnp.where(row == 0, pieces[0], jnp.where(row == 1, pieces[1], pieces[2]))
        val_k = jnp.where(row == 3, pieces[0], jnp.where(row == 4, pieces[1], pieces[2]))
        qaux = jnp.where(row < 3, val, jnp.where(row < 6, 1.0, 0.0))
        kaux = jnp.where(row < 3, one, jnp.where(row < 6, -val_k, 0.0))
        qaux_ref[h] = qaux.astype(BF16)
        kaux_ref[h] = kaux.astype(BF16)


def _fox_prep(f_t, bf):
    B, H, T = f_t.shape
    aux = jax.ShapeDtypeStruct((B, H, AUX_ROWS, T), BF16)
    aux_spec = pl.BlockSpec((None, H, AUX_ROWS, T), lambda b: (b, 0, 0, 0))
    return pl.pallas_call(
        functools.partial(_fox_prep_body, chunk=512),
        grid=(B,),
        in_specs=[pl.BlockSpec((None, H, T), lambda b: (b, 0, 0)), _const_spec((H, 1))],
        out_specs=[aux_spec, aux_spec],
        out_shape=[aux, aux],
        scratch_shapes=[pltpu.VMEM((H, T), F32)],
        compiler_params=pltpu.CompilerParams(dimension_semantics=("parallel",)),
        name="fox_prep",
    )(f_t, bf)


MASK_BLOCK = 256


def _fox_attn_body(q_ref, k_ref, v_ref, qaux_ref, kaux_ref, o_ref, sb0_ref, sb1_ref, p_ref, *, tq):
    T = q_ref.shape[-1]
    dh = FOX_HEAD_DIM
    mb = MASK_BLOCK
    nj = tq // mb
    kk = lax.broadcasted_iota(jnp.int32, (mb, mb), 0)
    qq = lax.broadcasted_iota(jnp.int32, (mb, mb), 1)
    causal = kk <= qq
    ones = jnp.where(lax.broadcasted_iota(jnp.int32, (AUX_ROWS, tq), 0) == 0, 1.0, 0.0).astype(BF16)
    sbufs = (sb0_ref, sb1_ref)

    tasks = []
    for qi in range(T // tq):
        tasks.append((qi, qi, True))
        tasks += [(qi, c, False) for c in range(qi)]

    def scores(task, j, sb, mc):
        qi, c, diag = task
        c0 = j * mb if diag else 0
        r0 = c * tq + j * mb
        ka = jnp.concatenate([k_ref[:, r0:r0 + mb], kaux_ref[:, r0:r0 + mb]], axis=0)
        qa = jnp.concatenate([q_ref[:, qi * tq + c0:(qi + 1) * tq],
                              qaux_ref[:, qi * tq + c0:(qi + 1) * tq]], axis=0)
        s = _dot_tn(ka, qa)
        if diag:
            head = jnp.where(causal, s[:, :mb], -jnp.inf)
            s = head if tq - c0 == mb else jnp.concatenate([head, s[:, mb:]], axis=1)
        sb[j * mb:(j + 1) * mb, c0:] = s
        cm = jnp.max(s, axis=0, keepdims=True)
        if c0:
            cm = jnp.concatenate([jnp.full((1, c0), -jnp.inf, F32), cm], axis=1)
        return cm if mc is None else jnp.maximum(mc, cm)

    def probs(task, j, sb, m_new):
        _, _, diag = task
        c0 = j * mb if diag else 0
        rows = slice(j * mb, (j + 1) * mb)
        p_ref[rows, c0:] = jnp.exp2(sb[rows, c0:] - m_new[:, c0:]).astype(BF16)

    def values(task):
        _, c, diag = task
        va = jnp.concatenate([v_ref[:, c * tq:(c + 1) * tq], ones], axis=0)
        if not diag:
            return _dot(va, p_ref[...])
        return jnp.concatenate(
            [_dot(va[:, :(bj + 1) * mb], p_ref[:(bj + 1) * mb, bj * mb:(bj + 1) * mb])
             for bj in range(nj)], axis=1)

    mc = None
    for j in range(nj):
        mc = scores(tasks[0], j, sbufs[0], mc)
    m_run = acc = None
    for k, task in enumerate(tasks):
        qi, c, diag = task
        nxt = tasks[k + 1] if k + 1 < len(tasks) else None
        m_new = mc if diag else jnp.maximum(m_run, mc)
        mc = None
        for j in range(nj):
            if nxt is not None:
                mc = scores(nxt, j, sbufs[(k + 1) % 2], mc)
            probs(task, j, sbufs[k % 2], m_new)
        pv = values(task)
        acc = pv if diag else jnp.exp2(m_run - m_new) * acc + pv
        m_run = m_new
        if nxt is None or nxt[0] != qi:
            o_ref[:, qi * tq:(qi + 1) * tq] = (acc[:dh] / acc[dh:dh + 1]).astype(o_ref.dtype)


def _fox_attn(qkv_t, qaux, kaux, *, tq):
    B, _, T = qkv_t.shape
    H, dh = FOX_HEADS, FOX_HEAD_DIM
    part = lambda off: pl.BlockSpec((None, dh, T), lambda b, h: (b, off * H + h, 0))
    aux = pl.BlockSpec((None, None, AUX_ROWS, T), lambda b, h: (b, h, 0, 0))
    return pl.pallas_call(
        functools.partial(_fox_attn_body, tq=tq),
        grid=(B, H),
        in_specs=[part(0), part(1), part(2), aux, aux],
        out_specs=pl.BlockSpec((None, dh, T), lambda b, h: (b, h, 0)),
        out_shape=jax.ShapeDtypeStruct((B, H * dh, T), BF16),
        scratch_shapes=[pltpu.VMEM((tq, tq), F32), pltpu.VMEM((tq, tq), F32),
                        pltpu.VMEM((tq, tq), BF16)],
        compiler_params=pltpu.CompilerParams(
            dimension_semantics=("parallel", "parallel"), vmem_limit_bytes=VMEM_LIMIT),
        name="fox_attn",
    )(qkv_t, qkv_t, qkv_t, qaux, kaux)


HALO = 32
SUBLANES = 8


def _mix_out_body(x_ref, u_ref, uprev_ref, gc_ref, ga_ref, attn_t_ref, dww_ref, dwb_ref,
                  lng_ref, lnb_ref, pw_ref, abw_ref, wout_ref, o_ref, ubuf_ref, ushift_ref,
                  *, tc):
    first = pl.program_id(1) == 0
    ubuf_ref[:HALO, :] = jnp.where(first, 0.0, uprev_ref[...])
    ubuf_ref[HALO:, :] = u_ref[...]
    span = tc + HALO - SUBLANES
    for s in range(1, SUBLANES):
        ushift_ref[s - 1, :, :] = ubuf_ref[s:s + span, :]
    acc = jnp.zeros((tc, CONV_DIM), F32)
    for k in range(CONV_WIDTH):
        off = HALO - (CONV_WIDTH - 1) + k
        s, base = off % SUBLANES, off - off % SUBLANES
        rows = ubuf_ref[base:base + tc, :] if s == 0 else ushift_ref[s - 1, base:base + tc, :]
        acc = acc + rows * dww_ref[k:k + 1, :]
    acc = acc + dwb_ref[...]
    mu = jnp.mean(acc, axis=-1, keepdims=True)
    d = acc - mu
    var = jnp.mean(d * d, axis=-1, keepdims=True)
    y = d * lax.rsqrt(var + EPS) * lng_ref[...] + lnb_ref[...]
    y = (y * _sigmoid(y)).astype(BF16)
    conv_out = _dot(y, pw_ref[...])
    attn_out = _dot_tn(attn_t_ref[...], abw_ref[...])
    merged = (_sigmoid(gc_ref[...].astype(F32)) * conv_out
              + _sigmoid(ga_ref[...].astype(F32)) * attn_out)
    o_ref[...] = x_ref[...] + _dot(merged.astype(BF16), wout_ref[...])


def _mix_out(x, u, gc, ga, attn_t, dww, dwb, lng, lnb, pw, abw, wout, *, tc):
    B, T, D = x.shape
    row = lambda b, i: (b, i, 0)
    halo_blocks = tc // HALO
    return pl.pallas_call(
        functools.partial(_mix_out_body, tc=tc),
        grid=(B, T // tc),
        in_specs=[
            pl.BlockSpec((None, tc, D), row),
            pl.BlockSpec((None, tc, CONV_DIM), row),
            pl.BlockSpec((None, HALO, CONV_DIM),
                         lambda b, i: (b, jnp.maximum(i * halo_blocks - 1, 0), 0)),
            pl.BlockSpec((None, tc, D), row),
            pl.BlockSpec((None, tc, D), row),
            pl.BlockSpec((None, FOX_DIM, tc), lambda b, i: (b, 0, i)),
            _const_spec(dww.shape), _const_spec(dwb.shape), _const_spec(lng.shape),
            _const_spec(lnb.shape), _const_spec(pw.shape), _const_spec(abw.shape),
            _const_spec(wout.shape),
        ],
        out_specs=pl.BlockSpec((None, tc, D), row),
        out_shape=jax.ShapeDtypeStruct((B, T, D), F32),
        scratch_shapes=[pltpu.VMEM((HALO + tc, CONV_DIM), F32),
                        pltpu.VMEM((SUBLANES - 1, HALO + tc - SUBLANES, CONV_DIM), F32)],
        compiler_params=pltpu.CompilerParams(
            dimension_semantics=("parallel", "parallel"), vmem_limit_bytes=VMEM_LIMIT),
        name="mix_out",
    )(x, u, u, gc, ga, attn_t, dww, dwb, lng, lnb, pw, abw, wout)


def _mem_kv_body(m_ref, g_ref, wk_ref, wv_ref, k_ref, v_ref):
    h = _rms(m_ref[...], g_ref[...]).astype(BF16)
    k_ref[...] = _dot(h, wk_ref[...]).astype(BF16)
    v_ref[...] = _dot(h, wv_ref[...]).astype(BF16)


def _mem_kv(mem, g, wk, wv):
    B, M, D = mem.shape
    spec = pl.BlockSpec((None, M, XA_DIM), lambda b: (b, 0, 0))
    out = jax.ShapeDtypeStruct((B, M, XA_DIM), BF16)
    return pl.pallas_call(
        _mem_kv_body,
        grid=(B,),
        in_specs=[pl.BlockSpec((None, M, D), lambda b: (b, 0, 0)), _const_spec((1, D)),
                  _const_spec(wk.shape), _const_spec(wv.shape)],
        out_specs=[spec, spec],
        out_shape=[out, out],
        compiler_params=pltpu.CompilerParams(dimension_semantics=("parallel",)),
        name="mem_kv",
    )(mem, g, wk, wv)


ROUTE_LANES = 128
META_LANES = 128
XA_WIDTH = D_MODEL + META_LANES
LANE_GROUP = EXPERTS_PER_GROUP
LANE_RANK = EXPERTS_PER_GROUP + 1
LANE_CODE = EXPERTS_PER_GROUP + 2
CODE_ROWS = 8


def _first_max_lane(v, valid, lane):
    neg = jnp.where(valid, v, -jnp.inf)
    mx = jnp.max(neg, axis=-1, keepdims=True)
    idx = jnp.min(jnp.where(valid & (neg == mx), lane, ROUTE_LANES), axis=-1, keepdims=True)
    return mx, idx


def _route(logits, run):
    n = logits.shape[0]
    lane = lax.broadcasted_iota(jnp.int32, logits.shape, 1)
    is_g = lane < N_GROUPS
    gmax, g_star = _first_max_lane(logits, is_g, lane)
    p_gsel = 1.0 / jnp.sum(jnp.where(is_g, jnp.exp(logits - gmax), 0.0), axis=-1, keepdims=True)
    e_lane0 = N_GROUPS + g_star * EXPERTS_PER_GROUP
    in_grp = (lane >= e_lane0) & (lane < e_lane0 + EXPERTS_PER_GROUP)
    emax, _ = _first_max_lane(logits, in_grp, lane)
    ex = jnp.where(in_grp, jnp.exp(logits - emax), 0.0)
    p_exp = ex / jnp.sum(ex, axis=-1, keepdims=True)
    v1, i1 = _first_max_lane(p_exp, in_grp, lane)
    v2, i2 = _first_max_lane(p_exp, in_grp & (lane != i1), lane)
    tot = v1 + v2
    w = jnp.where(lane == i1 - e_lane0, p_gsel * (v1 / tot),
                  jnp.where(lane == i2 - e_lane0, p_gsel * (v2 / tot), 0.0))
    onehot = jnp.where(lane == g_star, 1.0, 0.0).astype(BF16)
    r_i = lax.broadcasted_iota(jnp.int32, (n, n), 0)
    c_i = lax.broadcasted_iota(jnp.int32, (n, n), 1)
    cum = _dot(jnp.where(r_i >= c_i, 1.0, 0.0).astype(BF16), onehot) + run
    rank = jnp.sum(jnp.where(lane == g_star, cum, 0.0), axis=-1, keepdims=True) - 1.0
    meta = (w + jnp.where(lane == LANE_GROUP, g_star.astype(F32), 0.0)
            + jnp.where(lane == LANE_RANK, rank, 0.0)
            + jnp.where(lane == LANE_CODE, rank * N_GROUPS + g_star.astype(F32), 0.0))
    return meta, cum[n - 1:n, :]


def _xattn_body(x_ref, g_ref, wq_ref, k_ref, v_ref, wo_ref, gm_ref, wr_ref, br_ref,
                xa_ref, cnt_ref, code_ref, run_ref):
    @pl.when((pl.program_id(0) == 0) & (pl.program_id(1) == 0))
    def _():
        run_ref[...] = jnp.zeros_like(run_ref)

    x = x_ref[...]
    q = _dot(_rms(x, g_ref[...]).astype(BF16), wq_ref[...]).astype(BF16)
    scale = XA_HEAD_DIM ** -0.5
    heads = []
    for h in range(XA_HEADS):
        sl = slice(h * XA_HEAD_DIM, (h + 1) * XA_HEAD_DIM)
        s = _dot_nt(q[:, sl], k_ref[:, sl]) * scale
        p = jnp.exp(s - jnp.max(s, axis=-1, keepdims=True))
        p = p / jnp.sum(p, axis=-1, keepdims=True)
        heads.append(_dot(p.astype(BF16), v_ref[:, sl]).astype(BF16))
    o = jnp.concatenate(heads, axis=-1)
    x2 = x + _dot(o, wo_ref[...])
    hm = _rms(x2, gm_ref[...]).astype(BF16)
    meta, run = _route(_dot(hm, wr_ref[...]) + br_ref[...], run_ref[...])
    run_ref[...] = run
    cnt_ref[...] = run.astype(jnp.int32)
    code_ref[...] = meta.T[:CODE_ROWS, :].astype(jnp.int32)
    xa_ref[:, :D_MODEL] = x2
    xa_ref[:, D_MODEL:] = meta


def _xattn_route(x, g, wq, kx, vx, wo, gm, wr, br, *, tx):
    B, T, D = x.shape
    row = lambda b, i: (b, i, 0)
    kv = pl.BlockSpec((None, MEM_LEN, XA_DIM), lambda b, i: (b, 0, 0))
    return pl.pallas_call(
        _xattn_body,
        grid=(B, T // tx),
        in_specs=[pl.BlockSpec((None, tx, D), row), _const_spec((1, D)),
                  _const_spec(wq.shape), kv, kv, _const_spec(wo.shape), _const_spec((1, D)),
                  _const_spec(wr.shape), _const_spec(br.shape)],
        out_specs=[pl.BlockSpec((None, tx, XA_WIDTH), row), _const_spec((1, ROUTE_LANES)),
                   pl.BlockSpec((None, None, CODE_ROWS, tx), lambda b, i: (b, i, 0, 0))],
        out_shape=[jax.ShapeDtypeStruct((B, T, XA_WIDTH), F32),
                   jax.ShapeDtypeStruct((1, ROUTE_LANES), jnp.int32),
                   jax.ShapeDtypeStruct((B, T // tx, CODE_ROWS, tx), jnp.int32)],
        scratch_shapes=[pltpu.VMEM((1, ROUTE_LANES), F32)],
        compiler_params=pltpu.CompilerParams(
            dimension_semantics=("arbitrary", "arbitrary"), vmem_limit_bytes=VMEM_LIMIT),
        name="xattn_route",
    )(x, g, wq, kx, vx, wo, gm, wr, br)


VISIT_FIRST, VISIT_LAST, VISIT_VALID = 1, 2, 4


def _moe_plan_body(code_ref, cnt_ref, inv_ref, vt_ref, vg_ref, vf_ref, offs_ref, *, ts):
    n = code_ref.shape[0]
    nt, nv = n // ts, vt_ref.shape[0]
    ends = [cnt_ref[0, 0]]
    for g in range(1, N_GROUPS - 1):
        ends.append(ends[-1] + cnt_ref[0, g])
    offs_ref[0] = 0
    for g in range(1, N_GROUPS):
        offs_ref[g] = ends[g - 1]
    group_bits = N_GROUPS.bit_length() - 1
    assert 1 << group_bits == N_GROUPS

    def place(t, _):
        code = code_ref[t]
        slot = offs_ref[code & (N_GROUPS - 1)] + lax.shift_right_logical(code, group_bits)
        inv_ref[slot] = t
        return 0

    lax.fori_loop(0, n, place, 0, unroll=8)

    def group_of(slot):
        return sum(jnp.where(e <= slot, 1, 0) for e in ends)

    def tile(j, v):
        g_lo, g_hi = group_of(j * ts), group_of(j * ts + ts - 1)
        for k in range(N_GROUPS):
            @pl.when(k <= g_hi - g_lo)
            def _():
                vt_ref[v + k] = j
                vg_ref[v + k] = g_lo + k
                vf_ref[v + k] = (VISIT_VALID + (VISIT_FIRST if k == 0 else 0)
                                 + jnp.where(k == g_hi - g_lo, VISIT_LAST, 0))
        return v + g_hi - g_lo + 1

    used = lax.fori_loop(0, nt, tile, 0)
    g_last = group_of(n - 1)

    def pad(v, _):
        vt_ref[v] = nt - 1
        vg_ref[v] = g_last
        vf_ref[v] = 0
        return 0

    lax.fori_loop(used, nv, pad, 0)


def _moe_plan(code, cnt, *, ts):
    n = code.shape[0]
    nt = n // ts
    nv = nt + N_GROUPS - 1
    smem = pl.BlockSpec(memory_space=pltpu.SMEM)
    visits = jax.ShapeDtypeStruct((nv,), jnp.int32)
    inv, vt, vg, vf = pl.pallas_call(
        functools.partial(_moe_plan_body, ts=ts),
        in_specs=[smem, smem],
        out_specs=[smem, smem, smem, smem],
        out_shape=[jax.ShapeDtypeStruct((n,), jnp.int32), visits, visits, visits],
        scratch_shapes=[pltpu.SMEM((N_GROUPS,), jnp.int32)],
        name="moe_plan",
    )(code, cnt)
    return inv.reshape(nt, 1, ts), vt, vg, vf


def _moe_body(vt_ref, vg_ref, vf_ref, inv_cur_ref, inv_nxt_ref, inv_prv_ref, xa_hbm, gm_ref,
              wg_ref, wu_ref, wd_ref, gf_ref, out_hbm, xbuf, obuf, hm_ref, yacc_ref, gsem, ssem,
              *, ts, nt, sub):
    v = pl.program_id(0)
    t, grp, flags = vt_ref[v], vg_ref[v], vf_ref[v]
    slot = lax.rem(t, 2)
    first = (flags & VISIT_FIRST) != 0
    overlapped = first & (t >= 1)

    def gather_rows(idx_ref, dst_slot, lo, hi):
        for r in range(lo, min(hi, ts)):
            pltpu.make_async_copy(xa_hbm.at[pl.ds(idx_ref[0, r], 1)],
                                  xbuf.at[dst_slot, pl.ds(r, 1)], gsem.at[dst_slot]).start()

    def gather_wait(dst_slot):
        pltpu.make_async_copy(xa_hbm.at[pl.ds(0, ts)], xbuf.at[dst_slot], gsem.at[dst_slot]).wait()

    def scatter_rows(idx_ref, src_slot, lo, hi):
        for r in range(lo, min(hi, ts)):
            pltpu.make_async_copy(obuf.at[src_slot, pl.ds(r, 1)],
                                  out_hbm.at[pl.ds(idx_ref[0, r], 1)], ssem.at[src_slot]).start()

    def scatter_wait(src_slot):
        pltpu.make_async_copy(obuf.at[src_slot], out_hbm.at[pl.ds(0, ts)], ssem.at[src_slot]).wait()

    def experts(issue_dma):
        lane = lax.broadcasted_iota(jnp.int32, (sub, META_LANES), 1)
        n_sub = ts // sub
        dots_per_sub = 3 * EXPERTS_PER_GROUP
        batch = -(-ts // (dots_per_sub * max(n_sub // 2, 1)))
        done = [0, 0]

        def after_dot(r):
            if not issue_dma:
                return
            kind = 0 if r < max(n_sub // 2, 1) else 1
            lo = done[kind]
            if kind == 0:
                gather_rows(inv_nxt_ref, 1 - slot, lo, lo + batch)
            else:
                scatter_rows(inv_prv_ref, 1 - slot, lo, lo + batch)
            done[kind] = min(lo + batch, ts)

        for r in range(n_sub):
            rows = slice(r * sub, (r + 1) * sub)
            meta = xbuf[slot, rows, D_MODEL:]
            row_grp = jnp.sum(jnp.where(lane == LANE_GROUP, meta, 0.0), axis=-1, keepdims=True)
            member = row_grp == grp.astype(F32)
            hm = hm_ref[rows, :]
            y = jnp.zeros((sub, D_MODEL), F32)
            for e in range(EXPERTS_PER_GROUP):
                w = jnp.sum(jnp.where(lane == e, meta, 0.0), axis=-1, keepdims=True)
                w = jnp.where(member, w, 0.0)
                a = _dot(hm, wg_ref[e])
                after_dot(r)
                b = _dot(hm, wu_ref[e])
                after_dot(r)
                u = (a * _sigmoid(a)) * b * w
                y = y + _dot(u.astype(BF16), wd_ref[e])
                after_dot(r)
            yacc_ref[rows, :] += y
        if issue_dma:
            assert done == [ts, ts], done

    @pl.when(v == 0)
    def _():
        gather_rows(inv_cur_ref, 0, 0, ts)

    @pl.when(first)
    def _():
        gather_wait(slot)

        @pl.when(t == 0)
        def _():
            gather_rows(inv_nxt_ref, 1, 0, ts)

        hm_ref[...] = _rms(xbuf[slot, :, :D_MODEL], gm_ref[...]).astype(BF16)
        yacc_ref[...] = jnp.zeros_like(yacc_ref)

    @pl.when(overlapped)
    def _():
        experts(True)

    @pl.when(((flags & VISIT_VALID) != 0) & jnp.logical_not(overlapped))
    def _():
        experts(False)

    @pl.when((flags & VISIT_LAST) != 0)
    def _():
        @pl.when(t >= 2)
        def _():
            scatter_wait(slot)

        obuf[slot] = _rms(xbuf[slot, :, :D_MODEL] + yacc_ref[...], gf_ref[...])

    @pl.when(v == pl.num_programs(0) - 1)
    def _():
        last_slot = (nt - 1) % 2
        scatter_rows(inv_cur_ref, last_slot, 0, ts)
        scatter_wait(0)
        scatter_wait(1)
        gather_wait(1 - last_slot)


def _moe_sorted(xa, inv, vt, vg, vf, gm, wg, wu, wd, gf, *, ts, sub):
    N = xa.shape[0]
    nt = N // ts
    assert nt >= 2
    E, D, FF = EXPERTS_PER_GROUP, D_MODEL, EXPERT_FF
    grp = lambda v, vt, vg, vf: (vg[v], 0, 0)
    const = lambda v, vt, vg, vf: (0, 0)
    smem_row = lambda fn: pl.BlockSpec((None, 1, ts), fn, memory_space=pltpu.SMEM)
    grid_spec = pltpu.PrefetchScalarGridSpec(
        num_scalar_prefetch=3,
        grid=(vt.shape[0],),
        in_specs=[
            smem_row(lambda v, vt, vg, vf: (vt[v], 0, 0)),
            smem_row(lambda v, vt, vg, vf: (jnp.minimum(vt[v] + 1, nt - 1), 0, 0)),
            smem_row(lambda v, vt, vg, vf: (jnp.maximum(vt[v] - 1, 0), 0, 0)),
            pl.BlockSpec(memory_space=pl.ANY),
            pl.BlockSpec((1, D), const),
            pl.BlockSpec((E, D, FF), grp), pl.BlockSpec((E, D, FF), grp),
            pl.BlockSpec((E, FF, D), grp),
            pl.BlockSpec((1, D), const),
        ],
        out_specs=pl.BlockSpec(memory_space=pl.ANY),
        scratch_shapes=[
            pltpu.VMEM((2, ts, XA_WIDTH), F32), pltpu.VMEM((2, ts, D), F32),
            pltpu.VMEM((ts, D), BF16), pltpu.VMEM((ts, D), F32),
            pltpu.SemaphoreType.DMA((2,)), pltpu.SemaphoreType.DMA((2,)),
        ],
    )
    return pl.pallas_call(
        functools.partial(_moe_body, ts=ts, nt=nt, sub=sub),
        grid_spec=grid_spec,
        out_shape=jax.ShapeDtypeStruct((N, D), F32),
        compiler_params=pltpu.CompilerParams(
            dimension_semantics=("arbitrary",), vmem_limit_bytes=VMEM_LIMIT),
        name="moe_sorted",
    )(vt, vg, vf, inv, inv, inv, xa, gm, wg, wu, wd, gf)


def kernel(x, mem, norm_mix_g, w_in, fox_bf, conv_dw_w, conv_dw_b, conv_ln_g, conv_ln_b, conv_pw_w, attn_branch_w, w_out, norm_xa_g, norm_mem_g, xa_wq, xa_wk, xa_wv, xa_wo, norm_moe_g, router_group_w, router_group_b, router_expert_w, router_expert_b, expert_w_gate, expert_w_up, expert_w_down, norm_final_g):
    B, T, D = x.shape
    depth = w_in.shape[0]
    assert depth == 1, "the final norm is fused into the last layer's MoE kernel"
    row = lambda v: v.reshape(1, -1).astype(F32)
    c_glu = 2 * CONV_DIM
    c_q, c_f = c_glu, c_glu + 3 * FOX_DIM
    c_gc = c_f + FOX_HEADS
    c_ga = c_gc + D
    for l in range(depth):
        w = w_in[l]
        u, gc, ga, qkv_t, f_t = _proj_in(
            x, row(norm_mix_g[l]),
            w[:, :c_glu].astype(BF16), w[:, c_gc:c_ga].astype(BF16), w[:, c_ga:].astype(BF16),
            w[:, c_q:c_f].T.astype(BF16), w[:, c_f:c_gc].T.astype(BF16), tm=512)
        qaux, kaux = _fox_prep(f_t, fox_bf[l].reshape(FOX_HEADS, 1).astype(F32))
        attn_t = _fox_attn(qkv_t, qaux, kaux, tq=1024)
        x = _mix_out(x, u, gc, ga, attn_t, conv_dw_w[l].astype(F32), row(conv_dw_b[l]),
                     row(conv_ln_g[l]), row(conv_ln_b[l]), conv_pw_w[l].astype(BF16),
                     attn_branch_w[l].astype(BF16), w_out[l].astype(BF16), tc=512)
        kx, vx = _mem_kv(mem, row(norm_mem_g[l]), xa_wk[l].astype(BF16), xa_wv[l].astype(BF16))
        wr = jnp.zeros((D, ROUTE_LANES), F32)
        wr = wr.at[:, :N_GROUPS].set(router_group_w[l]).at[:, N_GROUPS:N_GROUPS + N_EXPERTS].set(
            router_expert_w[l])
        br = jnp.zeros((1, ROUTE_LANES), F32)
        br = br.at[0, :N_GROUPS].set(router_group_b[l]).at[0, N_GROUPS:N_GROUPS + N_EXPERTS].set(
            router_expert_b[l])
        xa, cnt, code = _xattn_route(x, row(norm_xa_g[l]), xa_wq[l].astype(BF16), kx, vx,
                                     xa_wo[l].astype(BF16), row(norm_moe_g[l]), wr.astype(BF16),
                                     br, tx=512)
        xa = xa.reshape(B * T, XA_WIDTH)
        inv, vt, vg, vf = _moe_plan(code[:, :, LANE_CODE, :].reshape(B * T), cnt, ts=512)
        x = _moe_sorted(xa, inv, vt, vg, vf, row(norm_moe_g[l]), expert_w_gate[l].astype(BF16),
                        expert_w_up[l].astype(BF16), expert_w_down[l].astype(BF16),
                        row(norm_final_g), ts=512, sub=256).reshape(B, T, D)
    return x
```

```python
import functools

import jax
import jax.numpy as jnp
from jax import lax
from jax.experimental import pallas as pl
from jax.experimental.pallas import tpu as pltpu

D_MODEL = 1024
MEM_LEN = 256
EPS = 1e-6
CONV_DIM = 512
CONV_WIDTH = 31
FOX_HEADS = 8
FOX_HEAD_DIM = 64
FOX_DIM = FOX_HEADS * FOX_HEAD_DIM
XA_HEADS = 4
XA_HEAD_DIM = 128
XA_DIM = XA_HEADS * XA_HEAD_DIM
N_GROUPS = 4
EXPERTS_PER_GROUP = 4
N_EXPERTS = N_GROUPS * EXPERTS_PER_GROUP
EXPERT_FF = 256

AUX_ROWS = 16
LOG2E = 1.4426950408889634
VMEM_LIMIT = 56 * 1024 * 1024

F32 = jnp.float32
BF16 = jnp.bfloat16


def _const_spec(shape):
    n = len(shape)
    return pl.BlockSpec(shape, lambda *_: (0,) * n)


def _rms(x, g):
    return x * lax.rsqrt(jnp.mean(x * x, axis=-1, keepdims=True) + EPS) * g


def _sigmoid(x):
    return 1.0 / (1.0 + jnp.exp(-x))


def _dot(a, b):
    return jnp.dot(a, b, preferred_element_type=F32)


def _dot_nt(a, b):
    return lax.dot_general(a, b, (((1,), (1,)), ((), ())), preferred_element_type=F32)


def _dot_tn(a, b):
    return lax.dot_general(a, b, (((0,), (0,)), ((), ())), preferred_element_type=F32)


def _proj_in_body(x_ref, g_ref, wglu_ref, wgc_ref, wga_ref, wqkv_t_ref, wf_t_ref,
                  u_ref, gc_ref, ga_ref, qkv_t_ref, f_t_ref):
    h = _rms(x_ref[...], g_ref[...]).astype(BF16)
    glu = _dot(h, wglu_ref[...])
    u_ref[...] = glu[:, :CONV_DIM] * _sigmoid(glu[:, CONV_DIM:])
    gc_ref[...] = _dot(h, wgc_ref[...]).astype(BF16)
    ga_ref[...] = _dot(h, wga_ref[...]).astype(BF16)
    qkv_t = _dot_nt(wqkv_t_ref[...], h)
    scale = FOX_HEAD_DIM ** -0.5 * LOG2E
    qkv_t_ref[:FOX_DIM, :] = (qkv_t[:FOX_DIM] * scale).astype(BF16)
    qkv_t_ref[FOX_DIM:, :] = qkv_t[FOX_DIM:].astype(BF16)
    f_t_ref[...] = _dot_nt(wf_t_ref[...], h)


def _proj_in(x, g, wglu, wgc, wga, wqkv_t, wf_t, *, tm):
    B, T, D = x.shape
    grid = (B, T // tm)
    row = lambda b, i: (b, i, 0)
    col = lambda b, i: (b, 0, i)
    return pl.pallas_call(
        _proj_in_body,
        grid=grid,
        in_specs=[
            pl.BlockSpec((None, tm, D), row),
            _const_spec((1, D)),
            _const_spec(wglu.shape), _const_spec(wgc.shape), _const_spec(wga.shape),
            _const_spec(wqkv_t.shape), _const_spec(wf_t.shape),
        ],
        out_specs=[
            pl.BlockSpec((None, tm, CONV_DIM), row),
            pl.BlockSpec((None, tm, D), row),
            pl.BlockSpec((None, tm, D), row),
            pl.BlockSpec((None, 3 * FOX_DIM, tm), col),
            pl.BlockSpec((None, FOX_HEADS, tm), col),
        ],
        out_shape=[
            jax.ShapeDtypeStruct((B, T, CONV_DIM), F32),
            jax.ShapeDtypeStruct((B, T, D), BF16),
            jax.ShapeDtypeStruct((B, T, D), BF16),
            jax.ShapeDtypeStruct((B, 3 * FOX_DIM, T), BF16),
            jax.ShapeDtypeStruct((B, FOX_HEADS, T), F32),
        ],
        compiler_params=pltpu.CompilerParams(
            dimension_semantics=("parallel", "parallel"), vmem_limit_bytes=VMEM_LIMIT),
        name="proj_in",
    )(x, g, wglu, wgc, wga, wqkv_t, wf_t)


def _split3(c):
    hi = c.astype(BF16)
    r = c - hi.astype(F32)
    mid = r.astype(BF16)
    lo = (r - mid.astype(F32)).astype(BF16)
    return hi, mid, lo


def _fox_prep_body(f_t_ref, bf_ref, qaux_ref, kaux_ref, c_ref, *, chunk):
    T = f_t_ref.shape[-1]
    z = f_t_ref[...] + bf_ref[...]
    logf = jnp.minimum(z, 0.0) - jnp.log1p(jnp.exp(-jnp.abs(z)))
    r_i = lax.broadcasted_iota(jnp.int32, (chunk, chunk), 0)
    c_i = lax.broadcasted_iota(jnp.int32, (chunk, chunk), 1)
    tri = jnp.where(r_i <= c_i, 1.0, 0.0).astype(BF16)
    carry = jnp.zeros((FOX_HEADS, 1), F32)
    for j in range(T // chunk):
        blk = logf[:, j * chunk:(j + 1) * chunk]
        hi, mid, lo = _split3(blk)
        cs = (_dot(lo, tri) + _dot(mid, tri)) + _dot(hi, tri) + carry
        c_ref[:, j * chunk:(j + 1) * chunk] = cs
        carry = cs[:, chunk - 1:chunk]
    row = lax.broadcasted_iota(jnp.int32, (AUX_ROWS, T), 0)
    for h in range(FOX_HEADS):
        hi, mid, lo = _split3(c_ref[h:h + 1, :] * LOG2E)
        pieces = [jnp.broadcast_to(p.astype(F32), (AUX_ROWS, T)) for p in (hi, mid, lo)]
        one = jnp.where(row < 3, 1.0, 0.0)
        val = jnp.where(row == 0, pieces[0], jnp.where(row == 1, pieces[1], pieces[2]))
        val_k = jnp.where(row == 3, pieces[0], jnp.where(row == 4, pieces[1], pieces[2]))
        qaux = jnp.where(row < 3, val, jnp.where(row < 6, 1.0, 0.0))
        kaux = jnp.where(row < 3, one, jnp.where(row < 6, -val_k, 0.0))
        qaux_ref[h] = qaux.astype(BF16)
        kaux_ref[h] = kaux.astype(BF16)


def _fox_prep(f_t, bf):
    B, H, T = f_t.shape
    aux = jax.ShapeDtypeStruct((B, H, AUX_ROWS, T), BF16)
    aux_spec = pl.BlockSpec((None, H, AUX_ROWS, T), lambda b: (b, 0, 0, 0))
    return pl.pallas_call(
        functools.partial(_fox_prep_body, chunk=512),
        grid=(B,),
        in_specs=[pl.BlockSpec((None, H, T), lambda b: (b, 0, 0)), _const_spec((H, 1))],
        out_specs=[aux_spec, aux_spec],
        out_shape=[aux, aux],
        scratch_shapes=[pltpu.VMEM((H, T), F32)],
        compiler_params=pltpu.CompilerParams(dimension_semantics=("parallel",)),
        name="fox_prep",
    )(f_t, bf)


MASK_BLOCK = 256


def _fox_attn_body(q_ref, k_ref, v_ref, qaux_ref, kaux_ref, o_ref, sb0_ref, sb1_ref, p_ref, *, tq):
    T = q_ref.shape[-1]
    dh = FOX_HEAD_DIM
    mb = MASK_BLOCK
    nj = tq // mb
    kk = lax.broadcasted_iota(jnp.int32, (mb, mb), 0)
    qq = lax.broadcasted_iota(jnp.int32, (mb, mb), 1)
    causal = kk <= qq
    ones = jnp.where(lax.broadcasted_iota(jnp.int32, (AUX_ROWS, tq), 0) == 0, 1.0, 0.0).astype(BF16)
    sbufs = (sb0_ref, sb1_ref)

    tasks = []
    for qi in range(T // tq):
        tasks.append((qi, qi, True))
        tasks += [(qi, c, False) for c in range(qi)]

    def scores(task, j, sb, mc):
        qi, c, diag = task
        c0 = j * mb if diag else 0
        r0 = c * tq + j * mb
        ka = jnp.concatenate([k_ref[:, r0:r0 + mb], kaux_ref[:, r0:r0 + mb]], axis=0)
        qa = jnp.concatenate([q_ref[:, qi * tq + c0:(qi + 1) * tq],
                              qaux_ref[:, qi * tq + c0:(qi + 1) * tq]], axis=0)
        s = _dot_tn(ka, qa)
        if diag:
            head = jnp.where(causal, s[:, :mb], -jnp.inf)
            s = head if tq - c0 == mb else jnp.concatenate([head, s[:, mb:]], axis=1)
        sb[j * mb:(j + 1) * mb, c0:] = s
        cm = jnp.max(s, axis=0, keepdims=True)
        if c0:
            cm = jnp.concatenate([jnp.full((1, c0), -jnp.inf, F32), cm], axis=1)
        return cm if mc is None else jnp.maximum(mc, cm)

    def probs(task, j, sb, m_new):
        _, _, diag = task
        c0 = j * mb if diag else 0
        rows = slice(j * mb, (j + 1) * mb)
        p_ref[rows, c0:] = jnp.exp2(sb[rows, c0:] - m_new[:, c0:]).astype(BF16)

    def values(task):
        _, c, diag = task
        va = jnp.concatenate([v_ref[:, c * tq:(c + 1) * tq], ones], axis=0)
        if not diag:
            return _dot(va, p_ref[...])
        return jnp.concatenate(
            [_dot(va[:, :(bj + 1) * mb], p_ref[:(bj + 1) * mb, bj * mb:(bj + 1) * mb])
             for bj in range(nj)], axis=1)

    mc = None
    for j in range(nj):
        mc = scores(tasks[0], j, sbufs[0], mc)
    m_run = acc = None
    for k, task in enumerate(tasks):
        qi, c, diag = task
        nxt = tasks[k + 1] if k + 1 < len(tasks) else None
        m_new = mc if diag else jnp.maximum(m_run, mc)
        mc = None
        for j in range(nj):
            if nxt is not None:
                mc = scores(nxt, j, sbufs[(k + 1) % 2], mc)
            probs(task, j, sbufs[k % 2], m_new)
        pv = values(task)
        acc = pv if diag else jnp.exp2(m_run - m_new) * acc + pv
        m_run = m_new
        if nxt is None or nxt[0] != qi:
            o_ref[:, qi * tq:(qi + 1) * tq] = (acc[:dh] / acc[dh:dh + 1]).astype(o_ref.dtype)


def _fox_attn(qkv_t, qaux, kaux, *, tq):
    B, _, T = qkv_t.shape
    H, dh = FOX_HEADS, FOX_HEAD_DIM
    part = lambda off: pl.BlockSpec((None, dh, T), lambda b, h: (b, off * H + h, 0))
    aux = pl.BlockSpec((None, None, AUX_ROWS, T), lambda b, h: (b, h, 0, 0))
    return pl.pallas_call(
        functools.partial(_fox_attn_body, tq=tq),
        grid=(B, H),
        in_specs=[part(0), part(1), part(2), aux, aux],
        out_specs=pl.BlockSpec((None, dh, T), lambda b, h: (b, h, 0)),
        out_shape=jax.ShapeDtypeStruct((B, H * dh, T), BF16),
        scratch_shapes=[pltpu.VMEM((tq, tq), F32), pltpu.VMEM((tq, tq), F32),
                        pltpu.VMEM((tq, tq), BF16)],
        compiler_params=pltpu.CompilerParams(
            dimension_semantics=("parallel", "parallel"), vmem_limit_bytes=VMEM_LIMIT),
        name="fox_attn",
    )(qkv_t, qkv_t, qkv_t, qaux, kaux)


HALO = 32
SUBLANES = 8


def _mix_out_body(x_ref, u_ref, uprev_ref, gc_ref, ga_ref, attn_t_ref, dww_ref, dwb_ref,
                  lng_ref, lnb_ref, pw_ref, abw_ref, wout_ref, o_ref, ubuf_ref, ushift_ref,
                  *, tc):
    first = pl.program_id(1) == 0
    ubuf_ref[:HALO, :] = jnp.where(first, 0.0, uprev_ref[...])
    ubuf_ref[HALO:, :] = u_ref[...]
    span = tc + HALO - SUBLANES
    for s in range(1, SUBLANES):
        ushift_ref[s - 1, :, :] = ubuf_ref[s:s + span, :]
    acc = jnp.zeros((tc, CONV_DIM), F32)
    for k in range(CONV_WIDTH):
        off = HALO - (CONV_WIDTH - 1) + k
        s, base = off % SUBLANES, off - off % SUBLANES
        rows = ubuf_ref[base:base + tc, :] if s == 0 else ushift_ref[s - 1, base:base + tc, :]
        acc = acc + rows * dww_ref[k:k + 1, :]
    acc = acc + dwb_ref[...]
    mu = jnp.mean(acc, axis=-1, keepdims=True)
    d = acc - mu
    var = jnp.mean(d * d, axis=-1, keepdims=True)
    y = d * lax.rsqrt(var + EPS) * lng_ref[...] + lnb_ref[...]
    y = (y * _sigmoid(y)).astype(BF16)
    conv_out = _dot(y, pw_ref[...])
    attn_out = _dot_tn(attn_t_ref[...], abw_ref[...])
    merged = (_sigmoid(gc_ref[...].astype(F32)) * conv_out
              + _sigmoid(ga_ref[...].astype(F32)) * attn_out)
    o_ref[...] = x_ref[...] + _dot(merged.astype(BF16), wout_ref[...])


def _mix_out(x, u, gc, ga, attn_t, dww, dwb, lng, lnb, pw, abw, wout, *, tc):
    B, T, D = x.shape
    row = lambda b, i: (b, i, 0)
    halo_blocks = tc // HALO
    return pl.pallas_call(
        functools.partial(_mix_out_body, tc=tc),
        grid=(B, T // tc),
        in_specs=[
            pl.BlockSpec((None, tc, D), row),
            pl.BlockSpec((None, tc, CONV_DIM), row),
            pl.BlockSpec((None, HALO, CONV_DIM),
                         lambda b, i: (b, jnp.maximum(i * halo_blocks - 1, 0), 0)),
            pl.BlockSpec((None, tc, D), row),
            pl.BlockSpec((None, tc, D), row),
            pl.BlockSpec((None, FOX_DIM, tc), lambda b, i: (b, 0, i)),
            _const_spec(dww.shape), _const_spec(dwb.shape), _const_spec(lng.shape),
            _const_spec(lnb.shape), _const_spec(pw.shape), _const_spec(abw.shape),
            _const_spec(wout.shape),
        ],
        out_specs=pl.BlockSpec((None, tc, D), row),
        out_shape=jax.ShapeDtypeStruct((B, T, D), F32),
        scratch_shapes=[pltpu.VMEM((HALO + tc, CONV_DIM), F32),
                        pltpu.VMEM((SUBLANES - 1, HALO + tc - SUBLANES, CONV_DIM), F32)],
        compiler_params=pltpu.CompilerParams(
            dimension_semantics=("parallel", "parallel"), vmem_limit_bytes=VMEM_LIMIT),
        name="mix_out",
    )(x, u, u, gc, ga, attn_t, dww, dwb, lng, lnb, pw, abw, wout)


def _mem_kv_body(m_ref, g_ref, wk_ref, wv_t_ref, k_ref, v_t_ref):
    h = _rms(m_ref[...], g_ref[...]).astype(BF16)
    k_ref[...] = _dot(h, wk_ref[...]).astype(BF16)
    v_t_ref[...] = _dot_nt(wv_t_ref[...], h).astype(BF16)


def _mem_kv(mem, g, wk, wv_t):
    B, M, D = mem.shape
    return pl.pallas_call(
        _mem_kv_body,
        grid=(B,),
        in_specs=[pl.BlockSpec((None, M, D), lambda b: (b, 0, 0)), _const_spec((1, D)),
                  _const_spec(wk.shape), _const_spec(wv_t.shape)],
        out_specs=[pl.BlockSpec((None, M, XA_DIM), lambda b: (b, 0, 0)),
                   pl.BlockSpec((None, XA_DIM, M), lambda b: (b, 0, 0))],
        out_shape=[jax.ShapeDtypeStruct((B, M, XA_DIM), BF16),
                   jax.ShapeDtypeStruct((B, XA_DIM, M), BF16)],
        compiler_params=pltpu.CompilerParams(dimension_semantics=("parallel",)),
        name="mem_kv",
    )(mem, g, wk, wv_t)


META_LANES = 128
XA_WIDTH = D_MODEL + META_LANES
LANE_GROUP = EXPERTS_PER_GROUP
LANE_RANK = EXPERTS_PER_GROUP + 1
LANE_CODE = EXPERTS_PER_GROUP + 2
CODE_ROWS = SUBLANES
ROUTE_ROWS = SUBLANES * (N_GROUPS + 1)


def _first_max_row(v, valid, row):
    neg = jnp.where(valid, v, -jnp.inf)
    mx = jnp.max(neg, axis=0, keepdims=True)
    idx = jnp.min(jnp.where(valid & (neg == mx), row, SUBLANES), axis=0, keepdims=True)
    return mx, idx


def _route(logits_t, run):
    n = logits_t.shape[1]
    row = lax.broadcasted_iota(jnp.int32, (SUBLANES, n), 0)
    valid = row < N_GROUPS
    glog = logits_t[:SUBLANES]
    gmax, g_star = _first_max_row(glog, valid, row)
    p_gsel = 1.0 / jnp.sum(jnp.where(valid, jnp.exp(glog - gmax), 0.0), axis=0, keepdims=True)
    elog = logits_t[SUBLANES:2 * SUBLANES]
    for g in range(1, N_GROUPS):
        elog = jnp.where(g_star == g, logits_t[(g + 1) * SUBLANES:(g + 2) * SUBLANES], elog)
    emax, _ = _first_max_row(elog, valid, row)
    ex = jnp.where(valid, jnp.exp(elog - emax), 0.0)
    p_exp = ex / jnp.sum(ex, axis=0, keepdims=True)
    v1, i1 = _first_max_row(p_exp, valid, row)
    v2, i2 = _first_max_row(p_exp, valid & (row != i1), row)
    tot = v1 + v2
    w = jnp.where(row == i1, p_gsel * (v1 / tot), jnp.where(row == i2, p_gsel * (v2 / tot), 0.0))
    onehot = jnp.where(row == g_star, 1.0, 0.0).astype(BF16)
    r_i = lax.broadcasted_iota(jnp.int32, (n, n), 0)
    c_i = lax.broadcasted_iota(jnp.int32, (n, n), 1)
    cum = _dot(onehot, jnp.where(r_i <= c_i, 1.0, 0.0).astype(BF16)) + run
    rank = jnp.sum(jnp.where(row == g_star, cum, 0.0), axis=0, keepdims=True) - 1.0
    g_f = g_star.astype(F32)
    meta_t = jnp.where(valid, w, jnp.where(row == LANE_GROUP, g_f, jnp.where(
        row == LANE_RANK, rank, jnp.where(row == LANE_CODE, rank * N_GROUPS + g_f, 0.0))))
    return meta_t, cum[:, n - 1:n]


def _xattn_body(x_ref, g_ref, wq_t_ref, k_ref, v_t_ref, wo_ref, gm_ref, wr_t_ref, br_t_ref,
                xa_ref, cnt_ref, code_ref, run_ref):
    @pl.when((pl.program_id(0) == 0) & (pl.program_id(1) == 0))
    def _():
        run_ref[...] = jnp.zeros_like(run_ref)

    x = x_ref[...]
    tx = x.shape[0]
    q_t = _dot_nt(wq_t_ref[...], _rms(x, g_ref[...]).astype(BF16)).astype(BF16)
    scale = XA_HEAD_DIM ** -0.5
    heads = []
    for h in range(XA_HEADS):
        sl = slice(h * XA_HEAD_DIM, (h + 1) * XA_HEAD_DIM)
        s_t = _dot(k_ref[:, sl], q_t[sl, :]) * scale
        p = jnp.exp(s_t - jnp.max(s_t, axis=0, keepdims=True))
        l = jnp.sum(p, axis=0, keepdims=True)
        heads.append((_dot(v_t_ref[sl, :], p.astype(BF16)) / l).astype(BF16))
    o_t = jnp.concatenate(heads, axis=0)
    x2 = x + _dot_tn(o_t, wo_ref[...])
    hm = _rms(x2, gm_ref[...]).astype(BF16)
    meta_t, run = _route(_dot_nt(wr_t_ref[...], hm) + br_t_ref[...], run_ref[...])
    run_ref[...] = run
    cnt_ref[...] = jnp.broadcast_to(run, cnt_ref.shape).astype(jnp.int32)
    code_ref[...] = meta_t.astype(jnp.int32)
    xa_ref[:, :D_MODEL] = x2
    xa_ref[:, D_MODEL:] = jnp.concatenate(
        [meta_t, jnp.zeros((META_LANES - CODE_ROWS, tx), F32)], axis=0).T


def _xattn_route(x, g, wq_t, kx, vx_t, wo, gm, wr_t, br_t, *, tx):
    B, T, D = x.shape
    row = lambda b, i: (b, i, 0)
    return pl.pallas_call(
        _xattn_body,
        grid=(B, T // tx),
        in_specs=[pl.BlockSpec((None, tx, D), row), _const_spec((1, D)),
                  _const_spec(wq_t.shape),
                  pl.BlockSpec((None, MEM_LEN, XA_DIM), lambda b, i: (b, 0, 0)),
                  pl.BlockSpec((None, XA_DIM, MEM_LEN), lambda b, i: (b, 0, 0)),
                  _const_spec(wo.shape), _const_spec((1, D)),
                  _const_spec(wr_t.shape), _const_spec(br_t.shape)],
        out_specs=[pl.BlockSpec((None, tx, XA_WIDTH), row), _const_spec((CODE_ROWS, META_LANES)),
                   pl.BlockSpec((None, None, CODE_ROWS, tx), lambda b, i: (b, i, 0, 0))],
        out_shape=[jax.ShapeDtypeStruct((B, T, XA_WIDTH), F32),
                   jax.ShapeDtypeStruct((CODE_ROWS, META_LANES), jnp.int32),
                   jax.ShapeDtypeStruct((B, T // tx, CODE_ROWS, tx), jnp.int32)],
        scratch_shapes=[pltpu.VMEM((CODE_ROWS, 1), F32)],
        compiler_params=pltpu.CompilerParams(
            dimension_semantics=("arbitrary", "arbitrary"), vmem_limit_bytes=VMEM_LIMIT),
        name="xattn_route",
    )(x, g, wq_t, kx, vx_t, wo, gm, wr_t, br_t)


VISIT_FIRST, VISIT_LAST, VISIT_VALID = 1, 2, 4


def _moe_plan_body(code_ref, cnt_ref, inv_ref, vt_ref, vg_ref, vf_ref, offs_ref, *, ts):
    n = code_ref.shape[0]
    nt, nv = n // ts, vt_ref.shape[0]
    ends = [cnt_ref[0, 0]]
    for g in range(1, N_GROUPS - 1):
        ends.append(ends[-1] + cnt_ref[g, 0])
    offs_ref[0] = 0
    for g in range(1, N_GROUPS):
        offs_ref[g] = ends[g - 1]
    group_bits = N_GROUPS.bit_length() - 1
    assert 1 << group_bits == N_GROUPS

    def place(t, _):
        code = code_ref[t]
        slot = offs_ref[code & (N_GROUPS - 1)] + lax.shift_right_logical(code, group_bits)
        inv_ref[slot] = t
        return 0

    lax.fori_loop(0, n, place, 0, unroll=8)

    def group_of(slot):
        return sum(jnp.where(e <= slot, 1, 0) for e in ends)

    def tile(j, v):
        g_lo, g_hi = group_of(j * ts), group_of(j * ts + ts - 1)
        for k in range(N_GROUPS):
            @pl.when(k <= g_hi - g_lo)
            def _():
                vt_ref[v + k] = j
                vg_ref[v + k] = g_lo + k
                vf_ref[v + k] = (VISIT_VALID + (VISIT_FIRST if k == 0 else 0)
                                 + jnp.where(k == g_hi - g_lo, VISIT_LAST, 0))
        return v + g_hi - g_lo + 1

    used = lax.fori_loop(0, nt, tile, 0)
    g_last = group_of(n - 1)

    def pad(v, _):
        vt_ref[v] = nt - 1
        vg_ref[v] = g_last
        vf_ref[v] = 0
        return 0

    lax.fori_loop(used, nv, pad, 0)


def _moe_plan(code, cnt, *, ts):
    n = code.shape[0]
    nt = n // ts
    nv = nt + N_GROUPS - 1
    smem = pl.BlockSpec(memory_space=pltpu.SMEM)
    visits = jax.ShapeDtypeStruct((nv,), jnp.int32)
    inv, vt, vg, vf = pl.pallas_call(
        functools.partial(_moe_plan_body, ts=ts),
        in_specs=[smem, smem],
        out_specs=[smem, smem, smem, smem],
        out_shape=[jax.ShapeDtypeStruct((n,), jnp.int32), visits, visits, visits],
        scratch_shapes=[pltpu.SMEM((N_GROUPS,), jnp.int32)],
        name="moe_plan",
    )(code, cnt)
    return inv.reshape(nt, 1, ts), vt, vg, vf


def _moe_body(vt_ref, vg_ref, vf_ref, inv_cur_ref, inv_nxt_ref, inv_prv_ref, xa_hbm, gm_ref,
              wg_ref, wu_ref, wd_ref, gf_ref, out_hbm, xbuf, obuf, hm_ref, yacc_ref, gsem, ssem,
              *, ts, nt, sub):
    v = pl.program_id(0)
    t, grp, flags = vt_ref[v], vg_ref[v], vf_ref[v]
    slot = lax.rem(t, 2)
    first = (flags & VISIT_FIRST) != 0
    overlapped = first & (t >= 1)

    def gather_rows(idx_ref, dst_slot, lo, hi):
        for r in range(lo, min(hi, ts)):
            pltpu.make_async_copy(xa_hbm.at[pl.ds(idx_ref[0, r], 1)],
                                  xbuf.at[dst_slot, pl.ds(r, 1)], gsem.at[dst_slot]).start()

    def gather_wait(dst_slot):
        pltpu.make_async_copy(xa_hbm.at[pl.ds(0, ts)], xbuf.at[dst_slot], gsem.at[dst_slot]).wait()

    def scatter_rows(idx_ref, src_slot, lo, hi):
        for r in range(lo, min(hi, ts)):
            pltpu.make_async_copy(obuf.at[src_slot, pl.ds(r, 1)],
                                  out_hbm.at[pl.ds(idx_ref[0, r], 1)], ssem.at[src_slot]).start()

    def scatter_wait(src_slot):
        pltpu.make_async_copy(obuf.at[src_slot], out_hbm.at[pl.ds(0, ts)], ssem.at[src_slot]).wait()

    def experts(issue_dma):
        lane = lax.broadcasted_iota(jnp.int32, (sub, META_LANES), 1)
        n_sub = ts // sub
        dots_per_sub = 3 * EXPERTS_PER_GROUP
        batch = -(-ts // (dots_per_sub * max(n_sub // 2, 1)))
        done = [0, 0]

        def after_dot(r):
            if not issue_dma:
                return
            kind = 0 if r < max(n_sub // 2, 1) else 1
            lo = done[kind]
            if kind == 0:
                gather_rows(inv_nxt_ref, 1 - slot, lo, lo + batch)
            else:
                scatter_rows(inv_prv_ref, 1 - slot, lo, lo + batch)
            done[kind] = min(lo + batch, ts)

        for r in range(n_sub):
            rows = slice(r * sub, (r + 1) * sub)
            meta = xbuf[slot, rows, D_MODEL:]
            row_grp = jnp.sum(jnp.where(lane == LANE_GROUP, meta, 0.0), axis=-1, keepdims=True)
            member = row_grp == grp.astype(F32)
            hm = hm_ref[rows, :]
            y = jnp.zeros((sub, D_MODEL), F32)
            for e in range(EXPERTS_PER_GROUP):
                w = jnp.sum(jnp.where(lane == e, meta, 0.0), axis=-1, keepdims=True)
                w = jnp.where(member, w, 0.0)
                a = _dot(hm, wg_ref[e])
                after_dot(r)
                b = _dot(hm, wu_ref[e])
                after_dot(r)
                u = (a * _sigmoid(a)) * b * w
                y = y + _dot(u.astype(BF16), wd_ref[e])
                after_dot(r)
            yacc_ref[rows, :] += y
        if issue_dma:
            assert done == [ts, ts], done

    @pl.when(v == 0)
    def _():
        gather_rows(inv_cur_ref, 0, 0, ts)

    @pl.when(first)
    def _():
        gather_wait(slot)

        @pl.when(t == 0)
        def _():
            gather_rows(inv_nxt_ref, 1, 0, ts)

        hm_ref[...] = _rms(xbuf[slot, :, :D_MODEL], gm_ref[...]).astype(BF16)
        yacc_ref[...] = jnp.zeros_like(yacc_ref)

    @pl.when(overlapped)
    def _():
        experts(True)

    @pl.when(((flags & VISIT_VALID) != 0) & jnp.logical_not(overlapped))
    def _():
        experts(False)

    @pl.when((flags & VISIT_LAST) != 0)
    def _():
        @pl.when(t >= 2)
        def _():
            scatter_wait(slot)

        obuf[slot] = _rms(xbuf[slot, :, :D_MODEL] + yacc_ref[...], gf_ref[...])

    @pl.when(v == pl.num_programs(0) - 1)
    def _():
        last_slot = (nt - 1) % 2
        scatter_rows(inv_cur_ref, last_slot, 0, ts)
        scatter_wait(0)
        scatter_wait(1)
        gather_wait(1 - last_slot)


def _moe_sorted(xa, inv, vt, vg, vf, gm, wg, wu, wd, gf, *, ts, sub):
    N = xa.shape[0]
    nt = N // ts
    assert nt >= 2
    E, D, FF = EXPERTS_PER_GROUP, D_MODEL, EXPERT_FF
    grp = lambda v, vt, vg, vf: (vg[v], 0, 0)
    const = lambda v, vt, vg, vf: (0, 0)
    smem_row = lambda fn: pl.BlockSpec((None, 1, ts), fn, memory_space=pltpu.SMEM)
    grid_spec = pltpu.PrefetchScalarGridSpec(
        num_scalar_prefetch=3,
        grid=(vt.shape[0],),
        in_specs=[
            smem_row(lambda v, vt, vg, vf: (vt[v], 0, 0)),
            smem_row(lambda v, vt, vg, vf: (jnp.minimum(vt[v] + 1, nt - 1), 0, 0)),
            smem_row(lambda v, vt, vg, vf: (jnp.maximum(vt[v] - 1, 0), 0, 0)),
            pl.BlockSpec(memory_space=pl.ANY),
            pl.BlockSpec((1, D), const),
            pl.BlockSpec((E, D, FF), grp), pl.BlockSpec((E, D, FF), grp),
            pl.BlockSpec((E, FF, D), grp),
            pl.BlockSpec((1, D), const),
        ],
        out_specs=pl.BlockSpec(memory_space=pl.ANY),
        scratch_shapes=[
            pltpu.VMEM((2, ts, XA_WIDTH), F32), pltpu.VMEM((2, ts, D), F32),
            pltpu.VMEM((ts, D), BF16), pltpu.VMEM((ts, D), F32),
            pltpu.SemaphoreType.DMA((2,)), pltpu.SemaphoreType.DMA((2,)),
        ],
    )
    return pl.pallas_call(
        functools.partial(_moe_body, ts=ts, nt=nt, sub=sub),
        grid_spec=grid_spec,
        out_shape=jax.ShapeDtypeStruct((N, D), F32),
        compiler_params=pltpu.CompilerParams(
            dimension_semantics=("arbitrary",), vmem_limit_bytes=VMEM_LIMIT),
        name="moe_sorted",
    )(vt, vg, vf, inv, inv, inv, xa, gm, wg, wu, wd, gf)


def kernel(x, mem, norm_mix_g, w_in, fox_bf, conv_dw_w, conv_dw_b, conv_ln_g, conv_ln_b, conv_pw_w, attn_branch_w, w_out, norm_xa_g, norm_mem_g, xa_wq, xa_wk, xa_wv, xa_wo, norm_moe_g, router_group_w, router_group_b, router_expert_w, router_expert_b, expert_w_gate, expert_w_up, expert_w_down, norm_final_g):
    B, T, D = x.shape
    depth = w_in.shape[0]
    assert depth == 1, "the final norm is fused into the last layer's MoE kernel"
    row = lambda v: v.reshape(1, -1).astype(F32)
    c_glu = 2 * CONV_DIM
    c_q, c_f = c_glu, c_glu + 3 * FOX_DIM
    c_gc = c_f + FOX_HEADS
    c_ga = c_gc + D
    for l in range(depth):
        w = w_in[l]
        u, gc, ga, qkv_t, f_t = _proj_in(
            x, row(norm_mix_g[l]),
            w[:, :c_glu].astype(BF16), w[:, c_gc:c_ga].astype(BF16), w[:, c_ga:].astype(BF16),
            w[:, c_q:c_f].T.astype(BF16), w[:, c_f:c_gc].T.astype(BF16), tm=512)
        qaux, kaux = _fox_prep(f_t, fox_bf[l].reshape(FOX_HEADS, 1).astype(F32))
        attn_t = _fox_attn(qkv_t, qaux, kaux, tq=1024)
        x = _mix_out(x, u, gc, ga, attn_t, conv_dw_w[l].astype(F32), row(conv_dw_b[l]),
                     row(conv_ln_g[l]), row(conv_ln_b[l]), conv_pw_w[l].astype(BF16),
                     attn_branch_w[l].astype(BF16), w_out[l].astype(BF16), tc=512)
        kx, vx_t = _mem_kv(mem, row(norm_mem_g[l]), xa_wk[l].astype(BF16),
                           xa_wv[l].T.astype(BF16))
        wr_t = jnp.zeros((ROUTE_ROWS, D), F32).at[:N_GROUPS].set(router_group_w[l].T)
        br_t = jnp.zeros((ROUTE_ROWS, 1), F32).at[:N_GROUPS, 0].set(router_group_b[l])
        for g in range(N_GROUPS):
            r0, e0 = SUBLANES * (g + 1), EXPERTS_PER_GROUP * g
            wr_t = wr_t.at[r0:r0 + EXPERTS_PER_GROUP].set(
                router_expert_w[l][:, e0:e0 + EXPERTS_PER_GROUP].T)
            br_t = br_t.at[r0:r0 + EXPERTS_PER_GROUP, 0].set(
                router_expert_b[l][e0:e0 + EXPERTS_PER_GROUP])
        xa, cnt, code = _xattn_route(x, row(norm_xa_g[l]), xa_wq[l].T.astype(BF16), kx, vx_t,
                                     xa_wo[l].astype(BF16), row(norm_moe_g[l]),
                                     wr_t.astype(BF16), br_t, tx=1024)
        xa = xa.reshape(B * T, XA_WIDTH)
        inv, vt, vg, vf = _moe_plan(code[:, :, LANE_CODE, :].reshape(B * T), cnt, ts=512)
        x = _moe_sorted(xa, inv, vt, vg, vf, row(norm_moe_g[l]), expert_w_gate[l].astype(BF16),
                        expert_w_up[l].astype(BF16), expert_w_down[l].astype(BF16),
                        row(norm_final_g), ts=512, sub=256).reshape(B, T, D)
    return x
```

```python
import functools

import jax
import jax.numpy as jnp
from jax import lax
from jax.experimental import pallas as pl
from jax.experimental.pallas import tpu as pltpu

D_MODEL = 1024
MEM_LEN = 256
EPS = 1e-6
CONV_DIM = 512
CONV_WIDTH = 31
FOX_HEADS = 8
FOX_HEAD_DIM = 64
FOX_DIM = FOX_HEADS * FOX_HEAD_DIM
XA_HEADS = 4
XA_HEAD_DIM = 128
XA_DIM = XA_HEADS * XA_HEAD_DIM
N_GROUPS = 4
EXPERTS_PER_GROUP = 4
N_EXPERTS = N_GROUPS * EXPERTS_PER_GROUP
EXPERT_FF = 256

AUX_ROWS = 16
LOG2E = 1.4426950408889634
VMEM_LIMIT = 56 * 1024 * 1024

F32 = jnp.float32
BF16 = jnp.bfloat16


def _const_spec(shape):
    n = len(shape)
    return pl.BlockSpec(shape, lambda *_: (0,) * n)


def _rms(x, g):
    return x * lax.rsqrt(jnp.mean(x * x, axis=-1, keepdims=True) + EPS) * g


def _sigmoid(x):
    return 0.5 * jnp.tanh(0.5 * x) + 0.5


def _dot(a, b):
    return jnp.dot(a, b, preferred_element_type=F32)


def _dot_nt(a, b):
    return lax.dot_general(a, b, (((1,), (1,)), ((), ())), preferred_element_type=F32)


def _dot_tn(a, b):
    return lax.dot_general(a, b, (((0,), (0,)), ((), ())), preferred_element_type=F32)


def _proj_in_body(x_ref, g_ref, wglu_ref, wgc_ref, wga_ref, wqkv_t_ref, wf_t_ref,
                  u_ref, gc_ref, ga_ref, qkv_t_ref, f_t_ref):
    h = _rms(x_ref[...], g_ref[...]).astype(BF16)
    glu = _dot(h, wglu_ref[...])
    u_ref[...] = glu[:, :CONV_DIM] * _sigmoid(glu[:, CONV_DIM:])
    gc_ref[...] = _dot(h, wgc_ref[...]).astype(BF16)
    ga_ref[...] = _dot(h, wga_ref[...]).astype(BF16)
    qkv_t = _dot_nt(wqkv_t_ref[...], h)
    scale = FOX_HEAD_DIM ** -0.5 * LOG2E
    qkv_t_ref[:FOX_DIM, :] = (qkv_t[:FOX_DIM] * scale).astype(BF16)
    qkv_t_ref[FOX_DIM:, :] = qkv_t[FOX_DIM:].astype(BF16)
    f_t_ref[...] = _dot_nt(wf_t_ref[...], h)


def _proj_in(x, g, wglu, wgc, wga, wqkv_t, wf_t, *, tm):
    B, T, D = x.shape
    grid = (B, T // tm)
    row = lambda b, i: (b, i, 0)
    col = lambda b, i: (b, 0, i)
    return pl.pallas_call(
        _proj_in_body,
        grid=grid,
        in_specs=[
            pl.BlockSpec((None, tm, D), row),
            _const_spec((1, D)),
            _const_spec(wglu.shape), _const_spec(wgc.shape), _const_spec(wga.shape),
            _const_spec(wqkv_t.shape), _const_spec(wf_t.shape),
        ],
        out_specs=[
            pl.BlockSpec((None, tm, CONV_DIM), row),
            pl.BlockSpec((None, tm, D), row),
            pl.BlockSpec((None, tm, D), row),
            pl.BlockSpec((None, 3 * FOX_DIM, tm), col),
            pl.BlockSpec((None, FOX_HEADS, tm), col),
        ],
        out_shape=[
            jax.ShapeDtypeStruct((B, T, CONV_DIM), F32),
            jax.ShapeDtypeStruct((B, T, D), BF16),
            jax.ShapeDtypeStruct((B, T, D), BF16),
            jax.ShapeDtypeStruct((B, 3 * FOX_DIM, T), BF16),
            jax.ShapeDtypeStruct((B, FOX_HEADS, T), F32),
        ],
        compiler_params=pltpu.CompilerParams(
            dimension_semantics=("parallel", "parallel"), vmem_limit_bytes=VMEM_LIMIT),
        name="proj_in",
    )(x, g, wglu, wgc, wga, wqkv_t, wf_t)


def _split3(c):
    hi = c.astype(BF16)
    r = c - hi.astype(F32)
    mid = r.astype(BF16)
    lo = (r - mid.astype(F32)).astype(BF16)
    return hi, mid, lo


def _fox_prep_body(f_t_ref, bf_ref, qaux_ref, kaux_ref, c_ref, *, chunk):
    T = f_t_ref.shape[-1]
    z = f_t_ref[...] + bf_ref[...]
    logf = jnp.minimum(z, 0.0) - jnp.log1p(jnp.exp(-jnp.abs(z)))
    r_i = lax.broadcasted_iota(jnp.int32, (chunk, chunk), 0)
    c_i = lax.broadcasted_iota(jnp.int32, (chunk, chunk), 1)
    tri = jnp.where(r_i <= c_i, 1.0, 0.0).astype(BF16)
    carry = jnp.zeros((FOX_HEADS, 1), F32)
    for j in range(T // chunk):
        blk = logf[:, j * chunk:(j + 1) * chunk]
        hi, mid, lo = _split3(blk)
        cs = (_dot(lo, tri) + _dot(mid, tri)) + _dot(hi, tri) + carry
        c_ref[:, j * chunk:(j + 1) * chunk] = cs
        carry = cs[:, chunk - 1:chunk]
    row = lax.broadcasted_iota(jnp.int32, (AUX_ROWS, T), 0)
    for h in range(FOX_HEADS):
        hi, mid, lo = _split3(c_ref[h:h + 1, :] * LOG2E)
        pieces = [jnp.broadcast_to(p.astype(F32), (AUX_ROWS, T)) for p in (hi, mid, lo)]
        one = jnp.where(row < 3, 1.0, 0.0)
        val = jnp.where(row == 0, pieces[0], jnp.where(row == 1, pieces[1], pieces[2]))
        val_k = jnp.where(row == 3, pieces[0], jnp.where(row == 4, pieces[1], pieces[2]))
        qaux = jnp.where(row < 3, val, jnp.where(row < 6, 1.0, 0.0))
        kaux = jnp.where(row < 3, one, jnp.where(row < 6, -val_k, 0.0))
        qaux_ref[h] = qaux.astype(BF16)
        kaux_ref[h] = kaux.astype(BF16)


def _fox_prep(f_t, bf):
    B, H, T = f_t.shape
    aux = jax.ShapeDtypeStruct((B, H, AUX_ROWS, T), BF16)
    aux_spec = pl.BlockSpec((None, H, AUX_ROWS, T), lambda b: (b, 0, 0, 0))
    return pl.pallas_call(
        functools.partial(_fox_prep_body, chunk=512),
        grid=(B,),
        in_specs=[pl.BlockSpec((None, H, T), lambda b: (b, 0, 0)), _const_spec((H, 1))],
        out_specs=[aux_spec, aux_spec],
        out_shape=[aux, aux],
        scratch_shapes=[pltpu.VMEM((H, T), F32)],
        compiler_params=pltpu.CompilerParams(dimension_semantics=("parallel",)),
        name="fox_prep",
    )(f_t, bf)


MASK_BLOCK = 256


def _fox_attn_body(q_ref, k_ref, v_ref, qaux_ref, kaux_ref, o_ref, sb0_ref, sb1_ref, p_ref, *, tq):
    T = q_ref.shape[-1]
    dh = FOX_HEAD_DIM
    mb = MASK_BLOCK
    nj = tq // mb
    kk = lax.broadcasted_iota(jnp.int32, (mb, mb), 0)
    qq = lax.broadcasted_iota(jnp.int32, (mb, mb), 1)
    causal = kk <= qq
    ones = jnp.where(lax.broadcasted_iota(jnp.int32, (AUX_ROWS, tq), 0) == 0, 1.0, 0.0).astype(BF16)
    sbufs = (sb0_ref, sb1_ref)

    tasks = []
    for qi in range(T // tq):
        tasks.append((qi, qi, True))
        tasks += [(qi, c, False) for c in range(qi)]

    def scores(task, j, sb, mc):
        qi, c, diag = task
        c0 = j * mb if diag else 0
        r0 = c * tq + j * mb
        ka = jnp.concatenate([k_ref[:, r0:r0 + mb], kaux_ref[:, r0:r0 + mb]], axis=0)
        qa = jnp.concatenate([q_ref[:, qi * tq + c0:(qi + 1) * tq],
                              qaux_ref[:, qi * tq + c0:(qi + 1) * tq]], axis=0)
        s = _dot_tn(ka, qa)
        if diag:
            head = jnp.where(causal, s[:, :mb], -jnp.inf)
            s = head if tq - c0 == mb else jnp.concatenate([head, s[:, mb:]], axis=1)
        sb[j * mb:(j + 1) * mb, c0:] = s
        cm = jnp.max(s, axis=0, keepdims=True)
        if c0:
            cm = jnp.concatenate([jnp.full((1, c0), -jnp.inf, F32), cm], axis=1)
        return cm if mc is None else jnp.maximum(mc, cm)

    def probs(task, j, sb, m_new):
        _, _, diag = task
        c0 = j * mb if diag else 0
        rows = slice(j * mb, (j + 1) * mb)
        p_ref[rows, c0:] = jnp.exp2(sb[rows, c0:] - m_new[:, c0:]).astype(BF16)

    def values(task):
        _, c, diag = task
        va = jnp.concatenate([v_ref[:, c * tq:(c + 1) * tq], ones], axis=0)
        if not diag:
            return _dot(va, p_ref[...])
        return jnp.concatenate(
            [_dot(va[:, :(bj + 1) * mb], p_ref[:(bj + 1) * mb, bj * mb:(bj + 1) * mb])
             for bj in range(nj)], axis=1)

    mc = None
    for j in range(nj):
        mc = scores(tasks[0], j, sbufs[0], mc)
    m_run = acc = None
    for k, task in enumerate(tasks):
        qi, c, diag = task
        nxt = tasks[k + 1] if k + 1 < len(tasks) else None
        m_new = mc if diag else jnp.maximum(m_run, mc)
        mc = None
        for j in range(nj):
            if nxt is not None:
                mc = scores(nxt, j, sbufs[(k + 1) % 2], mc)
            probs(task, j, sbufs[k % 2], m_new)
        pv = values(task)
        acc = pv if diag else jnp.exp2(m_run - m_new) * acc + pv
        m_run = m_new
        if nxt is None or nxt[0] != qi:
            o_ref[:, qi * tq:(qi + 1) * tq] = (acc[:dh] / acc[dh:dh + 1]).astype(o_ref.dtype)


def _fox_attn(qkv_t, qaux, kaux, *, tq):
    B, _, T = qkv_t.shape
    H, dh = FOX_HEADS, FOX_HEAD_DIM
    part = lambda off: pl.BlockSpec((None, dh, T), lambda b, h: (b, off * H + h, 0))
    aux = pl.BlockSpec((None, None, AUX_ROWS, T), lambda b, h: (b, h, 0, 0))
    return pl.pallas_call(
        functools.partial(_fox_attn_body, tq=tq),
        grid=(B, H),
        in_specs=[part(0), part(1), part(2), aux, aux],
        out_specs=pl.BlockSpec((None, dh, T), lambda b, h: (b, h, 0)),
        out_shape=jax.ShapeDtypeStruct((B, H * dh, T), BF16),
        scratch_shapes=[pltpu.VMEM((tq, tq), F32), pltpu.VMEM((tq, tq), F32),
                        pltpu.VMEM((tq, tq), BF16)],
        compiler_params=pltpu.CompilerParams(
            dimension_semantics=("parallel", "parallel"), vmem_limit_bytes=VMEM_LIMIT),
        name="fox_attn",
    )(qkv_t, qkv_t, qkv_t, qaux, kaux)


HALO = 32
SUBLANES = 8


def _mix_out_body(x_ref, u_ref, uprev_ref, gc_ref, ga_ref, attn_t_ref, dww_ref, dwb_ref,
                  lng_ref, lnb_ref, pw_ref, abw_ref, wout_ref, o_ref, ubuf_ref, ushift_ref,
                  *, tc, rb):
    first = pl.program_id(1) == 0
    ubuf_ref[:HALO, :] = jnp.where(first, 0.0, uprev_ref[...])
    ubuf_ref[HALO:, :] = u_ref[...]
    span = tc + HALO - SUBLANES
    for s in range(1, SUBLANES):
        ushift_ref[s - 1, :, :] = ubuf_ref[s:s + span, :]

    def conv_branch(r0):
        acc = jnp.zeros((rb, CONV_DIM), F32)
        for k in range(CONV_WIDTH):
            off = r0 + HALO - (CONV_WIDTH - 1) + k
            s, base = off % SUBLANES, off - off % SUBLANES
            rows = ubuf_ref[base:base + rb, :] if s == 0 else ushift_ref[s - 1, base:base + rb, :]
            acc = acc + rows * dww_ref[k:k + 1, :]
        acc = acc + dwb_ref[...]
        mu = jnp.mean(acc, axis=-1, keepdims=True)
        d = acc - mu
        var = jnp.mean(d * d, axis=-1, keepdims=True)
        y = d * lax.rsqrt(var + EPS) * lng_ref[...] + lnb_ref[...]
        return (y * _sigmoid(y)).astype(BF16)

    def project(r0, y):
        rows = slice(r0, r0 + rb)
        conv_out = _dot(y, pw_ref[...])
        attn_out = _dot_tn(attn_t_ref[:, rows], abw_ref[...])
        merged = (_sigmoid(gc_ref[rows, :].astype(F32)) * conv_out
                  + _sigmoid(ga_ref[rows, :].astype(F32)) * attn_out)
        o_ref[rows, :] = x_ref[rows, :] + _dot(merged.astype(BF16), wout_ref[...])

    nb = tc // rb
    y = conv_branch(0)
    for j in range(nb):
        y_next = conv_branch((j + 1) * rb) if j + 1 < nb else None
        project(j * rb, y)
        y = y_next


def _mix_out(x, u, gc, ga, attn_t, dww, dwb, lng, lnb, pw, abw, wout, *, tc, rb):
    B, T, D = x.shape
    row = lambda b, i: (b, i, 0)
    halo_blocks = tc // HALO
    return pl.pallas_call(
        functools.partial(_mix_out_body, tc=tc, rb=rb),
        grid=(B, T // tc),
        in_specs=[
            pl.BlockSpec((None, tc, D), row),
            pl.BlockSpec((None, tc, CONV_DIM), row),
            pl.BlockSpec((None, HALO, CONV_DIM),
                         lambda b, i: (b, jnp.maximum(i * halo_blocks - 1, 0), 0)),
            pl.BlockSpec((None, tc, D), row),
            pl.BlockSpec((None, tc, D), row),
            pl.BlockSpec((None, FOX_DIM, tc), lambda b, i: (b, 0, i)),
            _const_spec(dww.shape), _const_spec(dwb.shape), _const_spec(lng.shape),
            _const_spec(lnb.shape), _const_spec(pw.shape), _const_spec(abw.shape),
            _const_spec(wout.shape),
        ],
        out_specs=pl.BlockSpec((None, tc, D), row),
        out_shape=jax.ShapeDtypeStruct((B, T, D), F32),
        scratch_shapes=[pltpu.VMEM((HALO + tc, CONV_DIM), F32),
                        pltpu.VMEM((SUBLANES - 1, HALO + tc - SUBLANES, CONV_DIM), F32)],
        compiler_params=pltpu.CompilerParams(
            dimension_semantics=("parallel", "parallel"), vmem_limit_bytes=VMEM_LIMIT),
        name="mix_out",
    )(x, u, u, gc, ga, attn_t, dww, dwb, lng, lnb, pw, abw, wout)


def _mem_kv_body(m_ref, g_ref, wk_ref, wv_t_ref, k_ref, v_t_ref):
    h = _rms(m_ref[...], g_ref[...]).astype(BF16)
    k_ref[...] = _dot(h, wk_ref[...]).astype(BF16)
    v_t_ref[...] = _dot_nt(wv_t_ref[...], h).astype(BF16)


def _mem_kv(mem, g, wk, wv_t):
    B, M, D = mem.shape
    return pl.pallas_call(
        _mem_kv_body,
        grid=(B,),
        in_specs=[pl.BlockSpec((None, M, D), lambda b: (b, 0, 0)), _const_spec((1, D)),
                  _const_spec(wk.shape), _const_spec(wv_t.shape)],
        out_specs=[pl.BlockSpec((None, M, XA_DIM), lambda b: (b, 0, 0)),
                   pl.BlockSpec((None, XA_DIM, M), lambda b: (b, 0, 0))],
        out_shape=[jax.ShapeDtypeStruct((B, M, XA_DIM), BF16),
                   jax.ShapeDtypeStruct((B, XA_DIM, M), BF16)],
        compiler_params=pltpu.CompilerParams(dimension_semantics=("parallel",)),
        name="mem_kv",
    )(mem, g, wk, wv_t)


META_LANES = 128
XA_WIDTH = D_MODEL + META_LANES
LANE_GROUP = EXPERTS_PER_GROUP
LANE_RANK = EXPERTS_PER_GROUP + 1
LANE_CODE = EXPERTS_PER_GROUP + 2
CODE_ROWS = SUBLANES
ROUTE_ROWS = SUBLANES * (N_GROUPS + 1)
CUMSUM_BLOCK = 256


def _first_max_row(v, valid, row):
    neg = jnp.where(valid, v, -jnp.inf)
    mx = jnp.max(neg, axis=0, keepdims=True)
    idx = jnp.min(jnp.where(valid & (neg == mx), row, SUBLANES), axis=0, keepdims=True)
    return mx, idx


def _route(logits_t, run):
    n = logits_t.shape[1]
    row = lax.broadcasted_iota(jnp.int32, (SUBLANES, n), 0)
    valid = row < N_GROUPS
    glog = logits_t[:SUBLANES]
    gmax, g_star = _first_max_row(glog, valid, row)
    p_gsel = 1.0 / jnp.sum(jnp.where(valid, jnp.exp(glog - gmax), 0.0), axis=0, keepdims=True)
    elog = logits_t[SUBLANES:2 * SUBLANES]
    for g in range(1, N_GROUPS):
        elog = jnp.where(g_star == g, logits_t[(g + 1) * SUBLANES:(g + 2) * SUBLANES], elog)
    emax, _ = _first_max_row(elog, valid, row)
    ex = jnp.where(valid, jnp.exp(elog - emax), 0.0)
    p_exp = ex / jnp.sum(ex, axis=0, keepdims=True)
    v1, i1 = _first_max_row(p_exp, valid, row)
    v2, i2 = _first_max_row(p_exp, valid & (row != i1), row)
    tot = v1 + v2
    w = jnp.where(row == i1, p_gsel * (v1 / tot), jnp.where(row == i2, p_gsel * (v2 / tot), 0.0))
    onehot = jnp.where(row == g_star, 1.0, 0.0).astype(BF16)
    cb = min(n, CUMSUM_BLOCK)
    r_i = lax.broadcasted_iota(jnp.int32, (cb, cb), 0)
    c_i = lax.broadcasted_iota(jnp.int32, (cb, cb), 1)
    tri = jnp.where(r_i <= c_i, 1.0, 0.0).astype(BF16)
    pieces = []
    for j in range(n // cb):
        piece = _dot(onehot[:, j * cb:(j + 1) * cb], tri) + run
        pieces.append(piece)
        run = piece[:, cb - 1:cb]
    cum = jnp.concatenate(pieces, axis=1)
    rank = jnp.sum(jnp.where(row == g_star, cum, 0.0), axis=0, keepdims=True) - 1.0
    g_f = g_star.astype(F32)
    meta_t = jnp.where(valid, w, jnp.where(row == LANE_GROUP, g_f, jnp.where(
        row == LANE_RANK, rank, jnp.where(row == LANE_CODE, rank * N_GROUPS + g_f, 0.0))))
    return meta_t, cum[:, n - 1:n]


def _xattn_body(x_ref, g_ref, wq_t_ref, k_ref, v_t_ref, wo_ref, gm_ref, wr_t_ref, br_t_ref,
                xa_ref, cnt_ref, code_ref, run_ref, *, row_blocks):
    @pl.when((pl.program_id(0) == 0) & (pl.program_id(1) == 0))
    def _():
        run_ref[...] = jnp.zeros_like(run_ref)

    tx = x_ref.shape[0]
    hb = tx // row_blocks
    scale = XA_HEAD_DIM ** -0.5

    def attend(r0):
        rows = slice(r0, r0 + hb)
        x = x_ref[rows, :]
        q_t = _dot_nt(wq_t_ref[...], _rms(x, g_ref[...]).astype(BF16)).astype(BF16)
        yield
        head_rows = [slice(h * XA_HEAD_DIM, (h + 1) * XA_HEAD_DIM) for h in range(XA_HEADS)]
        scores = [_dot(k_ref[:, sl], q_t[sl, :]) * scale for sl in head_rows]
        yield
        heads = []
        for sl, s_t in zip(head_rows, scores):
            p = jnp.exp(s_t - jnp.max(s_t, axis=0, keepdims=True))
            l = jnp.sum(p, axis=0, keepdims=True)
            heads.append((_dot(v_t_ref[sl, :], p.astype(BF16)) / l).astype(BF16))
            yield
        x2 = x + _dot_tn(jnp.concatenate(heads, axis=0), wo_ref[...])
        xa_ref[rows, :D_MODEL] = x2
        yield
        hm = _rms(x2, gm_ref[...]).astype(BF16)
        logits[r0] = _dot_nt(wr_t_ref[...], hm) + br_t_ref[...]
        yield

    logits = {}
    blocks = [attend(r0) for r0 in range(0, tx, hb)]
    for _ in range(XA_HEADS + 4):
        for blk in blocks:
            next(blk)
    run = run_ref[...]
    for r0 in range(0, tx, hb):
        meta_t, run = _route(logits[r0], run)
        code_ref[:, r0:r0 + hb] = meta_t.astype(jnp.int32)
        xa_ref[r0:r0 + hb, D_MODEL:] = jnp.concatenate(
            [meta_t, jnp.zeros((META_LANES - CODE_ROWS, hb), F32)], axis=0).T
    run_ref[...] = run
    cnt_ref[...] = jnp.broadcast_to(run, cnt_ref.shape).astype(jnp.int32)


def _xattn_route(x, g, wq_t, kx, vx_t, wo, gm, wr_t, br_t, *, tx, row_blocks):
    B, T, D = x.shape
    row = lambda b, i: (b, i, 0)
    return pl.pallas_call(
        functools.partial(_xattn_body, row_blocks=row_blocks),
        grid=(B, T // tx),
        in_specs=[pl.BlockSpec((None, tx, D), row), _const_spec((1, D)),
                  _const_spec(wq_t.shape),
                  pl.BlockSpec((None, MEM_LEN, XA_DIM), lambda b, i: (b, 0, 0)),
                  pl.BlockSpec((None, XA_DIM, MEM_LEN), lambda b, i: (b, 0, 0)),
                  _const_spec(wo.shape), _const_spec((1, D)),
                  _const_spec(wr_t.shape), _const_spec(br_t.shape)],
        out_specs=[pl.BlockSpec((None, tx, XA_WIDTH), row), _const_spec((CODE_ROWS, META_LANES)),
                   pl.BlockSpec((None, None, CODE_ROWS, tx), lambda b, i: (b, i, 0, 0))],
        out_shape=[jax.ShapeDtypeStruct((B, T, XA_WIDTH), F32),
                   jax.ShapeDtypeStruct((CODE_ROWS, META_LANES), jnp.int32),
                   jax.ShapeDtypeStruct((B, T // tx, CODE_ROWS, tx), jnp.int32)],
        scratch_shapes=[pltpu.VMEM((CODE_ROWS, 1), F32)],
        compiler_params=pltpu.CompilerParams(
            dimension_semantics=("arbitrary", "arbitrary"), vmem_limit_bytes=VMEM_LIMIT),
        name="xattn_route",
    )(x, g, wq_t, kx, vx_t, wo, gm, wr_t, br_t)


VISIT_FIRST, VISIT_LAST, VISIT_VALID = 1, 2, 4


def _moe_plan_body(code_ref, cnt_ref, inv_ref, vt_ref, vg_ref, vf_ref, offs_ref, *, ts):
    n = code_ref.shape[0]
    nt, nv = n // ts, vt_ref.shape[0]
    ends = [cnt_ref[0, 0]]
    for g in range(1, N_GROUPS - 1):
        ends.append(ends[-1] + cnt_ref[g, 0])
    offs_ref[0] = 0
    for g in range(1, N_GROUPS):
        offs_ref[g] = ends[g - 1]
    group_bits = N_GROUPS.bit_length() - 1
    assert 1 << group_bits == N_GROUPS

    def place(t, _):
        code = code_ref[t]
        slot = offs_ref[code & (N_GROUPS - 1)] + lax.shift_right_logical(code, group_bits)
        inv_ref[slot] = t
        return 0

    lax.fori_loop(0, n, place, 0, unroll=8)

    def group_of(slot):
        return sum(jnp.where(e <= slot, 1, 0) for e in ends)

    def tile(j, v):
        g_lo, g_hi = group_of(j * ts), group_of(j * ts + ts - 1)
        for k in range(N_GROUPS):
            @pl.when(k <= g_hi - g_lo)
            def _():
                vt_ref[v + k] = j
                vg_ref[v + k] = g_lo + k
                vf_ref[v + k] = (VISIT_VALID + (VISIT_FIRST if k == 0 else 0)
                                 + jnp.where(k == g_hi - g_lo, VISIT_LAST, 0))
        return v + g_hi - g_lo + 1

    used = lax.fori_loop(0, nt, tile, 0)
    g_last = group_of(n - 1)

    def pad(v, _):
        vt_ref[v] = nt - 1
        vg_ref[v] = g_last
        vf_ref[v] = 0
        return 0

    lax.fori_loop(used, nv, pad, 0)


def _moe_plan(code, cnt, *, ts):
    n = code.shape[0]
    nt = n // ts
    nv = nt + N_GROUPS - 1
    smem = pl.BlockSpec(memory_space=pltpu.SMEM)
    visits = jax.ShapeDtypeStruct((nv,), jnp.int32)
    inv, vt, vg, vf = pl.pallas_call(
        functools.partial(_moe_plan_body, ts=ts),
        in_specs=[smem, smem],
        out_specs=[smem, smem, smem, smem],
        out_shape=[jax.ShapeDtypeStruct((n,), jnp.int32), visits, visits, visits],
        scratch_shapes=[pltpu.SMEM((N_GROUPS,), jnp.int32)],
        name="moe_plan",
    )(code, cnt)
    return inv.reshape(nt, 1, ts), vt, vg, vf


def _moe_body(vt_ref, vg_ref, vf_ref, inv_cur_ref, inv_nxt_ref, inv_prv_ref, xa_hbm, gm_ref,
              wg_ref, wu_ref, wd_ref, gf_ref, out_hbm, xbuf, obuf, hm_ref, yacc_ref, gsem, ssem,
              *, ts, nt, sub):
    v = pl.program_id(0)
    t, grp, flags = vt_ref[v], vg_ref[v], vf_ref[v]
    slot = lax.rem(t, 2)
    first = (flags & VISIT_FIRST) != 0
    overlapped = first & (t >= 1)

    def gather_rows(idx_ref, dst_slot, lo, hi):
        for r in range(lo, min(hi, ts)):
            pltpu.make_async_copy(xa_hbm.at[pl.ds(idx_ref[0, r], 1)],
                                  xbuf.at[dst_slot, pl.ds(r, 1)], gsem.at[dst_slot]).start()

    def gather_wait(dst_slot):
        pltpu.make_async_copy(xa_hbm.at[pl.ds(0, ts)], xbuf.at[dst_slot], gsem.at[dst_slot]).wait()

    def scatter_rows(idx_ref, src_slot, lo, hi):
        for r in range(lo, min(hi, ts)):
            pltpu.make_async_copy(obuf.at[src_slot, pl.ds(r, 1)],
                                  out_hbm.at[pl.ds(idx_ref[0, r], 1)], ssem.at[src_slot]).start()

    def scatter_wait(src_slot):
        pltpu.make_async_copy(obuf.at[src_slot], out_hbm.at[pl.ds(0, ts)], ssem.at[src_slot]).wait()

    def experts(issue_dma):
        lane = lax.broadcasted_iota(jnp.int32, (sub, META_LANES), 1)
        n_sub = ts // sub
        dots_per_sub = 3 * EXPERTS_PER_GROUP
        batch = -(-ts // (dots_per_sub * max(n_sub // 2, 1)))
        done = [0, 0]

        def after_dot(r):
            if not issue_dma:
                return
            kind = 0 if r < max(n_sub // 2, 1) else 1
            lo = done[kind]
            if kind == 0:
                gather_rows(inv_nxt_ref, 1 - slot, lo, lo + batch)
            else:
                scatter_rows(inv_prv_ref, 1 - slot, lo, lo + batch)
            done[kind] = min(lo + batch, ts)

        for r in range(n_sub):
            rows = slice(r * sub, (r + 1) * sub)
            meta = xbuf[slot, rows, D_MODEL:]
            row_grp = jnp.sum(jnp.where(lane == LANE_GROUP, meta, 0.0), axis=-1, keepdims=True)
            member = row_grp == grp.astype(F32)
            hm = hm_ref[rows, :]
            y = jnp.zeros((sub, D_MODEL), F32)
            for e in range(EXPERTS_PER_GROUP):
                w = jnp.sum(jnp.where(lane == e, meta, 0.0), axis=-1, keepdims=True)
                w = jnp.where(member, w, 0.0)
                a = _dot(hm, wg_ref[e])
                after_dot(r)
                b = _dot(hm, wu_ref[e])
                after_dot(r)
                u = (a * _sigmoid(a)) * b * w
                y = y + _dot(u.astype(BF16), wd_ref[e])
                after_dot(r)
            yacc_ref[rows, :] += y
        if issue_dma:
            assert done == [ts, ts], done

    @pl.when(v == 0)
    def _():
        gather_rows(inv_cur_ref, 0, 0, ts)

    @pl.when(first)
    def _():
        gather_wait(slot)

        @pl.when(t == 0)
        def _():
            gather_rows(inv_nxt_ref, 1, 0, ts)

        hm_ref[...] = _rms(xbuf[slot, :, :D_MODEL], gm_ref[...]).astype(BF16)
        yacc_ref[...] = jnp.zeros_like(yacc_ref)

    @pl.when(overlapped)
    def _():
        experts(True)

    @pl.when(((flags & VISIT_VALID) != 0) & jnp.logical_not(overlapped))
    def _():
        experts(False)

    @pl.when((flags & VISIT_LAST) != 0)
    def _():
        @pl.when(t >= 2)
        def _():
            scatter_wait(slot)

        obuf[slot] = _rms(xbuf[slot, :, :D_MODEL] + yacc_ref[...], gf_ref[...])

    @pl.when(v == pl.num_programs(0) - 1)
    def _():
        last_slot = (nt - 1) % 2
        scatter_rows(inv_cur_ref, last_slot, 0, ts)
        scatter_wait(0)
        scatter_wait(1)
        gather_wait(1 - last_slot)


def _moe_sorted(xa, inv, vt, vg, vf, gm, wg, wu, wd, gf, *, ts, sub):
    N = xa.shape[0]
    nt = N // ts
    assert nt >= 2
    E, D, FF = EXPERTS_PER_GROUP, D_MODEL, EXPERT_FF
    grp = lambda v, vt, vg, vf: (vg[v], 0, 0)
    const = lambda v, vt, vg, vf: (0, 0)
    smem_row = lambda fn: pl.BlockSpec((None, 1, ts), fn, memory_space=pltpu.SMEM)
    grid_spec = pltpu.PrefetchScalarGridSpec(
        num_scalar_prefetch=3,
        grid=(vt.shape[0],),
        in_specs=[
            smem_row(lambda v, vt, vg, vf: (vt[v], 0, 0)),
            smem_row(lambda v, vt, vg, vf: (jnp.minimum(vt[v] + 1, nt - 1), 0, 0)),
            smem_row(lambda v, vt, vg, vf: (jnp.maximum(vt[v] - 1, 0), 0, 0)),
            pl.BlockSpec(memory_space=pl.ANY),
            pl.BlockSpec((1, D), const),
            pl.BlockSpec((E, D, FF), grp), pl.BlockSpec((E, D, FF), grp),
            pl.BlockSpec((E, FF, D), grp),
            pl.BlockSpec((1, D), const),
        ],
        out_specs=pl.BlockSpec(memory_space=pl.ANY),
        scratch_shapes=[
            pltpu.VMEM((2, ts, XA_WIDTH), F32), pltpu.VMEM((2, ts, D), F32),
            pltpu.VMEM((ts, D), BF16), pltpu.VMEM((ts, D), F32),
            pltpu.SemaphoreType.DMA((2,)), pltpu.SemaphoreType.DMA((2,)),
        ],
    )
    return pl.pallas_call(
        functools.partial(_moe_body, ts=ts, nt=nt, sub=sub),
        grid_spec=grid_spec,
        out_shape=jax.ShapeDtypeStruct((N, D), F32),
        compiler_params=pltpu.CompilerParams(
            dimension_semantics=("arbitrary",), vmem_limit_bytes=VMEM_LIMIT),
        name="moe_sorted",
    )(vt, vg, vf, inv, inv, inv, xa, gm, wg, wu, wd, gf)


def kernel(x, mem, norm_mix_g, w_in, fox_bf, conv_dw_w, conv_dw_b, conv_ln_g, conv_ln_b, conv_pw_w, attn_branch_w, w_out, norm_xa_g, norm_mem_g, xa_wq, xa_wk, xa_wv, xa_wo, norm_moe_g, router_group_w, router_group_b, router_expert_w, router_expert_b, expert_w_gate, expert_w_up, expert_w_down, norm_final_g):
    B, T, D = x.shape
    depth = w_in.shape[0]
    assert depth == 1, "the final norm is fused into the last layer's MoE kernel"
    row = lambda v: v.reshape(1, -1).astype(F32)
    c_glu = 2 * CONV_DIM
    c_q, c_f = c_glu, c_glu + 3 * FOX_DIM
    c_gc = c_f + FOX_HEADS
    c_ga = c_gc + D
    for l in range(depth):
        w = w_in[l]
        u, gc, ga, qkv_t, f_t = _proj_in(
            x, row(norm_mix_g[l]),
            w[:, :c_glu].astype(BF16), w[:, c_gc:c_ga].astype(BF16), w[:, c_ga:].astype(BF16),
            w[:, c_q:c_f].T.astype(BF16), w[:, c_f:c_gc].T.astype(BF16), tm=512)
        qaux, kaux = _fox_prep(f_t, fox_bf[l].reshape(FOX_HEADS, 1).astype(F32))
        attn_t = _fox_attn(qkv_t, qaux, kaux, tq=1024)
        x = _mix_out(x, u, gc, ga, attn_t, conv_dw_w[l].astype(F32), row(conv_dw_b[l]),
                     row(conv_ln_g[l]), row(conv_ln_b[l]), conv_pw_w[l].astype(BF16),
                     attn_branch_w[l].astype(BF16), w_out[l].astype(BF16), tc=512, rb=512)
        kx, vx_t = _mem_kv(mem, row(norm_mem_g[l]), xa_wk[l].astype(BF16),
                           xa_wv[l].T.astype(BF16))
        wr_t = jnp.zeros((ROUTE_ROWS, D), F32).at[:N_GROUPS].set(router_group_w[l].T)
        br_t = jnp.zeros((ROUTE_ROWS, 1), F32).at[:N_GROUPS, 0].set(router_group_b[l])
        for g in range(N_GROUPS):
            r0, e0 = SUBLANES * (g + 1), EXPERTS_PER_GROUP * g
            wr_t = wr_t.at[r0:r0 + EXPERTS_PER_GROUP].set(
                router_expert_w[l][:, e0:e0 + EXPERTS_PER_GROUP].T)
            br_t = br_t.at[r0:r0 + EXPERTS_PER_GROUP, 0].set(
                router_expert_b[l][e0:e0 + EXPERTS_PER_GROUP])
        xa, cnt, code = _xattn_route(x, row(norm_xa_g[l]), xa_wq[l].T.astype(BF16), kx, vx_t,
                                     xa_wo[l].astype(BF16), row(norm_moe_g[l]),
                                     wr_t.astype(BF16), br_t, tx=1024, row_blocks=1)
        xa = xa.reshape(B * T, XA_WIDTH)
        inv, vt, vg, vf = _moe_plan(code[:, :, LANE_CODE, :].reshape(B * T), cnt, ts=512)
        x = _moe_sorted(xa, inv, vt, vg, vf, row(norm_moe_g[l]), expert_w_gate[l].astype(BF16),
                        expert_w_up[l].astype(BF16), expert_w_down[l].astype(BF16),
                        row(norm_final_g), ts=512, sub=256).reshape(B, T, D)
    return x
```

```python
import functools

import jax
import jax.numpy as jnp
from jax import lax
from jax.experimental import pallas as pl
from jax.experimental.pallas import tpu as pltpu

D_MODEL = 1024
MEM_LEN = 256
EPS = 1e-6
CONV_DIM = 512
CONV_WIDTH = 31
FOX_HEADS = 8
FOX_HEAD_DIM = 64
FOX_DIM = FOX_HEADS * FOX_HEAD_DIM
XA_HEADS = 4
XA_HEAD_DIM = 128
XA_DIM = XA_HEADS * XA_HEAD_DIM
N_GROUPS = 4
EXPERTS_PER_GROUP = 4
N_EXPERTS = N_GROUPS * EXPERTS_PER_GROUP
EXPERT_FF = 256

AUX_ROWS = 16
LOG2E = 1.4426950408889634
VMEM_LIMIT = 56 * 1024 * 1024

F32 = jnp.float32
BF16 = jnp.bfloat16


def _const_spec(shape):
    n = len(shape)
    return pl.BlockSpec(shape, lambda *_: (0,) * n)


def _rms(x, g):
    return x * lax.rsqrt(jnp.mean(x * x, axis=-1, keepdims=True) + EPS) * g


def _sigmoid(x):
    return 0.5 * jnp.tanh(0.5 * x) + 0.5


def _dot(a, b):
    return jnp.dot(a, b, preferred_element_type=F32)


def _dot_nt(a, b):
    return lax.dot_general(a, b, (((1,), (1,)), ((), ())), preferred_element_type=F32)


def _dot_tn(a, b):
    return lax.dot_general(a, b, (((0,), (0,)), ((), ())), preferred_element_type=F32)


C_GLU = 2 * CONV_DIM
C_F = C_GLU + 3 * FOX_DIM
C_GC = C_F + FOX_HEADS
C_GA = C_GC + D_MODEL
LANES = 128


def _proj_in_body(x_ref, g_ref, w_ref, u_ref, gc_ref, ga_ref, qkv_t_ref, f_t_ref,
                  wglu_ref, wgc_ref, wga_ref, wqkv_t_ref, wf_ref):
    @pl.when((pl.program_id(0) == 0) & (pl.program_id(1) == 0))
    def _():
        wglu_ref[...] = w_ref[:, :C_GLU].astype(BF16)
        wgc_ref[...] = w_ref[:, C_GC:C_GA].astype(BF16)
        wga_ref[...] = w_ref[:, C_GA:].astype(BF16)
        for j in range(3):
            cols = slice(C_GLU + j * FOX_DIM, C_GLU + (j + 1) * FOX_DIM)
            wqkv_t_ref[j * FOX_DIM:(j + 1) * FOX_DIM, :] = w_ref[:, cols].T.astype(BF16)
        wf_ref[...] = w_ref[:, C_F:C_F + LANES].astype(BF16)

    h = _rms(x_ref[...], g_ref[...]).astype(BF16)
    glu = _dot(h, wglu_ref[...])
    u_ref[...] = glu[:, :CONV_DIM] * _sigmoid(glu[:, CONV_DIM:])
    gc_ref[...] = _dot(h, wgc_ref[...]).astype(BF16)
    ga_ref[...] = _dot(h, wga_ref[...]).astype(BF16)
    qkv_t = _dot_nt(wqkv_t_ref[...], h)
    scale = FOX_HEAD_DIM ** -0.5 * LOG2E
    qkv_t_ref[:FOX_DIM, :] = (qkv_t[:FOX_DIM] * scale).astype(BF16)
    qkv_t_ref[FOX_DIM:, :] = qkv_t[FOX_DIM:].astype(BF16)
    f_t_ref[...] = _dot(h, wf_ref[...]).T[:FOX_HEADS, :]


def _proj_in(x, g, w, *, tm):
    B, T, D = x.shape
    grid = (B, T // tm)
    row = lambda b, i: (b, i, 0)
    col = lambda b, i: (b, 0, i)
    return pl.pallas_call(
        _proj_in_body,
        grid=grid,
        in_specs=[
            pl.BlockSpec((None, tm, D), row),
            _const_spec((1, D)),
            pl.BlockSpec(w.shape, lambda b, i: (0, 0), pipeline_mode=pl.Buffered(1)),
        ],
        out_specs=[
            pl.BlockSpec((None, tm, CONV_DIM), row),
            pl.BlockSpec((None, tm, D), row),
            pl.BlockSpec((None, tm, D), row),
            pl.BlockSpec((None, 3 * FOX_DIM, tm), col),
            pl.BlockSpec((None, FOX_HEADS, tm), col),
        ],
        out_shape=[
            jax.ShapeDtypeStruct((B, T, CONV_DIM), F32),
            jax.ShapeDtypeStruct((B, T, D), BF16),
            jax.ShapeDtypeStruct((B, T, D), BF16),
            jax.ShapeDtypeStruct((B, 3 * FOX_DIM, T), BF16),
            jax.ShapeDtypeStruct((B, FOX_HEADS, T), F32),
        ],
        scratch_shapes=[pltpu.VMEM((D, C_GLU), BF16), pltpu.VMEM((D, D), BF16),
                        pltpu.VMEM((D, D), BF16), pltpu.VMEM((3 * FOX_DIM, D), BF16),
                        pltpu.VMEM((D, LANES), BF16)],
        compiler_params=pltpu.CompilerParams(
            dimension_semantics=("arbitrary", "arbitrary"), vmem_limit_bytes=VMEM_LIMIT),
        name="proj_in",
    )(x, g, w)


def _split3(c):
    hi = c.astype(BF16)
    r = c - hi.astype(F32)
    mid = r.astype(BF16)
    lo = (r - mid.astype(F32)).astype(BF16)
    return hi, mid, lo


def _fox_prep_body(f_t_ref, bf_ref, qaux_ref, kaux_ref, c_ref, *, chunk):
    T = f_t_ref.shape[-1]
    z = f_t_ref[...] + bf_ref[...]
    logf = jnp.minimum(z, 0.0) - jnp.log1p(jnp.exp(-jnp.abs(z)))
    r_i = lax.broadcasted_iota(jnp.int32, (chunk, chunk), 0)
    c_i = lax.broadcasted_iota(jnp.int32, (chunk, chunk), 1)
    tri = jnp.where(r_i <= c_i, 1.0, 0.0).astype(BF16)
    carry = jnp.zeros((FOX_HEADS, 1), F32)
    for j in range(T // chunk):
        blk = logf[:, j * chunk:(j + 1) * chunk]
        hi, mid, lo = _split3(blk)
        cs = (_dot(lo, tri) + _dot(mid, tri)) + _dot(hi, tri) + carry
        c_ref[:, j * chunk:(j + 1) * chunk] = cs
        carry = cs[:, chunk - 1:chunk]
    row = lax.broadcasted_iota(jnp.int32, (AUX_ROWS, T), 0)
    for h in range(FOX_HEADS):
        hi, mid, lo = _split3(c_ref[h:h + 1, :] * LOG2E)
        pieces = [jnp.broadcast_to(p.astype(F32), (AUX_ROWS, T)) for p in (hi, mid, lo)]
        one = jnp.where(row < 3, 1.0, 0.0)
        val = jnp.where(row == 0, pieces[0], jnp.where(row == 1, pieces[1], pieces[2]))
        val_k = jnp.where(row == 3, pieces[0], jnp.where(row == 4, pieces[1], pieces[2]))
        qaux = jnp.where(row < 3, val, jnp.where(row < 6, 1.0, 0.0))
        kaux = jnp.where(row < 3, one, jnp.where(row < 6, -val_k, 0.0))
        qaux_ref[h] = qaux.astype(BF16)
        kaux_ref[h] = kaux.astype(BF16)


def _fox_prep(f_t, bf):
    B, H, T = f_t.shape
    aux = jax.ShapeDtypeStruct((B, H, AUX_ROWS, T), BF16)
    aux_spec = pl.BlockSpec((None, H, AUX_ROWS, T), lambda b: (b, 0, 0, 0))
    return pl.pallas_call(
        functools.partial(_fox_prep_body, chunk=512),
        grid=(B,),
        in_specs=[pl.BlockSpec((None, H, T), lambda b: (b, 0, 0)), _const_spec((H, 1))],
        out_specs=[aux_spec, aux_spec],
        out_shape=[aux, aux],
        scratch_shapes=[pltpu.VMEM((H, T), F32)],
        compiler_params=pltpu.CompilerParams(dimension_semantics=("parallel",)),
        name="fox_prep",
    )(f_t, bf)


MASK_BLOCK = 256


def _fox_attn_body(q_ref, k_ref, v_ref, qaux_ref, kaux_ref, o_ref, sb0_ref, sb1_ref, p_ref, *, tq):
    T = q_ref.shape[-1]
    dh = FOX_HEAD_DIM
    mb = MASK_BLOCK
    nj = tq // mb
    kk = lax.broadcasted_iota(jnp.int32, (mb, mb), 0)
    qq = lax.broadcasted_iota(jnp.int32, (mb, mb), 1)
    causal = kk <= qq
    ones = jnp.where(lax.broadcasted_iota(jnp.int32, (AUX_ROWS, tq), 0) == 0, 1.0, 0.0).astype(BF16)
    sbufs = (sb0_ref, sb1_ref)

    tasks = []
    for qi in range(T // tq):
        tasks.append((qi, qi, True))
        tasks += [(qi, c, False) for c in range(qi)]

    def scores(task, j, sb, mc):
        qi, c, diag = task
        c0 = j * mb if diag else 0
        r0 = c * tq + j * mb
        ka = jnp.concatenate([k_ref[:, r0:r0 + mb], kaux_ref[:, r0:r0 + mb]], axis=0)
        qa = jnp.concatenate([q_ref[:, qi * tq + c0:(qi + 1) * tq],
                              qaux_ref[:, qi * tq + c0:(qi + 1) * tq]], axis=0)
        s = _dot_tn(ka, qa)
        if diag:
            head = jnp.where(causal, s[:, :mb], -jnp.inf)
            s = head if tq - c0 == mb else jnp.concatenate([head, s[:, mb:]], axis=1)
        sb[j * mb:(j + 1) * mb, c0:] = s
        cm = jnp.max(s, axis=0, keepdims=True)
        if c0:
            cm = jnp.concatenate([jnp.full((1, c0), -jnp.inf, F32), cm], axis=1)
        return cm if mc is None else jnp.maximum(mc, cm)

    def probs(task, j, sb, m_new):
        _, _, diag = task
        c0 = j * mb if diag else 0
        rows = slice(j * mb, (j + 1) * mb)
        p_ref[rows, c0:] = jnp.exp2(sb[rows, c0:] - m_new[:, c0:]).astype(BF16)

    def values(task):
        _, c, diag = task
        va = jnp.concatenate([v_ref[:, c * tq:(c + 1) * tq], ones], axis=0)
        if not diag:
            return _dot(va, p_ref[...])
        return jnp.concatenate(
            [_dot(va[:, :(bj + 1) * mb], p_ref[:(bj + 1) * mb, bj * mb:(bj + 1) * mb])
             for bj in range(nj)], axis=1)

    mc = None
    for j in range(nj):
        mc = scores(tasks[0], j, sbufs[0], mc)
    m_run = acc = None
    for k, task in enumerate(tasks):
        qi, c, diag = task
        nxt = tasks[k + 1] if k + 1 < len(tasks) else None
        m_new = mc if diag else jnp.maximum(m_run, mc)
        mc = None
        for j in range(nj):
            if nxt is not None:
                mc = scores(nxt, j, sbufs[(k + 1) % 2], mc)
            probs(task, j, sbufs[k % 2], m_new)
        pv = values(task)
        acc = pv if diag else jnp.exp2(m_run - m_new) * acc + pv
        m_run = m_new
        if nxt is None or nxt[0] != qi:
            o_ref[:, qi * tq:(qi + 1) * tq] = (acc[:dh] / acc[dh:dh + 1]).astype(o_ref.dtype)


def _fox_attn(qkv_t, qaux, kaux, *, tq):
    B, _, T = qkv_t.shape
    H, dh = FOX_HEADS, FOX_HEAD_DIM
    part = lambda off: pl.BlockSpec((None, dh, T), lambda b, h: (b, off * H + h, 0))
    aux = pl.BlockSpec((None, None, AUX_ROWS, T), lambda b, h: (b, h, 0, 0))
    return pl.pallas_call(
        functools.partial(_fox_attn_body, tq=tq),
        grid=(B, H),
        in_specs=[part(0), part(1), part(2), aux, aux],
        out_specs=pl.BlockSpec((None, dh, T), lambda b, h: (b, h, 0)),
        out_shape=jax.ShapeDtypeStruct((B, H * dh, T), BF16),
        scratch_shapes=[pltpu.VMEM((tq, tq), F32), pltpu.VMEM((tq, tq), F32),
                        pltpu.VMEM((tq, tq), BF16)],
        compiler_params=pltpu.CompilerParams(
            dimension_semantics=("parallel", "parallel"), vmem_limit_bytes=VMEM_LIMIT),
        name="fox_attn",
    )(qkv_t, qkv_t, qkv_t, qaux, kaux)


HALO = 32
SUBLANES = 8


def _mix_out_body(x_ref, u_ref, uprev_ref, gc_ref, ga_ref, attn_t_ref, dww_ref, dwb_ref,
                  lng_ref, lnb_ref, pw32_ref, abw32_ref, wout32_ref, o_ref, ubuf_ref, ushift_ref,
                  pw_ref, abw_ref, wout_ref, *, tc, rb):
    first = pl.program_id(1) == 0

    @pl.when(first & (pl.program_id(0) == 0))
    def _():
        pw_ref[...] = pw32_ref[...].astype(BF16)
        abw_ref[...] = abw32_ref[...].astype(BF16)
        wout_ref[...] = wout32_ref[...].astype(BF16)

    ubuf_ref[:HALO, :] = jnp.where(first, 0.0, uprev_ref[...])
    ubuf_ref[HALO:, :] = u_ref[...]
    span = tc + HALO - SUBLANES
    for s in range(1, SUBLANES):
        ushift_ref[s - 1, :, :] = ubuf_ref[s:s + span, :]

    def conv_branch(r0):
        acc = jnp.zeros((rb, CONV_DIM), F32)
        for k in range(CONV_WIDTH):
            off = r0 + HALO - (CONV_WIDTH - 1) + k
            s, base = off % SUBLANES, off - off % SUBLANES
            rows = ubuf_ref[base:base + rb, :] if s == 0 else ushift_ref[s - 1, base:base + rb, :]
            acc = acc + rows * dww_ref[k:k + 1, :]
        acc = acc + dwb_ref[...]
        mu = jnp.mean(acc, axis=-1, keepdims=True)
        d = acc - mu
        var = jnp.mean(d * d, axis=-1, keepdims=True)
        y = d * lax.rsqrt(var + EPS) * lng_ref[...] + lnb_ref[...]
        return (y * _sigmoid(y)).astype(BF16)

    def project(r0, y):
        rows = slice(r0, r0 + rb)
        conv_out = _dot(y, pw_ref[...])
        attn_out = _dot_tn(attn_t_ref[:, rows], abw_ref[...])
        merged = (_sigmoid(gc_ref[rows, :].astype(F32)) * conv_out
                  + _sigmoid(ga_ref[rows, :].astype(F32)) * attn_out)
        o_ref[rows, :] = x_ref[rows, :] + _dot(merged.astype(BF16), wout_ref[...])

    nb = tc // rb
    y = conv_branch(0)
    for j in range(nb):
        y_next = conv_branch((j + 1) * rb) if j + 1 < nb else None
        project(j * rb, y)
        y = y_next


def _mix_out(x, u, gc, ga, attn_t, dww, dwb, lng, lnb, pw, abw, wout, *, tc, rb):
    B, T, D = x.shape
    row = lambda b, i: (b, i, 0)
    halo_blocks = tc // HALO
    return pl.pallas_call(
        functools.partial(_mix_out_body, tc=tc, rb=rb),
        grid=(B, T // tc),
        in_specs=[
            pl.BlockSpec((None, tc, D), row),
            pl.BlockSpec((None, tc, CONV_DIM), row),
            pl.BlockSpec((None, HALO, CONV_DIM),
                         lambda b, i: (b, jnp.maximum(i * halo_blocks - 1, 0), 0)),
            pl.BlockSpec((None, tc, D), row),
            pl.BlockSpec((None, tc, D), row),
            pl.BlockSpec((None, FOX_DIM, tc), lambda b, i: (b, 0, i)),
            _const_spec(dww.shape), _const_spec(dwb.shape), _const_spec(lng.shape),
            _const_spec(lnb.shape), _const_spec(pw.shape), _const_spec(abw.shape),
            _const_spec(wout.shape),
        ],
        out_specs=pl.BlockSpec((None, tc, D), row),
        out_shape=jax.ShapeDtypeStruct((B, T, D), F32),
        scratch_shapes=[pltpu.VMEM((HALO + tc, CONV_DIM), F32),
                        pltpu.VMEM((SUBLANES - 1, HALO + tc - SUBLANES, CONV_DIM), F32),
                        pltpu.VMEM(pw.shape, BF16), pltpu.VMEM(abw.shape, BF16),
                        pltpu.VMEM(wout.shape, BF16)],
        compiler_params=pltpu.CompilerParams(
            dimension_semantics=("arbitrary", "arbitrary"), vmem_limit_bytes=VMEM_LIMIT),
        name="mix_out",
    )(x, u, u, gc, ga, attn_t, dww, dwb, lng, lnb, pw, abw, wout)


def _mem_kv_body(m_ref, g_ref, wk_ref, wv_t_ref, k_ref, v_t_ref):
    h = _rms(m_ref[...], g_ref[...]).astype(BF16)
    k_ref[...] = _dot(h, wk_ref[...]).astype(BF16)
    v_t_ref[...] = _dot_nt(wv_t_ref[...], h).astype(BF16)


def _mem_kv(mem, g, wk, wv_t):
    B, M, D = mem.shape
    return pl.pallas_call(
        _mem_kv_body,
        grid=(B,),
        in_specs=[pl.BlockSpec((None, M, D), lambda b: (b, 0, 0)), _const_spec((1, D)),
                  _const_spec(wk.shape), _const_spec(wv_t.shape)],
        out_specs=[pl.BlockSpec((None, M, XA_DIM), lambda b: (b, 0, 0)),
                   pl.BlockSpec((None, XA_DIM, M), lambda b: (b, 0, 0))],
        out_shape=[jax.ShapeDtypeStruct((B, M, XA_DIM), BF16),
                   jax.ShapeDtypeStruct((B, XA_DIM, M), BF16)],
        compiler_params=pltpu.CompilerParams(dimension_semantics=("parallel",)),
        name="mem_kv",
    )(mem, g, wk, wv_t)


META_LANES = 128
XA_WIDTH = D_MODEL + META_LANES
LANE_GROUP = EXPERTS_PER_GROUP
LANE_RANK = EXPERTS_PER_GROUP + 1
LANE_CODE = EXPERTS_PER_GROUP + 2
CODE_ROWS = SUBLANES
ROUTE_ROWS = SUBLANES * (N_GROUPS + 1)
CUMSUM_BLOCK = 256


def _first_max_row(v, valid, row):
    neg = jnp.where(valid, v, -jnp.inf)
    mx = jnp.max(neg, axis=0, keepdims=True)
    idx = jnp.min(jnp.where(valid & (neg == mx), row, SUBLANES), axis=0, keepdims=True)
    return mx, idx


def _route(logits_t, run):
    n = logits_t.shape[1]
    row = lax.broadcasted_iota(jnp.int32, (SUBLANES, n), 0)
    valid = row < N_GROUPS
    glog = logits_t[:SUBLANES]
    gmax, g_star = _first_max_row(glog, valid, row)
    p_gsel = 1.0 / jnp.sum(jnp.where(valid, jnp.exp(glog - gmax), 0.0), axis=0, keepdims=True)
    elog = logits_t[SUBLANES:2 * SUBLANES]
    for g in range(1, N_GROUPS):
        elog = jnp.where(g_star == g, logits_t[(g + 1) * SUBLANES:(g + 2) * SUBLANES], elog)
    emax, _ = _first_max_row(elog, valid, row)
    ex = jnp.where(valid, jnp.exp(elog - emax), 0.0)
    p_exp = ex / jnp.sum(ex, axis=0, keepdims=True)
    v1, i1 = _first_max_row(p_exp, valid, row)
    v2, i2 = _first_max_row(p_exp, valid & (row != i1), row)
    tot = v1 + v2
    w = jnp.where(row == i1, p_gsel * (v1 / tot), jnp.where(row == i2, p_gsel * (v2 / tot), 0.0))
    onehot = jnp.where(row == g_star, 1.0, 0.0).astype(BF16)
    cb = min(n, CUMSUM_BLOCK)
    r_i = lax.broadcasted_iota(jnp.int32, (cb, cb), 0)
    c_i = lax.broadcasted_iota(jnp.int32, (cb, cb), 1)
    tri = jnp.where(r_i <= c_i, 1.0, 0.0).astype(BF16)
    pieces = []
    for j in range(n // cb):
        piece = _dot(onehot[:, j * cb:(j + 1) * cb], tri) + run
        pieces.append(piece)
        run = piece[:, cb - 1:cb]
    cum = jnp.concatenate(pieces, axis=1)
    rank = jnp.sum(jnp.where(row == g_star, cum, 0.0), axis=0, keepdims=True) - 1.0
    g_f = g_star.astype(F32)
    meta_t = jnp.where(valid, w, jnp.where(row == LANE_GROUP, g_f, jnp.where(
        row == LANE_RANK, rank, jnp.where(row == LANE_CODE, rank * N_GROUPS + g_f, 0.0))))
    return meta_t, cum[:, n - 1:n]


def _xattn_body(x_ref, g_ref, wq_t_ref, k_ref, v_t_ref, wo_ref, gm_ref, wr_t_ref, br_t_ref,
                xa_ref, cnt_ref, code_ref, run_ref, *, row_blocks):
    @pl.when((pl.program_id(0) == 0) & (pl.program_id(1) == 0))
    def _():
        run_ref[...] = jnp.zeros_like(run_ref)

    tx = x_ref.shape[0]
    hb = tx // row_blocks
    scale = XA_HEAD_DIM ** -0.5

    def attend(r0):
        rows = slice(r0, r0 + hb)
        x = x_ref[rows, :]
        q_t = _dot_nt(wq_t_ref[...], _rms(x, g_ref[...]).astype(BF16)).astype(BF16)
        yield
        head_rows = [slice(h * XA_HEAD_DIM, (h + 1) * XA_HEAD_DIM) for h in range(XA_HEADS)]
        scores = [_dot(k_ref[:, sl], q_t[sl, :]) * scale for sl in head_rows]
        yield
        heads = []
        for sl, s_t in zip(head_rows, scores):
            p = jnp.exp(s_t - jnp.max(s_t, axis=0, keepdims=True))
            l = jnp.sum(p, axis=0, keepdims=True)
            heads.append((_dot(v_t_ref[sl, :], p.astype(BF16)) / l).astype(BF16))
            yield
        x2 = x + _dot_tn(jnp.concatenate(heads, axis=0), wo_ref[...])
        xa_ref[rows, :D_MODEL] = x2
        yield
        hm = _rms(x2, gm_ref[...]).astype(BF16)
        logits[r0] = _dot_nt(wr_t_ref[...], hm) + br_t_ref[...]
        yield

    logits = {}
    blocks = [attend(r0) for r0 in range(0, tx, hb)]
    for _ in range(XA_HEADS + 4):
        for blk in blocks:
            next(blk)
    run = run_ref[...]
    for r0 in range(0, tx, hb):
        meta_t, run = _route(logits[r0], run)
        code_ref[:, r0:r0 + hb] = meta_t.astype(jnp.int32)
        xa_ref[r0:r0 + hb, D_MODEL:] = jnp.concatenate(
            [meta_t, jnp.zeros((META_LANES - CODE_ROWS, hb), F32)], axis=0).T
    run_ref[...] = run
    cnt_ref[...] = jnp.broadcast_to(run, cnt_ref.shape).astype(jnp.int32)


def _xattn_route(x, g, wq_t, kx, vx_t, wo, gm, wr_t, br_t, *, tx, row_blocks):
    B, T, D = x.shape
    row = lambda b, i: (b, i, 0)
    return pl.pallas_call(
        functools.partial(_xattn_body, row_blocks=row_blocks),
        grid=(B, T // tx),
        in_specs=[pl.BlockSpec((None, tx, D), row), _const_spec((1, D)),
                  _const_spec(wq_t.shape),
                  pl.BlockSpec((None, MEM_LEN, XA_DIM), lambda b, i: (b, 0, 0)),
                  pl.BlockSpec((None, XA_DIM, MEM_LEN), lambda b, i: (b, 0, 0)),
                  _const_spec(wo.shape), _const_spec((1, D)),
                  _const_spec(wr_t.shape), _const_spec(br_t.shape)],
        out_specs=[pl.BlockSpec((None, tx, XA_WIDTH), row), _const_spec((CODE_ROWS, META_LANES)),
                   pl.BlockSpec((None, None, CODE_ROWS, tx), lambda b, i: (b, i, 0, 0))],
        out_shape=[jax.ShapeDtypeStruct((B, T, XA_WIDTH), F32),
                   jax.ShapeDtypeStruct((CODE_ROWS, META_LANES), jnp.int32),
                   jax.ShapeDtypeStruct((B, T // tx, CODE_ROWS, tx), jnp.int32)],
        scratch_shapes=[pltpu.VMEM((CODE_ROWS, 1), F32)],
        compiler_params=pltpu.CompilerParams(
            dimension_semantics=("arbitrary", "arbitrary"), vmem_limit_bytes=VMEM_LIMIT),
        name="xattn_route",
    )(x, g, wq_t, kx, vx_t, wo, gm, wr_t, br_t)


VISIT_FIRST, VISIT_LAST, VISIT_VALID = 1, 2, 4


def _moe_plan_body(code_ref, cnt_ref, inv_ref, vt_ref, vg_ref, vf_ref, offs_ref, *, ts):
    n = code_ref.shape[0]
    nt, nv = n // ts, vt_ref.shape[0]
    ends = [cnt_ref[0, 0]]
    for g in range(1, N_GROUPS - 1):
        ends.append(ends[-1] + cnt_ref[g, 0])
    offs_ref[0] = 0
    for g in range(1, N_GROUPS):
        offs_ref[g] = ends[g - 1]
    group_bits = N_GROUPS.bit_length() - 1
    assert 1 << group_bits == N_GROUPS

    def place(t, _):
        code = code_ref[t]
        slot = offs_ref[code & (N_GROUPS - 1)] + lax.shift_right_logical(code, group_bits)
        inv_ref[slot] = t
        return 0

    lax.fori_loop(0, n, place, 0, unroll=8)

    def group_of(slot):
        return sum(jnp.where(e <= slot, 1, 0) for e in ends)

    def tile(j, v):
        g_lo, g_hi = group_of(j * ts), group_of(j * ts + ts - 1)
        for k in range(N_GROUPS):
            @pl.when(k <= g_hi - g_lo)
            def _():
                vt_ref[v + k] = j
                vg_ref[v + k] = g_lo + k
                vf_ref[v + k] = (VISIT_VALID + (VISIT_FIRST if k == 0 else 0)
                                 + jnp.where(k == g_hi - g_lo, VISIT_LAST, 0))
        return v + g_hi - g_lo + 1

    used = lax.fori_loop(0, nt, tile, 0)
    g_last = group_of(n - 1)

    def pad(v, _):
        vt_ref[v] = nt - 1
        vg_ref[v] = g_last
        vf_ref[v] = 0
        return 0

    lax.fori_loop(used, nv, pad, 0)


def _moe_plan(code, cnt, *, ts):
    n = code.shape[0]
    nt = n // ts
    nv = nt + N_GROUPS - 1
    smem = pl.BlockSpec(memory_space=pltpu.SMEM)
    visits = jax.ShapeDtypeStruct((nv,), jnp.int32)
    inv, vt, vg, vf = pl.pallas_call(
        functools.partial(_moe_plan_body, ts=ts),
        in_specs=[smem, smem],
        out_specs=[smem, smem, smem, smem],
        out_shape=[jax.ShapeDtypeStruct((n,), jnp.int32), visits, visits, visits],
        scratch_shapes=[pltpu.SMEM((N_GROUPS,), jnp.int32)],
        name="moe_plan",
    )(code, cnt)
    return inv.reshape(nt, 1, ts), vt, vg, vf


def _moe_body(vt_ref, vg_ref, vf_ref, inv_cur_ref, inv_nxt_ref, inv_prv_ref, xa_hbm, gm_ref,
              wg32_ref, wu32_ref, wd32_ref, gf_ref, out_hbm, xbuf, obuf, hm_ref, yacc_ref,
              wg_ref, wu_ref, wd_ref, gsem, ssem, *, ts, nt, sub):
    v = pl.program_id(0)
    t, grp, flags = vt_ref[v], vg_ref[v], vf_ref[v]
    slot = lax.rem(t, 2)
    first = (flags & VISIT_FIRST) != 0
    overlapped = first & (t >= 1)

    def gather_rows(idx_ref, dst_slot, lo, hi):
        for r in range(lo, min(hi, ts)):
            pltpu.make_async_copy(xa_hbm.at[pl.ds(idx_ref[0, r], 1)],
                                  xbuf.at[dst_slot, pl.ds(r, 1)], gsem.at[dst_slot]).start()

    def gather_wait(dst_slot):
        pltpu.make_async_copy(xa_hbm.at[pl.ds(0, ts)], xbuf.at[dst_slot], gsem.at[dst_slot]).wait()

    def scatter_rows(idx_ref, src_slot, lo, hi):
        for r in range(lo, min(hi, ts)):
            pltpu.make_async_copy(obuf.at[src_slot, pl.ds(r, 1)],
                                  out_hbm.at[pl.ds(idx_ref[0, r], 1)], ssem.at[src_slot]).start()

    def scatter_wait(src_slot):
        pltpu.make_async_copy(obuf.at[src_slot], out_hbm.at[pl.ds(0, ts)], ssem.at[src_slot]).wait()

    def experts(issue_dma):
        lane = lax.broadcasted_iota(jnp.int32, (sub, META_LANES), 1)
        n_sub = ts // sub
        dots_per_sub = 3 * EXPERTS_PER_GROUP
        batch = -(-ts // (dots_per_sub * max(n_sub // 2, 1)))
        done = [0, 0]

        def after_dot(r):
            if not issue_dma:
                return
            kind = 0 if r < max(n_sub // 2, 1) else 1
            lo = done[kind]
            if kind == 0:
                gather_rows(inv_nxt_ref, 1 - slot, lo, lo + batch)
            else:
                scatter_rows(inv_prv_ref, 1 - slot, lo, lo + batch)
            done[kind] = min(lo + batch, ts)

        for r in range(n_sub):
            rows = slice(r * sub, (r + 1) * sub)
            meta = xbuf[slot, rows, D_MODEL:]
            row_grp = jnp.sum(jnp.where(lane == LANE_GROUP, meta, 0.0), axis=-1, keepdims=True)
            member = row_grp == grp.astype(F32)
            hm = hm_ref[rows, :]
            y = jnp.zeros((sub, D_MODEL), F32)
            for e in range(EXPERTS_PER_GROUP):
                w = jnp.sum(jnp.where(lane == e, meta, 0.0), axis=-1, keepdims=True)
                w = jnp.where(member, w, 0.0)
                a = _dot(hm, wg_ref[e])
                after_dot(r)
                b = _dot(hm, wu_ref[e])
                after_dot(r)
                u = (a * _sigmoid(a)) * b * w
                y = y + _dot(u.astype(BF16), wd_ref[e])
                after_dot(r)
            yacc_ref[rows, :] += y
        if issue_dma:
            assert done == [ts, ts], done

    @pl.when(v == 0)
    def _():
        gather_rows(inv_cur_ref, 0, 0, ts)

    @pl.when((v == 0) | (grp != vg_ref[jnp.maximum(v - 1, 0)]))
    def _():
        wg_ref[...] = wg32_ref[...].astype(BF16)
        wu_ref[...] = wu32_ref[...].astype(BF16)
        wd_ref[...] = wd32_ref[...].astype(BF16)

    @pl.when(first)
    def _():
        gather_wait(slot)

        @pl.when(t == 0)
        def _():
            gather_rows(inv_nxt_ref, 1, 0, ts)

        hm_ref[...] = _rms(xbuf[slot, :, :D_MODEL], gm_ref[...]).astype(BF16)
        yacc_ref[...] = jnp.zeros_like(yacc_ref)

    @pl.when(overlapped)
    def _():
        experts(True)

    @pl.when(((flags & VISIT_VALID) != 0) & jnp.logical_not(overlapped))
    def _():
        experts(False)

    @pl.when((flags & VISIT_LAST) != 0)
    def _():
        @pl.when(t >= 2)
        def _():
            scatter_wait(slot)

        obuf[slot] = _rms(xbuf[slot, :, :D_MODEL] + yacc_ref[...], gf_ref[...])

    @pl.when(v == pl.num_programs(0) - 1)
    def _():
        last_slot = (nt - 1) % 2
        scatter_rows(inv_cur_ref, last_slot, 0, ts)
        scatter_wait(0)
        scatter_wait(1)
        gather_wait(1 - last_slot)


def _moe_sorted(xa, inv, vt, vg, vf, gm, wg, wu, wd, gf, *, ts, sub):
    N = xa.shape[0]
    nt = N // ts
    assert nt >= 2
    E, D, FF = EXPERTS_PER_GROUP, D_MODEL, EXPERT_FF
    grp = lambda v, vt, vg, vf: (vg[v], 0, 0)
    const = lambda v, vt, vg, vf: (0, 0)
    smem_row = lambda fn: pl.BlockSpec((None, 1, ts), fn, memory_space=pltpu.SMEM)
    grid_spec = pltpu.PrefetchScalarGridSpec(
        num_scalar_prefetch=3,
        grid=(vt.shape[0],),
        in_specs=[
            smem_row(lambda v, vt, vg, vf: (vt[v], 0, 0)),
            smem_row(lambda v, vt, vg, vf: (jnp.minimum(vt[v] + 1, nt - 1), 0, 0)),
            smem_row(lambda v, vt, vg, vf: (jnp.maximum(vt[v] - 1, 0), 0, 0)),
            pl.BlockSpec(memory_space=pl.ANY),
            pl.BlockSpec((1, D), const),
            pl.BlockSpec((E, D, FF), grp), pl.BlockSpec((E, D, FF), grp),
            pl.BlockSpec((E, FF, D), grp),
            pl.BlockSpec((1, D), const),
        ],
        out_specs=pl.BlockSpec(memory_space=pl.ANY),
        scratch_shapes=[
            pltpu.VMEM((2, ts, XA_WIDTH), F32), pltpu.VMEM((2, ts, D), F32),
            pltpu.VMEM((ts, D), BF16), pltpu.VMEM((ts, D), F32),
            pltpu.VMEM((E, D, FF), BF16), pltpu.VMEM((E, D, FF), BF16), pltpu.VMEM((E, FF, D), BF16),
            pltpu.SemaphoreType.DMA((2,)), pltpu.SemaphoreType.DMA((2,)),
        ],
    )
    return pl.pallas_call(
        functools.partial(_moe_body, ts=ts, nt=nt, sub=sub),
        grid_spec=grid_spec,
        out_shape=jax.ShapeDtypeStruct((N, D), F32),
        compiler_params=pltpu.CompilerParams(
            dimension_semantics=("arbitrary",), vmem_limit_bytes=VMEM_LIMIT),
        name="moe_sorted",
    )(vt, vg, vf, inv, inv, inv, xa, gm, wg, wu, wd, gf)


def kernel(x, mem, norm_mix_g, w_in, fox_bf, conv_dw_w, conv_dw_b, conv_ln_g, conv_ln_b, conv_pw_w, attn_branch_w, w_out, norm_xa_g, norm_mem_g, xa_wq, xa_wk, xa_wv, xa_wo, norm_moe_g, router_group_w, router_group_b, router_expert_w, router_expert_b, expert_w_gate, expert_w_up, expert_w_down, norm_final_g):
    B, T, D = x.shape
    depth = w_in.shape[0]
    assert depth == 1, "the final norm is fused into the last layer's MoE kernel"
    row = lambda v: v.reshape(1, -1).astype(F32)
    for l in range(depth):
        u, gc, ga, qkv_t, f_t = _proj_in(x, row(norm_mix_g[l]), w_in[l].astype(F32), tm=512)
        qaux, kaux = _fox_prep(f_t, fox_bf[l].reshape(FOX_HEADS, 1).astype(F32))
        attn_t = _fox_attn(qkv_t, qaux, kaux, tq=1024)
        x = _mix_out(x, u, gc, ga, attn_t, conv_dw_w[l].astype(F32), row(conv_dw_b[l]),
                     row(conv_ln_g[l]), row(conv_ln_b[l]), conv_pw_w[l].astype(F32),
                     attn_branch_w[l].astype(F32), w_out[l].astype(F32), tc=512, rb=512)
        kx, vx_t = _mem_kv(mem, row(norm_mem_g[l]), xa_wk[l].astype(BF16),
                           xa_wv[l].T.astype(BF16))
        wr_t = jnp.zeros((ROUTE_ROWS, D), F32).at[:N_GROUPS].set(router_group_w[l].T)
        br_t = jnp.zeros((ROUTE_ROWS, 1), F32).at[:N_GROUPS, 0].set(router_group_b[l])
        for g in range(N_GROUPS):
            r0, e0 = SUBLANES * (g + 1), EXPERTS_PER_GROUP * g
            wr_t = wr_t.at[r0:r0 + EXPERTS_PER_GROUP].set(
                router_expert_w[l][:, e0:e0 + EXPERTS_PER_GROUP].T)
            br_t = br_t.at[r0:r0 + EXPERTS_PER_GROUP, 0].set(
                router_expert_b[l][e0:e0 + EXPERTS_PER_GROUP])
        xa, cnt, code = _xattn_route(x, row(norm_xa_g[l]), xa_wq[l].T.astype(BF16), kx, vx_t,
                                     xa_wo[l].astype(BF16), row(norm_moe_g[l]),
                                     wr_t.astype(BF16), br_t, tx=1024, row_blocks=1)
        xa = xa.reshape(B * T, XA_WIDTH)
        inv, vt, vg, vf = _moe_plan(code[:, :, LANE_CODE, :].reshape(B * T), cnt, ts=512)
        x = _moe_sorted(xa, inv, vt, vg, vf, row(norm_moe_g[l]), expert_w_gate[l].astype(F32),
                        expert_w_up[l].astype(F32), expert_w_down[l].astype(F32),
                        row(norm_final_g), ts=512, sub=256).reshape(B, T, D)
    return x
```

```python
import functools

import jax
import jax.numpy as jnp
from jax import lax
from jax.experimental import pallas as pl
from jax.experimental.pallas import tpu as pltpu

D_MODEL = 1024
MEM_LEN = 256
EPS = 1e-6
CONV_DIM = 512
CONV_WIDTH = 31
FOX_HEADS = 8
FOX_HEAD_DIM = 64
FOX_DIM = FOX_HEADS * FOX_HEAD_DIM
XA_HEADS = 4
XA_HEAD_DIM = 128
XA_DIM = XA_HEADS * XA_HEAD_DIM
N_GROUPS = 4
EXPERTS_PER_GROUP = 4
N_EXPERTS = N_GROUPS * EXPERTS_PER_GROUP
EXPERT_FF = 256

AUX_ROWS = 16
LOG2E = 1.4426950408889634
VMEM_LIMIT = 56 * 1024 * 1024

F32 = jnp.float32
BF16 = jnp.bfloat16


def _const_spec(shape):
    n = len(shape)
    return pl.BlockSpec(shape, lambda *_: (0,) * n)


def _rms(x, g):
    return x * lax.rsqrt(jnp.mean(x * x, axis=-1, keepdims=True) + EPS) * g


def _sigmoid(x):
    return 0.5 * jnp.tanh(0.5 * x) + 0.5


def _dot(a, b):
    return jnp.dot(a, b, preferred_element_type=F32)


def _dot_nt(a, b):
    return lax.dot_general(a, b, (((1,), (1,)), ((), ())), preferred_element_type=F32)


def _dot_tn(a, b):
    return lax.dot_general(a, b, (((0,), (0,)), ((), ())), preferred_element_type=F32)


C_GLU = 2 * CONV_DIM
C_F = C_GLU + 3 * FOX_DIM
C_GC = C_F + FOX_HEADS
C_GA = C_GC + D_MODEL


def _proj_in_body(x_ref, g_ref, w_ref, u_ref, gc_ref, ga_ref, qkv_t_ref, f_t_ref,
                  wglu_ref, wgc_ref, wga_ref, wqkv_t_ref, wf_t_ref):
    @pl.when((pl.program_id(0) == 0) & (pl.program_id(1) == 0))
    def _():
        for j in range(C_GLU // D_MODEL):
            rows = slice(j * D_MODEL, (j + 1) * D_MODEL)
            wglu_ref[:, rows] = w_ref[rows, :].T.astype(BF16)
        wgc_ref[...] = w_ref[C_GC:C_GA, :].T.astype(BF16)
        wga_ref[...] = w_ref[C_GA:, :].T.astype(BF16)
        wqkv_t_ref[...] = w_ref[C_GLU:C_F, :].astype(BF16)
        wf_t_ref[...] = w_ref[C_F:C_GC, :].astype(BF16)

    h = _rms(x_ref[...], g_ref[...]).astype(BF16)
    glu = _dot(h, wglu_ref[...])
    u_ref[...] = glu[:, :CONV_DIM] * _sigmoid(glu[:, CONV_DIM:])
    gc_ref[...] = _dot(h, wgc_ref[...]).astype(BF16)
    ga_ref[...] = _dot(h, wga_ref[...]).astype(BF16)
    qkv_t = _dot_nt(wqkv_t_ref[...], h)
    scale = FOX_HEAD_DIM ** -0.5 * LOG2E
    qkv_t_ref[:FOX_DIM, :] = (qkv_t[:FOX_DIM] * scale).astype(BF16)
    qkv_t_ref[FOX_DIM:, :] = qkv_t[FOX_DIM:].astype(BF16)
    f_t_ref[...] = _dot_nt(wf_t_ref[...], h)


def _proj_in(x, g, w, *, layer, tm):
    B, T, D = x.shape
    grid = (B, T // tm)
    row = lambda b, i: (b, i, 0)
    col = lambda b, i: (b, 0, i)
    return pl.pallas_call(
        _proj_in_body,
        grid=grid,
        in_specs=[
            pl.BlockSpec((None, tm, D), row),
            _const_spec((1, D)),
            pl.BlockSpec((None,) + w.shape[1:], lambda b, i: (layer, 0, 0),
                         pipeline_mode=pl.Buffered(1)),
        ],
        out_specs=[
            pl.BlockSpec((None, tm, CONV_DIM), row),
            pl.BlockSpec((None, tm, D), row),
            pl.BlockSpec((None, tm, D), row),
            pl.BlockSpec((None, 3 * FOX_DIM, tm), col),
            pl.BlockSpec((None, FOX_HEADS, tm), col),
        ],
        out_shape=[
            jax.ShapeDtypeStruct((B, T, CONV_DIM), F32),
            jax.ShapeDtypeStruct((B, T, D), BF16),
            jax.ShapeDtypeStruct((B, T, D), BF16),
            jax.ShapeDtypeStruct((B, 3 * FOX_DIM, T), BF16),
            jax.ShapeDtypeStruct((B, FOX_HEADS, T), F32),
        ],
        scratch_shapes=[pltpu.VMEM((D, C_GLU), BF16), pltpu.VMEM((D, D), BF16),
                        pltpu.VMEM((D, D), BF16), pltpu.VMEM((3 * FOX_DIM, D), BF16),
                        pltpu.VMEM((FOX_HEADS, D), BF16)],
        compiler_params=pltpu.CompilerParams(
            dimension_semantics=("arbitrary", "arbitrary"), vmem_limit_bytes=VMEM_LIMIT),
        name="proj_in",
    )(x, g, w)


def _split3(c):
    hi = c.astype(BF16)
    r = c - hi.astype(F32)
    mid = r.astype(BF16)
    lo = (r - mid.astype(F32)).astype(BF16)
    return hi, mid, lo


def _fox_prep_body(f_t_ref, bf_ref, qaux_ref, kaux_ref, c_ref, *, chunk):
    T = f_t_ref.shape[-1]
    z = f_t_ref[...] + bf_ref[...]
    logf = jnp.minimum(z, 0.0) - jnp.log1p(jnp.exp(-jnp.abs(z)))
    r_i = lax.broadcasted_iota(jnp.int32, (chunk, chunk), 0)
    c_i = lax.broadcasted_iota(jnp.int32, (chunk, chunk), 1)
    tri = jnp.where(r_i <= c_i, 1.0, 0.0).astype(BF16)
    carry = jnp.zeros((FOX_HEADS, 1), F32)
    for j in range(T // chunk):
        blk = logf[:, j * chunk:(j + 1) * chunk]
        hi, mid, lo = _split3(blk)
        cs = (_dot(lo, tri) + _dot(mid, tri)) + _dot(hi, tri) + carry
        c_ref[:, j * chunk:(j + 1) * chunk] = cs
        carry = cs[:, chunk - 1:chunk]
    row = lax.broadcasted_iota(jnp.int32, (AUX_ROWS, T), 0)
    for h in range(FOX_HEADS):
        hi, mid, lo = _split3(c_ref[h:h + 1, :] * LOG2E)
        pieces = [jnp.broadcast_to(p.astype(F32), (AUX_ROWS, T)) for p in (hi, mid, lo)]
        one = jnp.where(row < 3, 1.0, 0.0)
        val = jnp.where(row == 0, pieces[0], jnp.where(row == 1, pieces[1], pieces[2]))
        val_k = jnp.where(row == 3, pieces[0], jnp.where(row == 4, pieces[1], pieces[2]))
        qaux = jnp.where(row < 3, val, jnp.where(row < 6, 1.0, 0.0))
        kaux = jnp.where(row < 3, one, jnp.where(row < 6, -val_k, 0.0))
        qaux_ref[h] = qaux.astype(BF16)
        kaux_ref[h] = kaux.astype(BF16)


def _fox_prep(f_t, bf):
    B, H, T = f_t.shape
    aux = jax.ShapeDtypeStruct((B, H, AUX_ROWS, T), BF16)
    aux_spec = pl.BlockSpec((None, H, AUX_ROWS, T), lambda b: (b, 0, 0, 0))
    return pl.pallas_call(
        functools.partial(_fox_prep_body, chunk=512),
        grid=(B,),
        in_specs=[pl.BlockSpec((None, H, T), lambda b: (b, 0, 0)), _const_spec((H, 1))],
        out_specs=[aux_spec, aux_spec],
        out_shape=[aux, aux],
        scratch_shapes=[pltpu.VMEM((H, T), F32)],
        compiler_params=pltpu.CompilerParams(dimension_semantics=("parallel",)),
        name="fox_prep",
    )(f_t, bf)


MASK_BLOCK = 256


def _fox_attn_body(q_ref, k_ref, v_ref, qaux_ref, kaux_ref, o_ref, sb0_ref, sb1_ref, p_ref, *, tq):
    T = q_ref.shape[-1]
    dh = FOX_HEAD_DIM
    mb = MASK_BLOCK
    nj = tq // mb
    kk = lax.broadcasted_iota(jnp.int32, (mb, mb), 0)
    qq = lax.broadcasted_iota(jnp.int32, (mb, mb), 1)
    causal = kk <= qq
    ones = jnp.where(lax.broadcasted_iota(jnp.int32, (AUX_ROWS, tq), 0) == 0, 1.0, 0.0).astype(BF16)
    sbufs = (sb0_ref, sb1_ref)

    tasks = []
    for qi in range(T // tq):
        tasks.append((qi, qi, True))
        tasks += [(qi, c, False) for c in range(qi)]

    def scores(task, j, sb, mc):
        qi, c, diag = task
        c0 = j * mb if diag else 0
        r0 = c * tq + j * mb
        ka = jnp.concatenate([k_ref[:, r0:r0 + mb], kaux_ref[:, r0:r0 + mb]], axis=0)
        qa = jnp.concatenate([q_ref[:, qi * tq + c0:(qi + 1) * tq],
                              qaux_ref[:, qi * tq + c0:(qi + 1) * tq]], axis=0)
        s = _dot_tn(ka, qa)
        if diag:
            head = jnp.where(causal, s[:, :mb], -jnp.inf)
            s = head if tq - c0 == mb else jnp.concatenate([head, s[:, mb:]], axis=1)
        sb[j * mb:(j + 1) * mb, c0:] = s
        cm = jnp.max(s, axis=0, keepdims=True)
        if c0:
            cm = jnp.concatenate([jnp.full((1, c0), -jnp.inf, F32), cm], axis=1)
        return cm if mc is None else jnp.maximum(mc, cm)

    def probs(task, j, sb, m_new):
        _, _, diag = task
        c0 = j * mb if diag else 0
        rows = slice(j * mb, (j + 1) * mb)
        p_ref[rows, c0:] = jnp.exp2(sb[rows, c0:] - m_new[:, c0:]).astype(BF16)

    def values(task):
        _, c, diag = task
        va = jnp.concatenate([v_ref[:, c * tq:(c + 1) * tq], ones], axis=0)
        if not diag:
            return _dot(va, p_ref[...])
        return jnp.concatenate(
            [_dot(va[:, :(bj + 1) * mb], p_ref[:(bj + 1) * mb, bj * mb:(bj + 1) * mb])
             for bj in range(nj)], axis=1)

    mc = None
    for j in range(nj):
        mc = scores(tasks[0], j, sbufs[0], mc)
    m_run = acc = None
    for k, task in enumerate(tasks):
        qi, c, diag = task
        nxt = tasks[k + 1] if k + 1 < len(tasks) else None
        m_new = mc if diag else jnp.maximum(m_run, mc)
        mc = None
        for j in range(nj):
            if nxt is not None:
                mc = scores(nxt, j, sbufs[(k + 1) % 2], mc)
            probs(task, j, sbufs[k % 2], m_new)
        pv = values(task)
        acc = pv if diag else jnp.exp2(m_run - m_new) * acc + pv
        m_run = m_new
        if nxt is None or nxt[0] != qi:
            o_ref[:, qi * tq:(qi + 1) * tq] = (acc[:dh] / acc[dh:dh + 1]).astype(o_ref.dtype)


def _fox_attn(qkv_t, qaux, kaux, *, tq):
    B, _, T = qkv_t.shape
    H, dh = FOX_HEADS, FOX_HEAD_DIM
    part = lambda off: pl.BlockSpec((None, dh, T), lambda b, h: (b, off * H + h, 0))
    aux = pl.BlockSpec((None, None, AUX_ROWS, T), lambda b, h: (b, h, 0, 0))
    return pl.pallas_call(
        functools.partial(_fox_attn_body, tq=tq),
        grid=(B, H),
        in_specs=[part(0), part(1), part(2), aux, aux],
        out_specs=pl.BlockSpec((None, dh, T), lambda b, h: (b, h, 0)),
        out_shape=jax.ShapeDtypeStruct((B, H * dh, T), BF16),
        scratch_shapes=[pltpu.VMEM((tq, tq), F32), pltpu.VMEM((tq, tq), F32),
                        pltpu.VMEM((tq, tq), BF16)],
        compiler_params=pltpu.CompilerParams(
            dimension_semantics=("parallel", "parallel"), vmem_limit_bytes=VMEM_LIMIT),
        name="fox_attn",
    )(qkv_t, qkv_t, qkv_t, qaux, kaux)


HALO = 32
SUBLANES = 8


def _mix_out_body(x_ref, u_ref, uprev_ref, gc_ref, ga_ref, attn_t_ref, dww_ref, dwb_ref,
                  lng_ref, lnb_ref, pw32_ref, abw32_ref, wout32_ref, o_ref, ubuf_ref, ushift_ref,
                  pw_ref, abw_ref, wout_ref, *, tc, rb):
    first = pl.program_id(1) == 0

    @pl.when(first & (pl.program_id(0) == 0))
    def _():
        pw_ref[...] = pw32_ref[...].astype(BF16)
        abw_ref[...] = abw32_ref[...].astype(BF16)
        wout_ref[...] = wout32_ref[...].astype(BF16)

    ubuf_ref[:HALO, :] = jnp.where(first, 0.0, uprev_ref[...])
    ubuf_ref[HALO:, :] = u_ref[...]
    span = tc + HALO - SUBLANES
    for s in range(1, SUBLANES):
        ushift_ref[s - 1, :, :] = ubuf_ref[s:s + span, :]

    def conv_branch(r0):
        acc = jnp.zeros((rb, CONV_DIM), F32)
        for k in range(CONV_WIDTH):
            off = r0 + HALO - (CONV_WIDTH - 1) + k
            s, base = off % SUBLANES, off - off % SUBLANES
            rows = ubuf_ref[base:base + rb, :] if s == 0 else ushift_ref[s - 1, base:base + rb, :]
            acc = acc + rows * dww_ref[k:k + 1, :]
        acc = acc + dwb_ref[...]
        mu = jnp.mean(acc, axis=-1, keepdims=True)
        d = acc - mu
        var = jnp.mean(d * d, axis=-1, keepdims=True)
        y = d * lax.rsqrt(var + EPS) * lng_ref[...] + lnb_ref[...]
        return (y * _sigmoid(y)).astype(BF16)

    def project(r0, y):
        rows = slice(r0, r0 + rb)
        conv_out = _dot(y, pw_ref[...])
        attn_out = _dot_tn(attn_t_ref[:, rows], abw_ref[...])
        merged = (_sigmoid(gc_ref[rows, :].astype(F32)) * conv_out
                  + _sigmoid(ga_ref[rows, :].astype(F32)) * attn_out)
        o_ref[rows, :] = x_ref[rows, :] + _dot(merged.astype(BF16), wout_ref[...])

    nb = tc // rb
    y = conv_branch(0)
    for j in range(nb):
        y_next = conv_branch((j + 1) * rb) if j + 1 < nb else None
        project(j * rb, y)
        y = y_next


def _mix_out(x, u, gc, ga, attn_t, dww, dwb, lng, lnb, pw, abw, wout, *, tc, rb):
    B, T, D = x.shape
    row = lambda b, i: (b, i, 0)
    halo_blocks = tc // HALO
    return pl.pallas_call(
        functools.partial(_mix_out_body, tc=tc, rb=rb),
        grid=(B, T // tc),
        in_specs=[
            pl.BlockSpec((None, tc, D), row),
            pl.BlockSpec((None, tc, CONV_DIM), row),
            pl.BlockSpec((None, HALO, CONV_DIM),
                         lambda b, i: (b, jnp.maximum(i * halo_blocks - 1, 0), 0)),
            pl.BlockSpec((None, tc, D), row),
            pl.BlockSpec((None, tc, D), row),
            pl.BlockSpec((None, FOX_DIM, tc), lambda b, i: (b, 0, i)),
            _const_spec(dww.shape), _const_spec(dwb.shape), _const_spec(lng.shape),
            _const_spec(lnb.shape), _const_spec(pw.shape), _const_spec(abw.shape),
            _const_spec(wout.shape),
        ],
        out_specs=pl.BlockSpec((None, tc, D), row),
        out_shape=jax.ShapeDtypeStruct((B, T, D), F32),
        scratch_shapes=[pltpu.VMEM((HALO + tc, CONV_DIM), F32),
                        pltpu.VMEM((SUBLANES - 1, HALO + tc - SUBLANES, CONV_DIM), F32),
                        pltpu.VMEM(pw.shape, BF16), pltpu.VMEM(abw.shape, BF16),
                        pltpu.VMEM(wout.shape, BF16)],
        compiler_params=pltpu.CompilerParams(
            dimension_semantics=("arbitrary", "arbitrary"), vmem_limit_bytes=VMEM_LIMIT),
        name="mix_out",
    )(x, u, u, gc, ga, attn_t, dww, dwb, lng, lnb, pw, abw, wout)


def _mem_kv_body(m_ref, g_ref, wk_ref, wv_t_ref, k_ref, v_t_ref):
    h = _rms(m_ref[...], g_ref[...]).astype(BF16)
    k_ref[...] = _dot(h, wk_ref[...]).astype(BF16)
    v_t_ref[...] = _dot_nt(wv_t_ref[...], h).astype(BF16)


def _mem_kv(mem, g, wk, wv_t):
    B, M, D = mem.shape
    return pl.pallas_call(
        _mem_kv_body,
        grid=(B,),
        in_specs=[pl.BlockSpec((None, M, D), lambda b: (b, 0, 0)), _const_spec((1, D)),
                  _const_spec(wk.shape), _const_spec(wv_t.shape)],
        out_specs=[pl.BlockSpec((None, M, XA_DIM), lambda b: (b, 0, 0)),
                   pl.BlockSpec((None, XA_DIM, M), lambda b: (b, 0, 0))],
        out_shape=[jax.ShapeDtypeStruct((B, M, XA_DIM), BF16),
                   jax.ShapeDtypeStruct((B, XA_DIM, M), BF16)],
        compiler_params=pltpu.CompilerParams(dimension_semantics=("parallel",)),
        name="mem_kv",
    )(mem, g, wk, wv_t)


META_LANES = 128
XA_WIDTH = D_MODEL + META_LANES
LANE_GROUP = EXPERTS_PER_GROUP
LANE_RANK = EXPERTS_PER_GROUP + 1
LANE_CODE = EXPERTS_PER_GROUP + 2
CODE_ROWS = SUBLANES
ROUTE_ROWS = SUBLANES * (N_GROUPS + 1)
CUMSUM_BLOCK = 256


def _first_max_row(v, valid, row):
    neg = jnp.where(valid, v, -jnp.inf)
    mx = jnp.max(neg, axis=0, keepdims=True)
    idx = jnp.min(jnp.where(valid & (neg == mx), row, SUBLANES), axis=0, keepdims=True)
    return mx, idx


def _route(logits_t, run):
    n = logits_t.shape[1]
    row = lax.broadcasted_iota(jnp.int32, (SUBLANES, n), 0)
    valid = row < N_GROUPS
    glog = logits_t[:SUBLANES]
    gmax, g_star = _first_max_row(glog, valid, row)
    p_gsel = 1.0 / jnp.sum(jnp.where(valid, jnp.exp(glog - gmax), 0.0), axis=0, keepdims=True)
    elog = logits_t[SUBLANES:2 * SUBLANES]
    for g in range(1, N_GROUPS):
        elog = jnp.where(g_star == g, logits_t[(g + 1) * SUBLANES:(g + 2) * SUBLANES], elog)
    emax, _ = _first_max_row(elog, valid, row)
    ex = jnp.where(valid, jnp.exp(elog - emax), 0.0)
    p_exp = ex / jnp.sum(ex, axis=0, keepdims=True)
    v1, i1 = _first_max_row(p_exp, valid, row)
    v2, i2 = _first_max_row(p_exp, valid & (row != i1), row)
    tot = v1 + v2
    w = jnp.where(row == i1, p_gsel * (v1 / tot), jnp.where(row == i2, p_gsel * (v2 / tot), 0.0))
    onehot = jnp.where(row == g_star, 1.0, 0.0).astype(BF16)
    cb = min(n, CUMSUM_BLOCK)
    r_i = lax.broadcasted_iota(jnp.int32, (cb, cb), 0)
    c_i = lax.broadcasted_iota(jnp.int32, (cb, cb), 1)
    tri = jnp.where(r_i <= c_i, 1.0, 0.0).astype(BF16)
    pieces = []
    for j in range(n // cb):
        piece = _dot(onehot[:, j * cb:(j + 1) * cb], tri) + run
        pieces.append(piece)
        run = piece[:, cb - 1:cb]
    cum = jnp.concatenate(pieces, axis=1)
    rank = jnp.sum(jnp.where(row == g_star, cum, 0.0), axis=0, keepdims=True) - 1.0
    g_f = g_star.astype(F32)
    meta_t = jnp.where(valid, w, jnp.where(row == LANE_GROUP, g_f, jnp.where(
        row == LANE_RANK, rank, jnp.where(row == LANE_CODE, rank * N_GROUPS + g_f, 0.0))))
    return meta_t, cum[:, n - 1:n]


def _xattn_body(x_ref, g_ref, wq_t_ref, k_ref, v_t_ref, wo_ref, gm_ref, wr_t_ref, br_t_ref,
                xa_ref, cnt_ref, code_ref, run_ref, *, row_blocks):
    @pl.when((pl.program_id(0) == 0) & (pl.program_id(1) == 0))
    def _():
        run_ref[...] = jnp.zeros_like(run_ref)

    tx = x_ref.shape[0]
    hb = tx // row_blocks
    scale = XA_HEAD_DIM ** -0.5

    def attend(r0):
        rows = slice(r0, r0 + hb)
        x = x_ref[rows, :]
        q_t = _dot_nt(wq_t_ref[...], _rms(x, g_ref[...]).astype(BF16)).astype(BF16)
        yield
        head_rows = [slice(h * XA_HEAD_DIM, (h + 1) * XA_HEAD_DIM) for h in range(XA_HEADS)]
        scores = [_dot(k_ref[:, sl], q_t[sl, :]) * scale for sl in head_rows]
        yield
        heads = []
        for sl, s_t in zip(head_rows, scores):
            p = jnp.exp(s_t - jnp.max(s_t, axis=0, keepdims=True))
            l = jnp.sum(p, axis=0, keepdims=True)
            heads.append((_dot(v_t_ref[sl, :], p.astype(BF16)) / l).astype(BF16))
            yield
        x2 = x + _dot_tn(jnp.concatenate(heads, axis=0), wo_ref[...])
        xa_ref[rows, :D_MODEL] = x2
        yield
        hm = _rms(x2, gm_ref[...]).astype(BF16)
        logits[r0] = _dot_nt(wr_t_ref[...], hm) + br_t_ref[...]
        yield

    logits = {}
    blocks = [attend(r0) for r0 in range(0, tx, hb)]
    for _ in range(XA_HEADS + 4):
        for blk in blocks:
            next(blk)
    run = run_ref[...]
    for r0 in range(0, tx, hb):
        meta_t, run = _route(logits[r0], run)
        code_ref[:, r0:r0 + hb] = meta_t.astype(jnp.int32)
        xa_ref[r0:r0 + hb, D_MODEL:] = jnp.concatenate(
            [meta_t, jnp.zeros((META_LANES - CODE_ROWS, hb), F32)], axis=0).T
    run_ref[...] = run
    cnt_ref[...] = jnp.broadcast_to(run, cnt_ref.shape).astype(jnp.int32)


def _xattn_route(x, g, wq_t, kx, vx_t, wo, gm, wr_t, br_t, *, tx, row_blocks):
    B, T, D = x.shape
    row = lambda b, i: (b, i, 0)
    return pl.pallas_call(
        functools.partial(_xattn_body, row_blocks=row_blocks),
        grid=(B, T // tx),
        in_specs=[pl.BlockSpec((None, tx, D), row), _const_spec((1, D)),
                  _const_spec(wq_t.shape),
                  pl.BlockSpec((None, MEM_LEN, XA_DIM), lambda b, i: (b, 0, 0)),
                  pl.BlockSpec((None, XA_DIM, MEM_LEN), lambda b, i: (b, 0, 0)),
                  _const_spec(wo.shape), _const_spec((1, D)),
                  _const_spec(wr_t.shape), _const_spec(br_t.shape)],
        out_specs=[pl.BlockSpec((None, tx, XA_WIDTH), row), _const_spec((CODE_ROWS, META_LANES)),
                   pl.BlockSpec((None, None, CODE_ROWS, tx), lambda b, i: (b, i, 0, 0))],
        out_shape=[jax.ShapeDtypeStruct((B, T, XA_WIDTH), F32),
                   jax.ShapeDtypeStruct((CODE_ROWS, META_LANES), jnp.int32),
                   jax.ShapeDtypeStruct((B, T // tx, CODE_ROWS, tx), jnp.int32)],
        scratch_shapes=[pltpu.VMEM((CODE_ROWS, 1), F32)],
        compiler_params=pltpu.CompilerParams(
            dimension_semantics=("arbitrary", "arbitrary"), vmem_limit_bytes=VMEM_LIMIT),
        name="xattn_route",
    )(x, g, wq_t, kx, vx_t, wo, gm, wr_t, br_t)


VISIT_FIRST, VISIT_LAST, VISIT_VALID = 1, 2, 4


def _moe_plan_body(code_ref, cnt_ref, inv_ref, vt_ref, vg_ref, vf_ref, offs_ref, *, ts):
    n = code_ref.shape[0]
    nt, nv = n // ts, vt_ref.shape[0]
    ends = [cnt_ref[0, 0]]
    for g in range(1, N_GROUPS - 1):
        ends.append(ends[-1] + cnt_ref[g, 0])
    offs_ref[0] = 0
    for g in range(1, N_GROUPS):
        offs_ref[g] = ends[g - 1]
    group_bits = N_GROUPS.bit_length() - 1
    assert 1 << group_bits == N_GROUPS

    def place(t, _):
        code = code_ref[t]
        slot = offs_ref[code & (N_GROUPS - 1)] + lax.shift_right_logical(code, group_bits)
        inv_ref[slot] = t
        return 0

    lax.fori_loop(0, n, place, 0, unroll=8)

    def group_of(slot):
        return sum(jnp.where(e <= slot, 1, 0) for e in ends)

    def tile(j, v):
        g_lo, g_hi = group_of(j * ts), group_of(j * ts + ts - 1)
        for k in range(N_GROUPS):
            @pl.when(k <= g_hi - g_lo)
            def _():
                vt_ref[v + k] = j
                vg_ref[v + k] = g_lo + k
                vf_ref[v + k] = (VISIT_VALID + (VISIT_FIRST if k == 0 else 0)
                                 + jnp.where(k == g_hi - g_lo, VISIT_LAST, 0))
        return v + g_hi - g_lo + 1

    used = lax.fori_loop(0, nt, tile, 0)
    g_last = group_of(n - 1)

    def pad(v, _):
        vt_ref[v] = nt - 1
        vg_ref[v] = g_last
        vf_ref[v] = 0
        return 0

    lax.fori_loop(used, nv, pad, 0)


def _moe_plan(code, cnt, *, ts):
    n = code.shape[0]
    nt = n // ts
    nv = nt + N_GROUPS - 1
    smem = pl.BlockSpec(memory_space=pltpu.SMEM)
    visits = jax.ShapeDtypeStruct((nv,), jnp.int32)
    inv, vt, vg, vf = pl.pallas_call(
        functools.partial(_moe_plan_body, ts=ts),
        in_specs=[smem, smem],
        out_specs=[smem, smem, smem, smem],
        out_shape=[jax.ShapeDtypeStruct((n,), jnp.int32), visits, visits, visits],
        scratch_shapes=[pltpu.SMEM((N_GROUPS,), jnp.int32)],
        name="moe_plan",
    )(code, cnt)
    return inv.reshape(nt, 1, ts), vt, vg, vf


def _moe_body(vt_ref, vg_ref, vf_ref, inv_cur_ref, inv_nxt_ref, inv_prv_ref, xa_hbm, gm_ref,
              wg32_ref, wu32_ref, wd32_ref, gf_ref, out_hbm, xbuf, obuf, hm_ref, yacc_ref,
              wg_ref, wu_ref, wd_ref, gsem, ssem, *, ts, nt, sub):
    v = pl.program_id(0)
    t, grp, flags = vt_ref[v], vg_ref[v], vf_ref[v]
    slot = lax.rem(t, 2)
    first = (flags & VISIT_FIRST) != 0
    overlapped = first & (t >= 1)

    def gather_rows(idx_ref, dst_slot, lo, hi):
        for r in range(lo, min(hi, ts)):
            pltpu.make_async_copy(xa_hbm.at[pl.ds(idx_ref[0, r], 1)],
                                  xbuf.at[dst_slot, pl.ds(r, 1)], gsem.at[dst_slot]).start()

    def gather_wait(dst_slot):
        pltpu.make_async_copy(xa_hbm.at[pl.ds(0, ts)], xbuf.at[dst_slot], gsem.at[dst_slot]).wait()

    def scatter_rows(idx_ref, src_slot, lo, hi):
        for r in range(lo, min(hi, ts)):
            pltpu.make_async_copy(obuf.at[src_slot, pl.ds(r, 1)],
                                  out_hbm.at[pl.ds(idx_ref[0, r], 1)], ssem.at[src_slot]).start()

    def scatter_wait(src_slot):
        pltpu.make_async_copy(obuf.at[src_slot], out_hbm.at[pl.ds(0, ts)], ssem.at[src_slot]).wait()

    def experts(issue_dma):
        lane = lax.broadcasted_iota(jnp.int32, (sub, META_LANES), 1)
        n_sub = ts // sub
        dots_per_sub = 3 * EXPERTS_PER_GROUP
        batch = -(-ts // (dots_per_sub * max(n_sub // 2, 1)))
        done = [0, 0]

        def after_dot(r):
            if not issue_dma:
                return
            kind = 0 if r < max(n_sub // 2, 1) else 1
            lo = done[kind]
            if kind == 0:
                gather_rows(inv_nxt_ref, 1 - slot, lo, lo + batch)
            else:
                scatter_rows(inv_prv_ref, 1 - slot, lo, lo + batch)
            done[kind] = min(lo + batch, ts)

        for r in range(n_sub):
            rows = slice(r * sub, (r + 1) * sub)
            meta = xbuf[slot, rows, D_MODEL:]
            row_grp = jnp.sum(jnp.where(lane == LANE_GROUP, meta, 0.0), axis=-1, keepdims=True)
            member = row_grp == grp.astype(F32)
            hm = hm_ref[rows, :]
            y = jnp.zeros((sub, D_MODEL), F32)
            for e in range(EXPERTS_PER_GROUP):
                w = jnp.sum(jnp.where(lane == e, meta, 0.0), axis=-1, keepdims=True)
                w = jnp.where(member, w, 0.0)
                a = _dot(hm, wg_ref[e])
                after_dot(r)
                b = _dot(hm, wu_ref[e])
                after_dot(r)
                u = (a * _sigmoid(a)) * b * w
                y = y + _dot(u.astype(BF16), wd_ref[e])
                after_dot(r)
            yacc_ref[rows, :] += y
        if issue_dma:
            assert done == [ts, ts], done

    @pl.when(v == 0)
    def _():
        gather_rows(inv_cur_ref, 0, 0, ts)

    @pl.when((v == 0) | (grp != vg_ref[jnp.maximum(v - 1, 0)]))
    def _():
        wg_ref[...] = wg32_ref[...].astype(BF16)
        wu_ref[...] = wu32_ref[...].astype(BF16)
        wd_ref[...] = wd32_ref[...].astype(BF16)

    @pl.when(first)
    def _():
        gather_wait(slot)

        @pl.when(t == 0)
        def _():
            gather_rows(inv_nxt_ref, 1, 0, ts)

        hm_ref[...] = _rms(xbuf[slot, :, :D_MODEL], gm_ref[...]).astype(BF16)
        yacc_ref[...] = jnp.zeros_like(yacc_ref)

    @pl.when(overlapped)
    def _():
        experts(True)

    @pl.when(((flags & VISIT_VALID) != 0) & jnp.logical_not(overlapped))
    def _():
        experts(False)

    @pl.when((flags & VISIT_LAST) != 0)
    def _():
        @pl.when(t >= 2)
        def _():
            scatter_wait(slot)

        obuf[slot] = _rms(xbuf[slot, :, :D_MODEL] + yacc_ref[...], gf_ref[...])

    @pl.when(v == pl.num_programs(0) - 1)
    def _():
        last_slot = (nt - 1) % 2
        scatter_rows(inv_cur_ref, last_slot, 0, ts)
        scatter_wait(0)
        scatter_wait(1)
        gather_wait(1 - last_slot)


def _moe_sorted(xa, inv, vt, vg, vf, gm, wg, wu, wd, gf, *, ts, sub):
    N = xa.shape[0]
    nt = N // ts
    assert nt >= 2
    E, D, FF = EXPERTS_PER_GROUP, D_MODEL, EXPERT_FF
    grp = lambda v, vt, vg, vf: (vg[v], 0, 0)
    const = lambda v, vt, vg, vf: (0, 0)
    smem_row = lambda fn: pl.BlockSpec((None, 1, ts), fn, memory_space=pltpu.SMEM)
    grid_spec = pltpu.PrefetchScalarGridSpec(
        num_scalar_prefetch=3,
        grid=(vt.shape[0],),
        in_specs=[
            smem_row(lambda v, vt, vg, vf: (vt[v], 0, 0)),
            smem_row(lambda v, vt, vg, vf: (jnp.minimum(vt[v] + 1, nt - 1), 0, 0)),
            smem_row(lambda v, vt, vg, vf: (jnp.maximum(vt[v] - 1, 0), 0, 0)),
            pl.BlockSpec(memory_space=pl.ANY),
            pl.BlockSpec((1, D), const),
            pl.BlockSpec((E, D, FF), grp), pl.BlockSpec((E, D, FF), grp),
            pl.BlockSpec((E, FF, D), grp),
            pl.BlockSpec((1, D), const),
        ],
        out_specs=pl.BlockSpec(memory_space=pl.ANY),
        scratch_shapes=[
            pltpu.VMEM((2, ts, XA_WIDTH), F32), pltpu.VMEM((2, ts, D), F32),
            pltpu.VMEM((ts, D), BF16), pltpu.VMEM((ts, D), F32),
            pltpu.VMEM((E, D, FF), BF16), pltpu.VMEM((E, D, FF), BF16), pltpu.VMEM((E, FF, D), BF16),
            pltpu.SemaphoreType.DMA((2,)), pltpu.SemaphoreType.DMA((2,)),
        ],
    )
    return pl.pallas_call(
        functools.partial(_moe_body, ts=ts, nt=nt, sub=sub),
        grid_spec=grid_spec,
        out_shape=jax.ShapeDtypeStruct((N, D), F32),
        compiler_params=pltpu.CompilerParams(
            dimension_semantics=("arbitrary",), vmem_limit_bytes=VMEM_LIMIT),
        name="moe_sorted",
    )(vt, vg, vf, inv, inv, inv, xa, gm, wg, wu, wd, gf)


def kernel(x, mem, norm_mix_g, w_in, fox_bf, conv_dw_w, conv_dw_b, conv_ln_g, conv_ln_b, conv_pw_w, attn_branch_w, w_out, norm_xa_g, norm_mem_g, xa_wq, xa_wk, xa_wv, xa_wo, norm_moe_g, router_group_w, router_group_b, router_expert_w, router_expert_b, expert_w_gate, expert_w_up, expert_w_down, norm_final_g):
    B, T, D = x.shape
    depth = w_in.shape[0]
    assert depth == 1, "the final norm is fused into the last layer's MoE kernel"
    row = lambda v: v.reshape(1, -1).astype(F32)
    for l in range(depth):
        u, gc, ga, qkv_t, f_t = _proj_in(x, row(norm_mix_g[l]), jnp.swapaxes(w_in, 1, 2).astype(F32),
                                         layer=l, tm=512)
        qaux, kaux = _fox_prep(f_t, fox_bf[l].reshape(FOX_HEADS, 1).astype(F32))
        attn_t = _fox_attn(qkv_t, qaux, kaux, tq=1024)
        x = _mix_out(x, u, gc, ga, attn_t, conv_dw_w[l].astype(F32), row(conv_dw_b[l]),
                     row(conv_ln_g[l]), row(conv_ln_b[l]), conv_pw_w[l].astype(F32),
                     attn_branch_w[l].astype(F32), w_out[l].astype(F32), tc=512, rb=512)
        kx, vx_t = _mem_kv(mem, row(norm_mem_g[l]), xa_wk[l].astype(BF16),
                           xa_wv[l].T.astype(BF16))
        wr_t = jnp.zeros((ROUTE_ROWS, D), F32).at[:N_GROUPS].set(router_group_w[l].T)
        br_t = jnp.zeros((ROUTE_ROWS, 1), F32).at[:N_GROUPS, 0].set(router_group_b[l])
        for g in range(N_GROUPS):
            r0, e0 = SUBLANES * (g + 1), EXPERTS_PER_GROUP * g
            wr_t = wr_t.at[r0:r0 + EXPERTS_PER_GROUP].set(
                router_expert_w[l][:, e0:e0 + EXPERTS_PER_GROUP].T)
            br_t = br_t.at[r0:r0 + EXPERTS_PER_GROUP, 0].set(
                router_expert_b[l][e0:e0 + EXPERTS_PER_GROUP])
        xa, cnt, code = _xattn_route(x, row(norm_xa_g[l]), xa_wq[l].T.astype(BF16), kx, vx_t,
                                     xa_wo[l].astype(BF16), row(norm_moe_g[l]),
                                     wr_t.astype(BF16), br_t, tx=1024, row_blocks=1)
        xa = xa.reshape(B * T, XA_WIDTH)
        inv, vt, vg, vf = _moe_plan(code[:, :, LANE_CODE, :].reshape(B * T), cnt, ts=512)
        x = _moe_sorted(xa, inv, vt, vg, vf, row(norm_moe_g[l]), expert_w_gate[l].astype(F32),
                        expert_w_up[l].astype(F32), expert_w_down[l].astype(F32),
                        row(norm_final_g), ts=512, sub=256).reshape(B, T, D)
    return x
```

```python
import functools

import jax
import jax.numpy as jnp
from jax import lax
from jax.experimental import pallas as pl
from jax.experimental.pallas import tpu as pltpu

D_MODEL = 1024
MEM_LEN = 256
EPS = 1e-6
CONV_DIM = 512
CONV_WIDTH = 31
FOX_HEADS = 8
FOX_HEAD_DIM = 64
FOX_DIM = FOX_HEADS * FOX_HEAD_DIM
XA_HEADS = 4
XA_HEAD_DIM = 128
XA_DIM = XA_HEADS * XA_HEAD_DIM
N_GROUPS = 4
EXPERTS_PER_GROUP = 4
N_EXPERTS = N_GROUPS * EXPERTS_PER_GROUP
EXPERT_FF = 256

AUX_ROWS = 16
LOG2E = 1.4426950408889634
VMEM_LIMIT = 56 * 1024 * 1024

F32 = jnp.float32
BF16 = jnp.bfloat16


def _const_spec(shape):
    n = len(shape)
    return pl.BlockSpec(shape, lambda *_: (0,) * n)


def _rms(x, g):
    return x * lax.rsqrt(jnp.mean(x * x, axis=-1, keepdims=True) + EPS) * g


def _sigmoid(x):
    return 0.5 * jnp.tanh(0.5 * x) + 0.5


def _dot(a, b):
    return jnp.dot(a, b, preferred_element_type=F32)


def _dot_nt(a, b):
    return lax.dot_general(a, b, (((1,), (1,)), ((), ())), preferred_element_type=F32)


def _dot_tn(a, b):
    return lax.dot_general(a, b, (((0,), (0,)), ((), ())), preferred_element_type=F32)


C_GLU = 2 * CONV_DIM
C_F = C_GLU + 3 * FOX_DIM
C_GC = C_F + FOX_HEADS
C_GA = C_GC + D_MODEL


def _proj_in_body(x_ref, g_ref, w_ref, u_ref, gc_ref, ga_ref, qkv_t_ref, f_t_ref,
                  wglu_ref, wgc_ref, wga_ref, wqkv_t_ref, wf_t_ref):
    @pl.when((pl.program_id(0) == 0) & (pl.program_id(1) == 0))
    def _():
        for j in range(C_GLU // D_MODEL):
            rows = slice(j * D_MODEL, (j + 1) * D_MODEL)
            wglu_ref[:, rows] = w_ref[rows, :].T.astype(BF16)
        wgc_ref[...] = w_ref[C_GC:C_GA, :].T.astype(BF16)
        wga_ref[...] = w_ref[C_GA:, :].T.astype(BF16)
        wqkv_t_ref[...] = w_ref[C_GLU:C_F, :].astype(BF16)
        wf_t_ref[...] = w_ref[C_F:C_GC, :].astype(BF16)

    h = _rms(x_ref[...], g_ref[...]).astype(BF16)
    glu = _dot(h, wglu_ref[...])
    u_ref[...] = glu[:, :CONV_DIM] * _sigmoid(glu[:, CONV_DIM:])
    gc_ref[...] = _dot(h, wgc_ref[...]).astype(BF16)
    ga_ref[...] = _dot(h, wga_ref[...]).astype(BF16)
    qkv_t = _dot_nt(wqkv_t_ref[...], h)
    scale = FOX_HEAD_DIM ** -0.5 * LOG2E
    qkv_t_ref[:FOX_DIM, :] = (qkv_t[:FOX_DIM] * scale).astype(BF16)
    qkv_t_ref[FOX_DIM:, :] = qkv_t[FOX_DIM:].astype(BF16)
    f_t_ref[...] = _dot_nt(wf_t_ref[...], h)


def _proj_in(x, g, w, *, layer, tm):
    B, T, D = x.shape
    grid = (B, T // tm)
    row = lambda b, i: (b, i, 0)
    col = lambda b, i: (b, 0, i)
    return pl.pallas_call(
        _proj_in_body,
        grid=grid,
        in_specs=[
            pl.BlockSpec((None, tm, D), row),
            _const_spec((1, D)),
            pl.BlockSpec((None,) + w.shape[1:], lambda b, i: (layer, 0, 0),
                         pipeline_mode=pl.Buffered(1)),
        ],
        out_specs=[
            pl.BlockSpec((None, tm, CONV_DIM), row),
            pl.BlockSpec((None, tm, D), row),
            pl.BlockSpec((None, tm, D), row),
            pl.BlockSpec((None, 3 * FOX_DIM, tm), col),
            pl.BlockSpec((None, FOX_HEADS, tm), col),
        ],
        out_shape=[
            jax.ShapeDtypeStruct((B, T, CONV_DIM), F32),
            jax.ShapeDtypeStruct((B, T, D), BF16),
            jax.ShapeDtypeStruct((B, T, D), BF16),
            jax.ShapeDtypeStruct((B, 3 * FOX_DIM, T), BF16),
            jax.ShapeDtypeStruct((B, FOX_HEADS, T), F32),
        ],
        scratch_shapes=[pltpu.VMEM((D, C_GLU), BF16), pltpu.VMEM((D, D), BF16),
                        pltpu.VMEM((D, D), BF16), pltpu.VMEM((3 * FOX_DIM, D), BF16),
                        pltpu.VMEM((FOX_HEADS, D), BF16)],
        compiler_params=pltpu.CompilerParams(
            dimension_semantics=("arbitrary", "arbitrary"), vmem_limit_bytes=VMEM_LIMIT),
        name="proj_in",
    )(x, g, w)


def _split3(c):
    hi = c.astype(BF16)
    r = c - hi.astype(F32)
    mid = r.astype(BF16)
    lo = (r - mid.astype(F32)).astype(BF16)
    return hi, mid, lo


def _fox_prep_body(f_t_ref, bf_ref, qaux_ref, kaux_ref, c_ref, *, chunk):
    T = f_t_ref.shape[-1]
    z = f_t_ref[...] + bf_ref[...]
    logf = jnp.minimum(z, 0.0) - jnp.log1p(jnp.exp(-jnp.abs(z)))
    r_i = lax.broadcasted_iota(jnp.int32, (chunk, chunk), 0)
    c_i = lax.broadcasted_iota(jnp.int32, (chunk, chunk), 1)
    tri = jnp.where(r_i <= c_i, 1.0, 0.0).astype(BF16)
    carry = jnp.zeros((FOX_HEADS, 1), F32)
    for j in range(T // chunk):
        blk = logf[:, j * chunk:(j + 1) * chunk]
        hi, mid, lo = _split3(blk)
        cs = (_dot(lo, tri) + _dot(mid, tri)) + _dot(hi, tri) + carry
        c_ref[:, j * chunk:(j + 1) * chunk] = cs
        carry = cs[:, chunk - 1:chunk]
    row = lax.broadcasted_iota(jnp.int32, (AUX_ROWS, T), 0)
    for h in range(FOX_HEADS):
        hi, mid, lo = _split3(c_ref[h:h + 1, :] * LOG2E)
        pieces = [jnp.broadcast_to(p.astype(F32), (AUX_ROWS, T)) for p in (hi, mid, lo)]
        one = jnp.where(row < 3, 1.0, 0.0)
        val = jnp.where(row == 0, pieces[0], jnp.where(row == 1, pieces[1], pieces[2]))
        val_k = jnp.where(row == 3, pieces[0], jnp.where(row == 4, pieces[1], pieces[2]))
        qaux = jnp.where(row < 3, val, jnp.where(row < 6, 1.0, 0.0))
        kaux = jnp.where(row < 3, one, jnp.where(row < 6, -val_k, 0.0))
        qaux_ref[h] = qaux.astype(BF16)
        kaux_ref[h] = kaux.astype(BF16)


def _fox_prep(f_t, bf):
    B, H, T = f_t.shape
    aux = jax.ShapeDtypeStruct((B, H, AUX_ROWS, T), BF16)
    aux_spec = pl.BlockSpec((None, H, AUX_ROWS, T), lambda b: (b, 0, 0, 0))
    return pl.pallas_call(
        functools.partial(_fox_prep_body, chunk=512),
        grid=(B,),
        in_specs=[pl.BlockSpec((None, H, T), lambda b: (b, 0, 0)), _const_spec((H, 1))],
        out_specs=[aux_spec, aux_spec],
        out_shape=[aux, aux],
        scratch_shapes=[pltpu.VMEM((H, T), F32)],
        compiler_params=pltpu.CompilerParams(dimension_semantics=("parallel",)),
        name="fox_prep",
    )(f_t, bf)


MASK_BLOCK = 256


def _fox_attn_body(q_ref, k_ref, v_ref, qaux_ref, kaux_ref, o_ref, *scratch, tq, heads):
    T = q_ref.shape[-1]
    dh = FOX_HEAD_DIM
    mb = MASK_BLOCK
    nj = tq // mb
    kk = lax.broadcasted_iota(jnp.int32, (mb, mb), 0)
    qq = lax.broadcasted_iota(jnp.int32, (mb, mb), 1)
    causal = kk <= qq
    ones = jnp.where(lax.broadcasted_iota(jnp.int32, (AUX_ROWS, tq), 0) == 0, 1.0, 0.0).astype(BF16)
    sbufs = [scratch[2 * h:2 * h + 2] for h in range(heads)]
    pbufs = scratch[2 * heads:]

    tasks = []
    for qi in range(T // tq):
        tasks.append((qi, qi, True))
        tasks += [(qi, c, False) for c in range(qi)]

    def scores(h, task, j, sb, mc):
        qi, c, diag = task
        c0 = j * mb if diag else 0
        r0 = c * tq + j * mb
        hd = slice(h * dh, (h + 1) * dh)
        cols = slice(qi * tq + c0, (qi + 1) * tq)
        ka = jnp.concatenate([k_ref[hd, r0:r0 + mb], kaux_ref[h, :, r0:r0 + mb]], axis=0)
        qa = jnp.concatenate([q_ref[hd, cols], qaux_ref[h, :, cols]], axis=0)
        s = _dot_tn(ka, qa)
        if diag:
            head = jnp.where(causal, s[:, :mb], -jnp.inf)
            s = head if tq - c0 == mb else jnp.concatenate([head, s[:, mb:]], axis=1)
        sb[j * mb:(j + 1) * mb, c0:] = s
        cm = jnp.max(s, axis=0, keepdims=True)
        if c0:
            cm = jnp.concatenate([jnp.full((1, c0), -jnp.inf, F32), cm], axis=1)
        return cm if mc is None else jnp.maximum(mc, cm)

    def probs(h, task, j, sb, m_new):
        c0 = j * mb if task[2] else 0
        rows = slice(j * mb, (j + 1) * mb)
        pbufs[h][rows, c0:] = jnp.exp2(sb[rows, c0:] - m_new[:, c0:]).astype(BF16)

    def values(h, task):
        _, c, diag = task
        p_ref = pbufs[h]
        va = jnp.concatenate([v_ref[h * dh:(h + 1) * dh, c * tq:(c + 1) * tq], ones], axis=0)
        if not diag:
            return _dot(va, p_ref[...])
        return jnp.concatenate(
            [_dot(va[:, :(bj + 1) * mb], p_ref[:(bj + 1) * mb, bj * mb:(bj + 1) * mb])
             for bj in range(nj)], axis=1)

    hs = range(heads)
    mc = [None] * heads
    for j in range(nj):
        for h in hs:
            mc[h] = scores(h, tasks[0], j, sbufs[h][0], mc[h])
    m_run, acc = [None] * heads, [None] * heads
    for k, task in enumerate(tasks):
        qi, c, diag = task
        nxt = tasks[k + 1] if k + 1 < len(tasks) else None
        m_new = [mc[h] if diag else jnp.maximum(m_run[h], mc[h]) for h in hs]
        mc = [None] * heads
        for j in range(nj):
            for h in hs:
                if nxt is not None:
                    mc[h] = scores(h, nxt, j, sbufs[h][(k + 1) % 2], mc[h])
                probs(h, task, j, sbufs[h][k % 2], m_new[h])
        for h in hs:
            pv = values(h, task)
            acc[h] = pv if diag else jnp.exp2(m_run[h] - m_new[h]) * acc[h] + pv
            if nxt is None or nxt[0] != qi:
                o_ref[h * dh:(h + 1) * dh, qi * tq:(qi + 1) * tq] = (
                    acc[h][:dh] / acc[h][dh:dh + 1]).astype(o_ref.dtype)
        m_run = m_new


def _fox_attn(qkv_t, qaux, kaux, *, tq, heads):
    B, _, T = qkv_t.shape
    H, dh = FOX_HEADS, FOX_HEAD_DIM
    hg = H // heads
    part = lambda off: pl.BlockSpec((None, heads * dh, T), lambda b, g: (b, off * hg + g, 0))
    aux = pl.BlockSpec((None, heads, AUX_ROWS, T), lambda b, g: (b, g, 0, 0))
    return pl.pallas_call(
        functools.partial(_fox_attn_body, tq=tq, heads=heads),
        grid=(B, hg),
        in_specs=[part(0), part(1), part(2), aux, aux],
        out_specs=pl.BlockSpec((None, heads * dh, T), lambda b, g: (b, g, 0)),
        out_shape=jax.ShapeDtypeStruct((B, H * dh, T), BF16),
        scratch_shapes=([pltpu.VMEM((tq, tq), F32)] * (2 * heads)
                        + [pltpu.VMEM((tq, tq), BF16)] * heads),
        compiler_params=pltpu.CompilerParams(
            dimension_semantics=("parallel", "parallel"), vmem_limit_bytes=VMEM_LIMIT),
        name="fox_attn",
    )(qkv_t, qkv_t, qkv_t, qaux, kaux)


HALO = 32
SUBLANES = 8


def _mix_out_body(x_ref, u_ref, uprev_ref, gc_ref, ga_ref, attn_t_ref, dww_ref, dwb_ref,
                  lng_ref, lnb_ref, pw32_ref, abw32_ref, wout32_ref, o_ref, ubuf_ref, ushift_ref,
                  pw_ref, abw_ref, wout_ref, *, tc, rb):
    first = pl.program_id(1) == 0

    @pl.when(first & (pl.program_id(0) == 0))
    def _():
        pw_ref[...] = pw32_ref[...].astype(BF16)
        abw_ref[...] = abw32_ref[...].astype(BF16)
        wout_ref[...] = wout32_ref[...].astype(BF16)

    ubuf_ref[:HALO, :] = jnp.where(first, 0.0, uprev_ref[...])
    ubuf_ref[HALO:, :] = u_ref[...]
    span = tc + HALO - SUBLANES
    for s in range(1, SUBLANES):
        ushift_ref[s - 1, :, :] = ubuf_ref[s:s + span, :]

    def conv_branch(r0):
        acc = jnp.zeros((rb, CONV_DIM), F32)
        for k in range(CONV_WIDTH):
            off = r0 + HALO - (CONV_WIDTH - 1) + k
            s, base = off % SUBLANES, off - off % SUBLANES
            rows = ubuf_ref[base:base + rb, :] if s == 0 else ushift_ref[s - 1, base:base + rb, :]
            acc = acc + rows * dww_ref[k:k + 1, :]
        acc = acc + dwb_ref[...]
        mu = jnp.mean(acc, axis=-1, keepdims=True)
        d = acc - mu
        var = jnp.mean(d * d, axis=-1, keepdims=True)
        y = d * lax.rsqrt(var + EPS) * lng_ref[...] + lnb_ref[...]
        return (y * _sigmoid(y)).astype(BF16)

    def project(r0, y):
        rows = slice(r0, r0 + rb)
        conv_out = _dot(y, pw_ref[...])
        attn_out = _dot_tn(attn_t_ref[:, rows], abw_ref[...])
        merged = (_sigmoid(gc_ref[rows, :].astype(F32)) * conv_out
                  + _sigmoid(ga_ref[rows, :].astype(F32)) * attn_out)
        o_ref[rows, :] = x_ref[rows, :] + _dot(merged.astype(BF16), wout_ref[...])

    nb = tc // rb
    y = conv_branch(0)
    for j in range(nb):
        y_next = conv_branch((j + 1) * rb) if j + 1 < nb else None
        project(j * rb, y)
        y = y_next


def _mix_out(x, u, gc, ga, attn_t, dww, dwb, lng, lnb, pw, abw, wout, *, tc, rb):
    B, T, D = x.shape
    row = lambda b, i: (b, i, 0)
    halo_blocks = tc // HALO
    return pl.pallas_call(
        functools.partial(_mix_out_body, tc=tc, rb=rb),
        grid=(B, T // tc),
        in_specs=[
            pl.BlockSpec((None, tc, D), row),
            pl.BlockSpec((None, tc, CONV_DIM), row),
            pl.BlockSpec((None, HALO, CONV_DIM),
                         lambda b, i: (b, jnp.maximum(i * halo_blocks - 1, 0), 0)),
            pl.BlockSpec((None, tc, D), row),
            pl.BlockSpec((None, tc, D), row),
            pl.BlockSpec((None, FOX_DIM, tc), lambda b, i: (b, 0, i)),
            _const_spec(dww.shape), _const_spec(dwb.shape), _const_spec(lng.shape),
            _const_spec(lnb.shape), _const_spec(pw.shape), _const_spec(abw.shape),
            _const_spec(wout.shape),
        ],
        out_specs=pl.BlockSpec((None, tc, D), row),
        out_shape=jax.ShapeDtypeStruct((B, T, D), F32),
        scratch_shapes=[pltpu.VMEM((HALO + tc, CONV_DIM), F32),
                        pltpu.VMEM((SUBLANES - 1, HALO + tc - SUBLANES, CONV_DIM), F32),
                        pltpu.VMEM(pw.shape, BF16), pltpu.VMEM(abw.shape, BF16),
                        pltpu.VMEM(wout.shape, BF16)],
        compiler_params=pltpu.CompilerParams(
            dimension_semantics=("arbitrary", "arbitrary"), vmem_limit_bytes=VMEM_LIMIT),
        name="mix_out",
    )(x, u, u, gc, ga, attn_t, dww, dwb, lng, lnb, pw, abw, wout)


def _mem_kv_body(m_ref, g_ref, wk_ref, wv_t_ref, k_ref, v_t_ref):
    h = _rms(m_ref[...], g_ref[...]).astype(BF16)
    k_ref[...] = _dot(h, wk_ref[...]).astype(BF16)
    v_t_ref[...] = _dot_nt(wv_t_ref[...], h).astype(BF16)


def _mem_kv(mem, g, wk, wv_t):
    B, M, D = mem.shape
    return pl.pallas_call(
        _mem_kv_body,
        grid=(B,),
        in_specs=[pl.BlockSpec((None, M, D), lambda b: (b, 0, 0)), _const_spec((1, D)),
                  _const_spec(wk.shape), _const_spec(wv_t.shape)],
        out_specs=[pl.BlockSpec((None, M, XA_DIM), lambda b: (b, 0, 0)),
                   pl.BlockSpec((None, XA_DIM, M), lambda b: (b, 0, 0))],
        out_shape=[jax.ShapeDtypeStruct((B, M, XA_DIM), BF16),
                   jax.ShapeDtypeStruct((B, XA_DIM, M), BF16)],
        compiler_params=pltpu.CompilerParams(dimension_semantics=("parallel",)),
        name="mem_kv",
    )(mem, g, wk, wv_t)


META_LANES = 128
XA_WIDTH = D_MODEL + META_LANES
LANE_GROUP = EXPERTS_PER_GROUP
LANE_RANK = EXPERTS_PER_GROUP + 1
LANE_CODE = EXPERTS_PER_GROUP + 2
CODE_ROWS = SUBLANES
ROUTE_ROWS = SUBLANES * (N_GROUPS + 1)
CUMSUM_BLOCK = 256


def _first_max_row(v, valid, row):
    neg = jnp.where(valid, v, -jnp.inf)
    mx = jnp.max(neg, axis=0, keepdims=True)
    idx = jnp.min(jnp.where(valid & (neg == mx), row, SUBLANES), axis=0, keepdims=True)
    return mx, idx


def _route(logits_t, run):
    n = logits_t.shape[1]
    row = lax.broadcasted_iota(jnp.int32, (SUBLANES, n), 0)
    valid = row < N_GROUPS
    glog = logits_t[:SUBLANES]
    gmax, g_star = _first_max_row(glog, valid, row)
    p_gsel = 1.0 / jnp.sum(jnp.where(valid, jnp.exp(glog - gmax), 0.0), axis=0, keepdims=True)
    elog = logits_t[SUBLANES:2 * SUBLANES]
    for g in range(1, N_GROUPS):
        elog = jnp.where(g_star == g, logits_t[(g + 1) * SUBLANES:(g + 2) * SUBLANES], elog)
    emax, _ = _first_max_row(elog, valid, row)
    ex = jnp.where(valid, jnp.exp(elog - emax), 0.0)
    p_exp = ex / jnp.sum(ex, axis=0, keepdims=True)
    v1, i1 = _first_max_row(p_exp, valid, row)
    v2, i2 = _first_max_row(p_exp, valid & (row != i1), row)
    tot = v1 + v2
    w = jnp.where(row == i1, p_gsel * (v1 / tot), jnp.where(row == i2, p_gsel * (v2 / tot), 0.0))
    onehot = jnp.where(row == g_star, 1.0, 0.0).astype(BF16)
    cb = min(n, CUMSUM_BLOCK)
    r_i = lax.broadcasted_iota(jnp.int32, (cb, cb), 0)
    c_i = lax.broadcasted_iota(jnp.int32, (cb, cb), 1)
    tri = jnp.where(r_i <= c_i, 1.0, 0.0).astype(BF16)
    pieces = []
    for j in range(n // cb):
        piece = _dot(onehot[:, j * cb:(j + 1) * cb], tri) + run
        pieces.append(piece)
        run = piece[:, cb - 1:cb]
    cum = jnp.concatenate(pieces, axis=1)
    rank = jnp.sum(jnp.where(row == g_star, cum, 0.0), axis=0, keepdims=True) - 1.0
    g_f = g_star.astype(F32)
    meta_t = jnp.where(valid, w, jnp.where(row == LANE_GROUP, g_f, jnp.where(
        row == LANE_RANK, rank, jnp.where(row == LANE_CODE, rank * N_GROUPS + g_f, 0.0))))
    return meta_t, cum[:, n - 1:n]


def _xattn_body(x_ref, g_ref, wq_t_ref, k_ref, v_t_ref, wo_ref, gm_ref, wr_t_ref, br_t_ref,
                xa_ref, cnt_ref, code_ref, run_ref, *, row_blocks):
    @pl.when((pl.program_id(0) == 0) & (pl.program_id(1) == 0))
    def _():
        run_ref[...] = jnp.zeros_like(run_ref)

    tx = x_ref.shape[0]
    hb = tx // row_blocks
    scale = XA_HEAD_DIM ** -0.5

    def attend(r0):
        rows = slice(r0, r0 + hb)
        x = x_ref[rows, :]
        q_t = _dot_nt(wq_t_ref[...], _rms(x, g_ref[...]).astype(BF16)).astype(BF16)
        yield
        head_rows = [slice(h * XA_HEAD_DIM, (h + 1) * XA_HEAD_DIM) for h in range(XA_HEADS)]
        scores = [_dot(k_ref[:, sl], q_t[sl, :]) * scale for sl in head_rows]
        yield
        heads = []
        for sl, s_t in zip(head_rows, scores):
            p = jnp.exp(s_t - jnp.max(s_t, axis=0, keepdims=True))
            l = jnp.sum(p, axis=0, keepdims=True)
            heads.append((_dot(v_t_ref[sl, :], p.astype(BF16)) / l).astype(BF16))
            yield
        x2 = x + _dot_tn(jnp.concatenate(heads, axis=0), wo_ref[...])
        xa_ref[rows, :D_MODEL] = x2
        yield
        hm = _rms(x2, gm_ref[...]).astype(BF16)
        logits[r0] = _dot_nt(wr_t_ref[...], hm) + br_t_ref[...]
        yield

    logits = {}
    blocks = [attend(r0) for r0 in range(0, tx, hb)]
    for _ in range(XA_HEADS + 4):
        for blk in blocks:
            next(blk)
    run = run_ref[...]
    for r0 in range(0, tx, hb):
        meta_t, run = _route(logits[r0], run)
        code_ref[:, r0:r0 + hb] = meta_t.astype(jnp.int32)
        xa_ref[r0:r0 + hb, D_MODEL:] = jnp.concatenate(
            [meta_t, jnp.zeros((META_LANES - CODE_ROWS, hb), F32)], axis=0).T
    run_ref[...] = run
    cnt_ref[...] = jnp.broadcast_to(run, cnt_ref.shape).astype(jnp.int32)


def _xattn_route(x, g, wq_t, kx, vx_t, wo, gm, wr_t, br_t, *, tx, row_blocks):
    B, T, D = x.shape
    row = lambda b, i: (b, i, 0)
    return pl.pallas_call(
        functools.partial(_xattn_body, row_blocks=row_blocks),
        grid=(B, T // tx),
        in_specs=[pl.BlockSpec((None, tx, D), row), _const_spec((1, D)),
                  _const_spec(wq_t.shape),
                  pl.BlockSpec((None, MEM_LEN, XA_DIM), lambda b, i: (b, 0, 0)),
                  pl.BlockSpec((None, XA_DIM, MEM_LEN), lambda b, i: (b, 0, 0)),
                  _const_spec(wo.shape), _const_spec((1, D)),
                  _const_spec(wr_t.shape), _const_spec(br_t.shape)],
        out_specs=[pl.BlockSpec((None, tx, XA_WIDTH), row), _const_spec((CODE_ROWS, META_LANES)),
                   pl.BlockSpec((None, None, CODE_ROWS, tx), lambda b, i: (b, i, 0, 0))],
        out_shape=[jax.ShapeDtypeStruct((B, T, XA_WIDTH), F32),
                   jax.ShapeDtypeStruct((CODE_ROWS, META_LANES), jnp.int32),
                   jax.ShapeDtypeStruct((B, T // tx, CODE_ROWS, tx), jnp.int32)],
        scratch_shapes=[pltpu.VMEM((CODE_ROWS, 1), F32)],
        compiler_params=pltpu.CompilerParams(
            dimension_semantics=("arbitrary", "arbitrary"), vmem_limit_bytes=VMEM_LIMIT),
        name="xattn_route",
    )(x, g, wq_t, kx, vx_t, wo, gm, wr_t, br_t)


VISIT_FIRST, VISIT_LAST, VISIT_VALID = 1, 2, 4


def _moe_plan_body(code_ref, cnt_ref, inv_ref, vt_ref, vg_ref, vf_ref, offs_ref, *, ts):
    n = code_ref.shape[0]
    nt, nv = n // ts, vt_ref.shape[0]
    ends = [cnt_ref[0, 0]]
    for g in range(1, N_GROUPS - 1):
        ends.append(ends[-1] + cnt_ref[g, 0])
    offs_ref[0] = 0
    for g in range(1, N_GROUPS):
        offs_ref[g] = ends[g - 1]
    group_bits = N_GROUPS.bit_length() - 1
    assert 1 << group_bits == N_GROUPS

    def place(t, _):
        code = code_ref[t]
        slot = offs_ref[code & (N_GROUPS - 1)] + lax.shift_right_logical(code, group_bits)
        inv_ref[slot] = t
        return 0

    lax.fori_loop(0, n, place, 0, unroll=8)

    def group_of(slot):
        return sum(jnp.where(e <= slot, 1, 0) for e in ends)

    def tile(j, v):
        g_lo, g_hi = group_of(j * ts), group_of(j * ts + ts - 1)
        for k in range(N_GROUPS):
            @pl.when(k <= g_hi - g_lo)
            def _():
                vt_ref[v + k] = j
                vg_ref[v + k] = g_lo + k
                vf_ref[v + k] = (VISIT_VALID + (VISIT_FIRST if k == 0 else 0)
                                 + jnp.where(k == g_hi - g_lo, VISIT_LAST, 0))
        return v + g_hi - g_lo + 1

    used = lax.fori_loop(0, nt, tile, 0)
    g_last = group_of(n - 1)

    def pad(v, _):
        vt_ref[v] = nt - 1
        vg_ref[v] = g_last
        vf_ref[v] = 0
        return 0

    lax.fori_loop(used, nv, pad, 0)


def _moe_plan(code, cnt, *, ts):
    n = code.shape[0]
    nt = n // ts
    nv = nt + N_GROUPS - 1
    smem = pl.BlockSpec(memory_space=pltpu.SMEM)
    visits = jax.ShapeDtypeStruct((nv,), jnp.int32)
    inv, vt, vg, vf = pl.pallas_call(
        functools.partial(_moe_plan_body, ts=ts),
        in_specs=[smem, smem],
        out_specs=[smem, smem, smem, smem],
        out_shape=[jax.ShapeDtypeStruct((n,), jnp.int32), visits, visits, visits],
        scratch_shapes=[pltpu.SMEM((N_GROUPS,), jnp.int32)],
        name="moe_plan",
    )(code, cnt)
    return inv.reshape(nt, 1, ts), vt, vg, vf


def _moe_body(vt_ref, vg_ref, vf_ref, inv_cur_ref, inv_nxt_ref, inv_prv_ref, xa_hbm, gm_ref,
              wg32_ref, wu32_ref, wd32_ref, gf_ref, out_hbm, xbuf, obuf, hm_ref, yacc_ref,
              wg_ref, wu_ref, wd_ref, gsem, ssem, *, ts, nt, sub):
    v = pl.program_id(0)
    t, grp, flags = vt_ref[v], vg_ref[v], vf_ref[v]
    slot = lax.rem(t, 2)
    first = (flags & VISIT_FIRST) != 0
    overlapped = first & (t >= 1)

    def gather_rows(idx_ref, dst_slot, lo, hi):
        for r in range(lo, min(hi, ts)):
            pltpu.make_async_copy(xa_hbm.at[pl.ds(idx_ref[0, r], 1)],
                                  xbuf.at[dst_slot, pl.ds(r, 1)], gsem.at[dst_slot]).start()

    def gather_wait(dst_slot):
        pltpu.make_async_copy(xa_hbm.at[pl.ds(0, ts)], xbuf.at[dst_slot], gsem.at[dst_slot]).wait()

    def scatter_rows(idx_ref, src_slot, lo, hi):
        for r in range(lo, min(hi, ts)):
            pltpu.make_async_copy(obuf.at[src_slot, pl.ds(r, 1)],
                                  out_hbm.at[pl.ds(idx_ref[0, r], 1)], ssem.at[src_slot]).start()

    def scatter_wait(src_slot):
        pltpu.make_async_copy(obuf.at[src_slot], out_hbm.at[pl.ds(0, ts)], ssem.at[src_slot]).wait()

    def experts(issue_dma):
        lane = lax.broadcasted_iota(jnp.int32, (sub, META_LANES), 1)
        n_sub = ts // sub
        half = n_sub * 3 * EXPERTS_PER_GROUP // 2
        batch = -(-ts // half)
        done = [0, 0]
        dots = [0]

        def after_dot(r):
            if not issue_dma:
                return
            kind = 0 if dots[0] < half else 1
            dots[0] += 1
            lo = done[kind]
            if kind == 0:
                gather_rows(inv_nxt_ref, 1 - slot, lo, lo + batch)
            else:
                scatter_rows(inv_prv_ref, 1 - slot, lo, lo + batch)
            done[kind] = min(lo + batch, ts)

        for r in range(n_sub):
            rows = slice(r * sub, (r + 1) * sub)
            meta = xbuf[slot, rows, D_MODEL:]
            row_grp = jnp.sum(jnp.where(lane == LANE_GROUP, meta, 0.0), axis=-1, keepdims=True)
            member = row_grp == grp.astype(F32)
            hm = hm_ref[rows, :]
            y = jnp.zeros((sub, D_MODEL), F32)
            for e in range(EXPERTS_PER_GROUP):
                w = jnp.sum(jnp.where(lane == e, meta, 0.0), axis=-1, keepdims=True)
                w = jnp.where(member, w, 0.0)
                a = _dot(hm, wg_ref[e])
                after_dot(r)
                b = _dot(hm, wu_ref[e])
                after_dot(r)
                u = (a * _sigmoid(a)) * b * w
                y = y + _dot(u.astype(BF16), wd_ref[e])
                after_dot(r)
            yacc_ref[rows, :] += y
        if issue_dma:
            assert done == [ts, ts], done

    @pl.when(v == 0)
    def _():
        gather_rows(inv_cur_ref, 0, 0, ts)

    @pl.when((v == 0) | (grp != vg_ref[jnp.maximum(v - 1, 0)]))
    def _():
        wg_ref[...] = wg32_ref[...].astype(BF16)
        wu_ref[...] = wu32_ref[...].astype(BF16)
        wd_ref[...] = wd32_ref[...].astype(BF16)

    @pl.when(first)
    def _():
        gather_wait(slot)

        @pl.when(t == 0)
        def _():
            gather_rows(inv_nxt_ref, 1, 0, ts)

        hm_ref[...] = _rms(xbuf[slot, :, :D_MODEL], gm_ref[...]).astype(BF16)
        yacc_ref[...] = jnp.zeros_like(yacc_ref)

    @pl.when(overlapped)
    def _():
        experts(True)

    @pl.when(((flags & VISIT_VALID) != 0) & jnp.logical_not(overlapped))
    def _():
        experts(False)

    @pl.when((flags & VISIT_LAST) != 0)
    def _():
        @pl.when(t >= 2)
        def _():
            scatter_wait(slot)

        obuf[slot] = _rms(xbuf[slot, :, :D_MODEL] + yacc_ref[...], gf_ref[...])

    @pl.when(v == pl.num_programs(0) - 1)
    def _():
        last_slot = (nt - 1) % 2
        scatter_rows(inv_cur_ref, last_slot, 0, ts)
        scatter_wait(0)
        scatter_wait(1)
        gather_wait(1 - last_slot)


def _moe_sorted(xa, inv, vt, vg, vf, gm, wg, wu, wd, gf, *, ts, sub):
    N = xa.shape[0]
    nt = N // ts
    assert nt >= 2
    E, D, FF = EXPERTS_PER_GROUP, D_MODEL, EXPERT_FF
    grp = lambda v, vt, vg, vf: (vg[v], 0, 0)
    const = lambda v, vt, vg, vf: (0, 0)
    smem_row = lambda fn: pl.BlockSpec((None, 1, ts), fn, memory_space=pltpu.SMEM)
    grid_spec = pltpu.PrefetchScalarGridSpec(
        num_scalar_prefetch=3,
        grid=(vt.shape[0],),
        in_specs=[
            smem_row(lambda v, vt, vg, vf: (vt[v], 0, 0)),
            smem_row(lambda v, vt, vg, vf: (jnp.minimum(vt[v] + 1, nt - 1), 0, 0)),
            smem_row(lambda v, vt, vg, vf: (jnp.maximum(vt[v] - 1, 0), 0, 0)),
            pl.BlockSpec(memory_space=pl.ANY),
            pl.BlockSpec((1, D), const),
            pl.BlockSpec((E, D, FF), grp), pl.BlockSpec((E, D, FF), grp),
            pl.BlockSpec((E, FF, D), grp),
            pl.BlockSpec((1, D), const),
        ],
        out_specs=pl.BlockSpec(memory_space=pl.ANY),
        scratch_shapes=[
            pltpu.VMEM((2, ts, XA_WIDTH), F32), pltpu.VMEM((2, ts, D), F32),
            pltpu.VMEM((ts, D), BF16), pltpu.VMEM((ts, D), F32),
            pltpu.VMEM((E, D, FF), BF16), pltpu.VMEM((E, D, FF), BF16), pltpu.VMEM((E, FF, D), BF16),
            pltpu.SemaphoreType.DMA((2,)), pltpu.SemaphoreType.DMA((2,)),
        ],
    )
    return pl.pallas_call(
        functools.partial(_moe_body, ts=ts, nt=nt, sub=sub),
        grid_spec=grid_spec,
        out_shape=jax.ShapeDtypeStruct((N, D), F32),
        compiler_params=pltpu.CompilerParams(
            dimension_semantics=("arbitrary",), vmem_limit_bytes=VMEM_LIMIT),
        name="moe_sorted",
    )(vt, vg, vf, inv, inv, inv, xa, gm, wg, wu, wd, gf)


def kernel(x, mem, norm_mix_g, w_in, fox_bf, conv_dw_w, conv_dw_b, conv_ln_g, conv_ln_b, conv_pw_w, attn_branch_w, w_out, norm_xa_g, norm_mem_g, xa_wq, xa_wk, xa_wv, xa_wo, norm_moe_g, router_group_w, router_group_b, router_expert_w, router_expert_b, expert_w_gate, expert_w_up, expert_w_down, norm_final_g):
    B, T, D = x.shape
    depth = w_in.shape[0]
    assert depth == 1, "the final norm is fused into the last layer's MoE kernel"
    row = lambda v: v.reshape(1, -1).astype(F32)
    for l in range(depth):
        u, gc, ga, qkv_t, f_t = _proj_in(x, row(norm_mix_g[l]), jnp.swapaxes(w_in, 1, 2).astype(F32),
                                         layer=l, tm=512)
        qaux, kaux = _fox_prep(f_t, fox_bf[l].reshape(FOX_HEADS, 1).astype(F32))
        attn_t = _fox_attn(qkv_t, qaux, kaux, tq=1024, heads=1)
        x = _mix_out(x, u, gc, ga, attn_t, conv_dw_w[l].astype(F32), row(conv_dw_b[l]),
                     row(conv_ln_g[l]), row(conv_ln_b[l]), conv_pw_w[l].astype(F32),
                     attn_branch_w[l].astype(F32), w_out[l].astype(F32), tc=512, rb=512)
        kx, vx_t = _mem_kv(mem, row(norm_mem_g[l]), xa_wk[l].astype(BF16),
                           xa_wv[l].T.astype(BF16))
        wr_t = jnp.zeros((ROUTE_ROWS, D), F32).at[:N_GROUPS].set(router_group_w[l].T)
        br_t = jnp.zeros((ROUTE_ROWS, 1), F32).at[:N_GROUPS, 0].set(router_group_b[l])
        for g in range(N_GROUPS):
            r0, e0 = SUBLANES * (g + 1), EXPERTS_PER_GROUP * g
            wr_t = wr_t.at[r0:r0 + EXPERTS_PER_GROUP].set(
                router_expert_w[l][:, e0:e0 + EXPERTS_PER_GROUP].T)
            br_t = br_t.at[r0:r0 + EXPERTS_PER_GROUP, 0].set(
                router_expert_b[l][e0:e0 + EXPERTS_PER_GROUP])
        xa, cnt, code = _xattn_route(x, row(norm_xa_g[l]), xa_wq[l].T.astype(BF16), kx, vx_t,
                                     xa_wo[l].astype(BF16), row(norm_moe_g[l]),
                                     wr_t.astype(BF16), br_t, tx=1024, row_blocks=1)
        xa = xa.reshape(B * T, XA_WIDTH)
        inv, vt, vg, vf = _moe_plan(code[:, :, LANE_CODE, :].reshape(B * T), cnt, ts=512)
        x = _moe_sorted(xa, inv, vt, vg, vf, row(norm_moe_g[l]), expert_w_gate[l].astype(F32),
                        expert_w_up[l].astype(F32), expert_w_down[l].astype(F32),
                        row(norm_final_g), ts=512, sub=512).reshape(B, T, D)
    return x
```

```python
import functools

import jax
import jax.numpy as jnp
from jax import lax
from jax.experimental import pallas as pl
from jax.experimental.pallas import tpu as pltpu

D_MODEL = 1024
MEM_LEN = 256
EPS = 1e-6
CONV_DIM = 512
CONV_WIDTH = 31
FOX_HEADS = 8
FOX_HEAD_DIM = 64
FOX_DIM = FOX_HEADS * FOX_HEAD_DIM
XA_HEADS = 4
XA_HEAD_DIM = 128
XA_DIM = XA_HEADS * XA_HEAD_DIM
N_GROUPS = 4
EXPERTS_PER_GROUP = 4
N_EXPERTS = N_GROUPS * EXPERTS_PER_GROUP
EXPERT_FF = 256

AUX_ROWS = 16
LOG2E = 1.4426950408889634
VMEM_LIMIT = 56 * 1024 * 1024

F32 = jnp.float32
BF16 = jnp.bfloat16


def _const_spec(shape):
    n = len(shape)
    return pl.BlockSpec(shape, lambda *_: (0,) * n)


def _rms(x, g):
    return x * lax.rsqrt(jnp.mean(x * x, axis=-1, keepdims=True) + EPS) * g


def _sigmoid(x):
    return 0.5 * jnp.tanh(0.5 * x) + 0.5


def _dot(a, b):
    return jnp.dot(a, b, preferred_element_type=F32)


def _dot_nt(a, b):
    return lax.dot_general(a, b, (((1,), (1,)), ((), ())), preferred_element_type=F32)


def _dot_tn(a, b):
    return lax.dot_general(a, b, (((0,), (0,)), ((), ())), preferred_element_type=F32)


C_GLU = 2 * CONV_DIM
C_F = C_GLU + 3 * FOX_DIM
C_GC = C_F + FOX_HEADS
C_GA = C_GC + D_MODEL


def _proj_in_body(x_ref, g_ref, w_ref, u_ref, gc_ref, ga_ref, qkv_t_ref, f_t_ref,
                  wglu_ref, wgc_ref, wga_ref, wqkv_t_ref, wf_t_ref):
    @pl.when((pl.program_id(0) == 0) & (pl.program_id(1) == 0))
    def _():
        for j in range(C_GLU // D_MODEL):
            rows = slice(j * D_MODEL, (j + 1) * D_MODEL)
            wglu_ref[:, rows] = w_ref[rows, :].T.astype(BF16)
        wgc_ref[...] = w_ref[C_GC:C_GA, :].T.astype(BF16)
        wga_ref[...] = w_ref[C_GA:, :].T.astype(BF16)
        wqkv_t_ref[...] = w_ref[C_GLU:C_F, :].astype(BF16)
        wf_t_ref[...] = w_ref[C_F:C_GC, :].astype(BF16)

    h = _rms(x_ref[...], g_ref[...]).astype(BF16)
    glu = _dot(h, wglu_ref[...])
    u_ref[...] = glu[:, :CONV_DIM] * _sigmoid(glu[:, CONV_DIM:])
    gc_ref[...] = _dot(h, wgc_ref[...]).astype(BF16)
    ga_ref[...] = _dot(h, wga_ref[...]).astype(BF16)
    qkv_t = _dot_nt(wqkv_t_ref[...], h)
    scale = FOX_HEAD_DIM ** -0.5 * LOG2E
    qkv_t_ref[:FOX_DIM, :] = (qkv_t[:FOX_DIM] * scale).astype(BF16)
    qkv_t_ref[FOX_DIM:, :] = qkv_t[FOX_DIM:].astype(BF16)
    f_t_ref[...] = _dot_nt(wf_t_ref[...], h)


def _proj_in(x, g, w, *, layer, tm):
    B, T, D = x.shape
    grid = (B, T // tm)
    row = lambda b, i: (b, i, 0)
    col = lambda b, i: (b, 0, i)
    return pl.pallas_call(
        _proj_in_body,
        grid=grid,
        in_specs=[
            pl.BlockSpec((None, tm, D), row),
            _const_spec((1, D)),
            pl.BlockSpec((None,) + w.shape[1:], lambda b, i: (layer, 0, 0),
                         pipeline_mode=pl.Buffered(1)),
        ],
        out_specs=[
            pl.BlockSpec((None, tm, CONV_DIM), row),
            pl.BlockSpec((None, tm, D), row),
            pl.BlockSpec((None, tm, D), row),
            pl.BlockSpec((None, 3 * FOX_DIM, tm), col),
            pl.BlockSpec((None, FOX_HEADS, tm), col),
        ],
        out_shape=[
            jax.ShapeDtypeStruct((B, T, CONV_DIM), F32),
            jax.ShapeDtypeStruct((B, T, D), BF16),
            jax.ShapeDtypeStruct((B, T, D), BF16),
            jax.ShapeDtypeStruct((B, 3 * FOX_DIM, T), BF16),
            jax.ShapeDtypeStruct((B, FOX_HEADS, T), F32),
        ],
        scratch_shapes=[pltpu.VMEM((D, C_GLU), BF16), pltpu.VMEM((D, D), BF16),
                        pltpu.VMEM((D, D), BF16), pltpu.VMEM((3 * FOX_DIM, D), BF16),
                        pltpu.VMEM((FOX_HEADS, D), BF16)],
        compiler_params=pltpu.CompilerParams(
            dimension_semantics=("arbitrary", "arbitrary"), vmem_limit_bytes=VMEM_LIMIT),
        name="proj_in",
    )(x, g, w)


def _split3(c):
    hi = c.astype(BF16)
    r = c - hi.astype(F32)
    mid = r.astype(BF16)
    lo = (r - mid.astype(F32)).astype(BF16)
    return hi, mid, lo


def _fox_prep_body(f_t_ref, bf_ref, qaux_ref, kaux_ref, c_ref, *, chunk):
    T = f_t_ref.shape[-1]
    z = f_t_ref[...] + bf_ref[...]
    logf = jnp.minimum(z, 0.0) - jnp.log1p(jnp.exp(-jnp.abs(z)))
    r_i = lax.broadcasted_iota(jnp.int32, (chunk, chunk), 0)
    c_i = lax.broadcasted_iota(jnp.int32, (chunk, chunk), 1)
    tri = jnp.where(r_i <= c_i, 1.0, 0.0).astype(BF16)
    carry = jnp.zeros((FOX_HEADS, 1), F32)
    for j in range(T // chunk):
        blk = logf[:, j * chunk:(j + 1) * chunk]
        hi, mid, lo = _split3(blk)
        cs = (_dot(lo, tri) + _dot(mid, tri)) + _dot(hi, tri) + carry
        c_ref[:, j * chunk:(j + 1) * chunk] = cs
        carry = cs[:, chunk - 1:chunk]
    row = lax.broadcasted_iota(jnp.int32, (AUX_ROWS, T), 0)
    for h in range(FOX_HEADS):
        hi, mid, lo = _split3(c_ref[h:h + 1, :] * LOG2E)
        pieces = [jnp.broadcast_to(p.astype(F32), (AUX_ROWS, T)) for p in (hi, mid, lo)]
        one = jnp.where(row < 3, 1.0, 0.0)
        val = jnp.where(row == 0, pieces[0], jnp.where(row == 1, pieces[1], pieces[2]))
        val_k = jnp.where(row == 3, pieces[0], jnp.where(row == 4, pieces[1], pieces[2]))
        qaux = jnp.where(row < 3, val, jnp.where(row < 6, 1.0, 0.0))
        kaux = jnp.where(row < 3, one, jnp.where(row < 6, -val_k, 0.0))
        qaux_ref[h] = qaux.astype(BF16)
        kaux_ref[h] = kaux.astype(BF16)


def _fox_prep(f_t, bf):
    B, H, T = f_t.shape
    aux = jax.ShapeDtypeStruct((B, H, AUX_ROWS, T), BF16)
    aux_spec = pl.BlockSpec((None, H, AUX_ROWS, T), lambda b: (b, 0, 0, 0))
    return pl.pallas_call(
        functools.partial(_fox_prep_body, chunk=512),
        grid=(B,),
        in_specs=[pl.BlockSpec((None, H, T), lambda b: (b, 0, 0)), _const_spec((H, 1))],
        out_specs=[aux_spec, aux_spec],
        out_shape=[aux, aux],
        scratch_shapes=[pltpu.VMEM((H, T), F32)],
        compiler_params=pltpu.CompilerParams(dimension_semantics=("parallel",)),
        name="fox_prep",
    )(f_t, bf)


MASK_BLOCK = 256


def _fox_attn_body(q_ref, k_ref, v_ref, qaux_ref, kaux_ref, o_ref, *scratch, tq, heads):
    T = q_ref.shape[-1]
    dh = FOX_HEAD_DIM
    mb = MASK_BLOCK
    nj = tq // mb
    kk = lax.broadcasted_iota(jnp.int32, (mb, mb), 0)
    qq = lax.broadcasted_iota(jnp.int32, (mb, mb), 1)
    causal = kk <= qq
    ones = jnp.where(lax.broadcasted_iota(jnp.int32, (AUX_ROWS, tq), 0) == 0, 1.0, 0.0).astype(BF16)
    sbufs = [scratch[2 * h:2 * h + 2] for h in range(heads)]
    pbufs = scratch[2 * heads:]

    tasks = []
    for qi in range(T // tq):
        tasks.append((qi, qi, True))
        tasks += [(qi, c, False) for c in range(qi)]

    def scores(h, task, j, sb, mc):
        qi, c, diag = task
        c0 = j * mb if diag else 0
        r0 = c * tq + j * mb
        hd = slice(h * dh, (h + 1) * dh)
        cols = slice(qi * tq + c0, (qi + 1) * tq)
        ka = jnp.concatenate([k_ref[hd, r0:r0 + mb], kaux_ref[h, :, r0:r0 + mb]], axis=0)
        qa = jnp.concatenate([q_ref[hd, cols], qaux_ref[h, :, cols]], axis=0)
        s = _dot_tn(ka, qa)
        if diag:
            head = jnp.where(causal, s[:, :mb], -jnp.inf)
            s = head if tq - c0 == mb else jnp.concatenate([head, s[:, mb:]], axis=1)
        sb[j * mb:(j + 1) * mb, c0:] = s
        cm = jnp.max(s, axis=0, keepdims=True)
        if c0:
            cm = jnp.concatenate([jnp.full((1, c0), -jnp.inf, F32), cm], axis=1)
        return cm if mc is None else jnp.maximum(mc, cm)

    def probs(h, task, j, sb, m_new):
        c0 = j * mb if task[2] else 0
        rows = slice(j * mb, (j + 1) * mb)
        pbufs[h][rows, c0:] = jnp.exp2(sb[rows, c0:] - m_new[:, c0:]).astype(BF16)

    def values(h, task):
        _, c, diag = task
        p_ref = pbufs[h]
        va = jnp.concatenate([v_ref[h * dh:(h + 1) * dh, c * tq:(c + 1) * tq], ones], axis=0)
        if not diag:
            return _dot(va, p_ref[...])
        return jnp.concatenate(
            [_dot(va[:, :(bj + 1) * mb], p_ref[:(bj + 1) * mb, bj * mb:(bj + 1) * mb])
             for bj in range(nj)], axis=1)

    hs = range(heads)
    mc = [None] * heads
    for j in range(nj):
        for h in hs:
            mc[h] = scores(h, tasks[0], j, sbufs[h][0], mc[h])
    m_run, acc = [None] * heads, [None] * heads
    for k, task in enumerate(tasks):
        qi, c, diag = task
        nxt = tasks[k + 1] if k + 1 < len(tasks) else None
        m_new = [mc[h] if diag else jnp.maximum(m_run[h], mc[h]) for h in hs]
        mc = [None] * heads
        for j in range(nj):
            for h in hs:
                if nxt is not None:
                    mc[h] = scores(h, nxt, j, sbufs[h][(k + 1) % 2], mc[h])
                probs(h, task, j, sbufs[h][k % 2], m_new[h])
        for h in hs:
            pv = values(h, task)
            acc[h] = pv if diag else jnp.exp2(m_run[h] - m_new[h]) * acc[h] + pv
            if nxt is None or nxt[0] != qi:
                o_ref[h * dh:(h + 1) * dh, qi * tq:(qi + 1) * tq] = (
                    acc[h][:dh] / acc[h][dh:dh + 1]).astype(o_ref.dtype)
        m_run = m_new


def _fox_attn(qkv_t, qaux, kaux, *, tq, heads):
    B, _, T = qkv_t.shape
    H, dh = FOX_HEADS, FOX_HEAD_DIM
    hg = H // heads
    part = lambda off: pl.BlockSpec((None, heads * dh, T), lambda b, g: (b, off * hg + g, 0))
    aux = pl.BlockSpec((None, heads, AUX_ROWS, T), lambda b, g: (b, g, 0, 0))
    return pl.pallas_call(
        functools.partial(_fox_attn_body, tq=tq, heads=heads),
        grid=(B, hg),
        in_specs=[part(0), part(1), part(2), aux, aux],
        out_specs=pl.BlockSpec((None, heads * dh, T), lambda b, g: (b, g, 0)),
        out_shape=jax.ShapeDtypeStruct((B, H * dh, T), BF16),
        scratch_shapes=([pltpu.VMEM((tq, tq), F32)] * (2 * heads)
                        + [pltpu.VMEM((tq, tq), BF16)] * heads),
        compiler_params=pltpu.CompilerParams(
            dimension_semantics=("parallel", "parallel"), vmem_limit_bytes=VMEM_LIMIT),
        name="fox_attn",
    )(qkv_t, qkv_t, qkv_t, qaux, kaux)


HALO = 32
SUBLANES = 8


def _mix_out_body(x_ref, u_ref, uprev_ref, gc_ref, ga_ref, attn_t_ref, dww_ref, dwb_ref,
                  lng_ref, lnb_ref, pw32_ref, abw32_ref, wout32_ref, o_ref, ubuf_ref, ushift_ref,
                  pw_ref, abw_ref, wout_ref, *, tc, rb):
    first = pl.program_id(1) == 0

    @pl.when(first & (pl.program_id(0) == 0))
    def _():
        pw_ref[...] = pw32_ref[...].astype(BF16)
        abw_ref[...] = abw32_ref[...].astype(BF16)
        wout_ref[...] = wout32_ref[...].astype(BF16)

    ubuf_ref[:HALO, :] = jnp.where(first, 0.0, uprev_ref[...])
    ubuf_ref[HALO:, :] = u_ref[...]
    span = tc + HALO - SUBLANES
    for s in range(1, SUBLANES):
        ushift_ref[s - 1, :, :] = ubuf_ref[s:s + span, :]

    def conv_branch(r0):
        acc = jnp.zeros((rb, CONV_DIM), F32)
        for k in range(CONV_WIDTH):
            off = r0 + HALO - (CONV_WIDTH - 1) + k
            s, base = off % SUBLANES, off - off % SUBLANES
            rows = ubuf_ref[base:base + rb, :] if s == 0 else ushift_ref[s - 1, base:base + rb, :]
            acc = acc + rows * dww_ref[k:k + 1, :]
        acc = acc + dwb_ref[...]
        mu = jnp.mean(acc, axis=-1, keepdims=True)
        d = acc - mu
        var = jnp.mean(d * d, axis=-1, keepdims=True)
        y = d * lax.rsqrt(var + EPS) * lng_ref[...] + lnb_ref[...]
        return (y * _sigmoid(y)).astype(BF16)

    def project(r0, y):
        rows = slice(r0, r0 + rb)
        conv_out = _dot(y, pw_ref[...])
        attn_out = _dot_tn(attn_t_ref[:, rows], abw_ref[...])
        merged = (_sigmoid(gc_ref[rows, :].astype(F32)) * conv_out
                  + _sigmoid(ga_ref[rows, :].astype(F32)) * attn_out)
        o_ref[rows, :] = x_ref[rows, :] + _dot(merged.astype(BF16), wout_ref[...])

    nb = tc // rb
    y = conv_branch(0)
    for j in range(nb):
        y_next = conv_branch((j + 1) * rb) if j + 1 < nb else None
        project(j * rb, y)
        y = y_next


def _mix_out(x, u, gc, ga, attn_t, dww, dwb, lng, lnb, pw, abw, wout, *, tc, rb):
    B, T, D = x.shape
    row = lambda b, i: (b, i, 0)
    halo_blocks = tc // HALO
    return pl.pallas_call(
        functools.partial(_mix_out_body, tc=tc, rb=rb),
        grid=(B, T // tc),
        in_specs=[
            pl.BlockSpec((None, tc, D), row),
            pl.BlockSpec((None, tc, CONV_DIM), row),
            pl.BlockSpec((None, HALO, CONV_DIM),
                         lambda b, i: (b, jnp.maximum(i * halo_blocks - 1, 0), 0)),
            pl.BlockSpec((None, tc, D), row),
            pl.BlockSpec((None, tc, D), row),
            pl.BlockSpec((None, FOX_DIM, tc), lambda b, i: (b, 0, i)),
            _const_spec(dww.shape), _const_spec(dwb.shape), _const_spec(lng.shape),
            _const_spec(lnb.shape), _const_spec(pw.shape), _const_spec(abw.shape),
            _const_spec(wout.shape),
        ],
        out_specs=pl.BlockSpec((None, tc, D), row),
        out_shape=jax.ShapeDtypeStruct((B, T, D), F32),
        scratch_shapes=[pltpu.VMEM((HALO + tc, CONV_DIM), F32),
                        pltpu.VMEM((SUBLANES - 1, HALO + tc - SUBLANES, CONV_DIM), F32),
                        pltpu.VMEM(pw.shape, BF16), pltpu.VMEM(abw.shape, BF16),
                        pltpu.VMEM(wout.shape, BF16)],
        compiler_params=pltpu.CompilerParams(
            dimension_semantics=("arbitrary", "arbitrary"), vmem_limit_bytes=VMEM_LIMIT),
        name="mix_out",
    )(x, u, u, gc, ga, attn_t, dww, dwb, lng, lnb, pw, abw, wout)


def _mem_kv_body(m_ref, g_ref, wk_ref, wv_t_ref, k_ref, v_t_ref):
    h = _rms(m_ref[...], g_ref[...]).astype(BF16)
    k_ref[...] = _dot(h, wk_ref[...]).astype(BF16)
    v_t_ref[...] = _dot_nt(wv_t_ref[...], h).astype(BF16)


def _mem_kv(mem, g, wk, wv_t):
    B, M, D = mem.shape
    return pl.pallas_call(
        _mem_kv_body,
        grid=(B,),
        in_specs=[pl.BlockSpec((None, M, D), lambda b: (b, 0, 0)), _const_spec((1, D)),
                  _const_spec(wk.shape), _const_spec(wv_t.shape)],
        out_specs=[pl.BlockSpec((None, M, XA_DIM), lambda b: (b, 0, 0)),
                   pl.BlockSpec((None, XA_DIM, M), lambda b: (b, 0, 0))],
        out_shape=[jax.ShapeDtypeStruct((B, M, XA_DIM), BF16),
                   jax.ShapeDtypeStruct((B, XA_DIM, M), BF16)],
        compiler_params=pltpu.CompilerParams(dimension_semantics=("parallel",)),
        name="mem_kv",
    )(mem, g, wk, wv_t)


META_LANES = 128
XA_WIDTH = D_MODEL + META_LANES
LANE_GROUP = EXPERTS_PER_GROUP
LANE_RANK = EXPERTS_PER_GROUP + 1
LANE_CODE = EXPERTS_PER_GROUP + 2
CODE_ROWS = SUBLANES
ROUTE_ROWS = SUBLANES * (N_GROUPS + 1)
CUMSUM_BLOCK = 256


def _first_max_row(v, valid, row):
    neg = jnp.where(valid, v, -jnp.inf)
    mx = jnp.max(neg, axis=0, keepdims=True)
    idx = jnp.min(jnp.where(valid & (neg == mx), row, SUBLANES), axis=0, keepdims=True)
    return mx, idx


def _route(logits_t, run):
    n = logits_t.shape[1]
    row = lax.broadcasted_iota(jnp.int32, (SUBLANES, n), 0)
    valid = row < N_GROUPS
    glog = logits_t[:SUBLANES]
    gmax, g_star = _first_max_row(glog, valid, row)
    p_gsel = 1.0 / jnp.sum(jnp.where(valid, jnp.exp(glog - gmax), 0.0), axis=0, keepdims=True)
    elog = logits_t[SUBLANES:2 * SUBLANES]
    for g in range(1, N_GROUPS):
        elog = jnp.where(g_star == g, logits_t[(g + 1) * SUBLANES:(g + 2) * SUBLANES], elog)
    emax, _ = _first_max_row(elog, valid, row)
    ex = jnp.where(valid, jnp.exp(elog - emax), 0.0)
    p_exp = ex / jnp.sum(ex, axis=0, keepdims=True)
    v1, i1 = _first_max_row(p_exp, valid, row)
    v2, i2 = _first_max_row(p_exp, valid & (row != i1), row)
    tot = v1 + v2
    w = jnp.where(row == i1, p_gsel * (v1 / tot), jnp.where(row == i2, p_gsel * (v2 / tot), 0.0))
    onehot = jnp.where(row == g_star, 1.0, 0.0).astype(BF16)
    cb = min(n, CUMSUM_BLOCK)
    r_i = lax.broadcasted_iota(jnp.int32, (cb, cb), 0)
    c_i = lax.broadcasted_iota(jnp.int32, (cb, cb), 1)
    tri = jnp.where(r_i <= c_i, 1.0, 0.0).astype(BF16)
    pieces = []
    for j in range(n // cb):
        piece = _dot(onehot[:, j * cb:(j + 1) * cb], tri) + run
        pieces.append(piece)
        run = piece[:, cb - 1:cb]
    cum = jnp.concatenate(pieces, axis=1)
    rank = jnp.sum(jnp.where(row == g_star, cum, 0.0), axis=0, keepdims=True) - 1.0
    g_f = g_star.astype(F32)
    meta_t = jnp.where(valid, w, jnp.where(row == LANE_GROUP, g_f, jnp.where(
        row == LANE_RANK, rank, jnp.where(row == LANE_CODE, rank * N_GROUPS + g_f, 0.0))))
    return meta_t, cum[:, n - 1:n]


def _xattn_body(x_ref, g_ref, wq_t_ref, k_ref, v_t_ref, wo_ref, gm_ref, wr_t_ref, br_t_ref,
                xa_ref, cnt_ref, code_ref, run_ref, *, row_blocks):
    @pl.when((pl.program_id(0) == 0) & (pl.program_id(1) == 0))
    def _():
        run_ref[...] = jnp.zeros_like(run_ref)

    tx = x_ref.shape[0]
    hb = tx // row_blocks
    scale = XA_HEAD_DIM ** -0.5

    def attend(r0):
        rows = slice(r0, r0 + hb)
        x = x_ref[rows, :]
        q_t = _dot_nt(wq_t_ref[...], _rms(x, g_ref[...]).astype(BF16)).astype(BF16)
        yield
        head_rows = [slice(h * XA_HEAD_DIM, (h + 1) * XA_HEAD_DIM) for h in range(XA_HEADS)]
        scores = [_dot(k_ref[:, sl], q_t[sl, :]) * scale for sl in head_rows]
        yield
        heads = []
        for sl, s_t in zip(head_rows, scores):
            p = jnp.exp(s_t - jnp.max(s_t, axis=0, keepdims=True))
            l = jnp.sum(p, axis=0, keepdims=True)
            heads.append((_dot(v_t_ref[sl, :], p.astype(BF16)) / l).astype(BF16))
            yield
        x2 = x + _dot_tn(jnp.concatenate(heads, axis=0), wo_ref[...])
        xa_ref[rows, :D_MODEL] = x2
        yield
        hm = _rms(x2, gm_ref[...]).astype(BF16)
        logits[r0] = _dot_nt(wr_t_ref[...], hm) + br_t_ref[...]
        yield

    logits = {}
    blocks = [attend(r0) for r0 in range(0, tx, hb)]
    for _ in range(XA_HEADS + 4):
        for blk in blocks:
            next(blk)
    run = run_ref[...]
    for r0 in range(0, tx, hb):
        meta_t, run = _route(logits[r0], run)
        code_ref[:, r0:r0 + hb] = meta_t.astype(jnp.int32)
        xa_ref[r0:r0 + hb, D_MODEL:] = jnp.concatenate(
            [meta_t, jnp.zeros((META_LANES - CODE_ROWS, hb), F32)], axis=0).T
    run_ref[...] = run
    cnt_ref[...] = jnp.broadcast_to(run, cnt_ref.shape).astype(jnp.int32)


def _xattn_route(x, g, wq_t, kx, vx_t, wo, gm, wr_t, br_t, *, tx, row_blocks):
    B, T, D = x.shape
    row = lambda b, i: (b, i, 0)
    return pl.pallas_call(
        functools.partial(_xattn_body, row_blocks=row_blocks),
        grid=(B, T // tx),
        in_specs=[pl.BlockSpec((None, tx, D), row), _const_spec((1, D)),
                  _const_spec(wq_t.shape),
                  pl.BlockSpec((None, MEM_LEN, XA_DIM), lambda b, i: (b, 0, 0)),
                  pl.BlockSpec((None, XA_DIM, MEM_LEN), lambda b, i: (b, 0, 0)),
                  _const_spec(wo.shape), _const_spec((1, D)),
                  _const_spec(wr_t.shape), _const_spec(br_t.shape)],
        out_specs=[pl.BlockSpec((None, tx, XA_WIDTH), row), _const_spec((CODE_ROWS, META_LANES)),
                   pl.BlockSpec((None, None, CODE_ROWS, tx), lambda b, i: (b, i, 0, 0))],
        out_shape=[jax.ShapeDtypeStruct((B, T, XA_WIDTH), F32),
                   jax.ShapeDtypeStruct((CODE_ROWS, META_LANES), jnp.int32),
                   jax.ShapeDtypeStruct((B, T // tx, CODE_ROWS, tx), jnp.int32)],
        scratch_shapes=[pltpu.VMEM((CODE_ROWS, 1), F32)],
        compiler_params=pltpu.CompilerParams(
            dimension_semantics=("arbitrary", "arbitrary"), vmem_limit_bytes=VMEM_LIMIT),
        name="xattn_route",
    )(x, g, wq_t, kx, vx_t, wo, gm, wr_t, br_t)


VISIT_FIRST, VISIT_LAST, VISIT_VALID = 1, 2, 4


def _moe_plan_body(code_ref, cnt_ref, inv_ref, vt_ref, vg_ref, vf_ref, offs_ref, *, ts):
    n = code_ref.shape[0]
    nt, nv = n // ts, vt_ref.shape[0]
    ends = [cnt_ref[0, 0]]
    for g in range(1, N_GROUPS - 1):
        ends.append(ends[-1] + cnt_ref[g, 0])
    offs_ref[0] = 0
    for g in range(1, N_GROUPS):
        offs_ref[g] = ends[g - 1]
    group_bits = N_GROUPS.bit_length() - 1
    assert 1 << group_bits == N_GROUPS

    def place(t, _):
        code = code_ref[t]
        slot = offs_ref[code & (N_GROUPS - 1)] + lax.shift_right_logical(code, group_bits)
        inv_ref[slot] = t
        return 0

    lax.fori_loop(0, n, place, 0, unroll=8)

    def group_of(slot):
        return sum(jnp.where(e <= slot, 1, 0) for e in ends)

    def tile(j, v):
        g_lo, g_hi = group_of(j * ts), group_of(j * ts + ts - 1)
        for k in range(N_GROUPS):
            @pl.when(k <= g_hi - g_lo)
            def _():
                vt_ref[v + k] = j
                vg_ref[v + k] = g_lo + k
                vf_ref[v + k] = (VISIT_VALID + (VISIT_FIRST if k == 0 else 0)
                                 + jnp.where(k == g_hi - g_lo, VISIT_LAST, 0))
        return v + g_hi - g_lo + 1

    used = lax.fori_loop(0, nt, tile, 0)
    g_last = group_of(n - 1)

    def pad(v, _):
        vt_ref[v] = nt - 1
        vg_ref[v] = g_last
        vf_ref[v] = 0
        return 0

    lax.fori_loop(used, nv, pad, 0)


def _moe_plan(code, cnt, *, ts):
    n = code.shape[0]
    nt = n // ts
    nv = nt + N_GROUPS - 1
    smem = pl.BlockSpec(memory_space=pltpu.SMEM)
    visits = jax.ShapeDtypeStruct((nv,), jnp.int32)
    inv, vt, vg, vf = pl.pallas_call(
        functools.partial(_moe_plan_body, ts=ts),
        in_specs=[smem, smem],
        out_specs=[smem, smem, smem, smem],
        out_shape=[jax.ShapeDtypeStruct((n,), jnp.int32), visits, visits, visits],
        scratch_shapes=[pltpu.SMEM((N_GROUPS,), jnp.int32)],
        name="moe_plan",
    )(code, cnt)
    return inv.reshape(nt, 1, ts), vt, vg, vf


def _moe_body(vt_ref, vg_ref, vf_ref, inv_cur_ref, inv_nxt_ref, inv_prv_ref, xa_hbm, gm_ref,
              wg32_ref, wu32_ref, wd32_ref, gf_ref, out_hbm, xbuf, obuf, hm_ref, yacc_ref,
              wg_ref, wu_ref, wd_ref, gsem, ssem, *, ts, nt, sub):
    v = pl.program_id(0)
    t, grp, flags = vt_ref[v], vg_ref[v], vf_ref[v]
    slot = lax.rem(t, 2)
    first = (flags & VISIT_FIRST) != 0
    overlapped = first & (t >= 1)

    def gather_rows(idx_ref, dst_slot, lo, hi):
        for r in range(lo, min(hi, ts)):
            pltpu.make_async_copy(xa_hbm.at[pl.ds(idx_ref[0, r], 1)],
                                  xbuf.at[dst_slot, pl.ds(r, 1)], gsem.at[dst_slot]).start()

    def gather_wait(dst_slot):
        pltpu.make_async_copy(xa_hbm.at[pl.ds(0, ts)], xbuf.at[dst_slot], gsem.at[dst_slot]).wait()

    def scatter_rows(idx_ref, src_slot, lo, hi):
        for r in range(lo, min(hi, ts)):
            pltpu.make_async_copy(obuf.at[src_slot, pl.ds(r, 1)],
                                  out_hbm.at[pl.ds(idx_ref[0, r], 1)], ssem.at[src_slot]).start()

    def scatter_wait(src_slot):
        pltpu.make_async_copy(obuf.at[src_slot], out_hbm.at[pl.ds(0, ts)], ssem.at[src_slot]).wait()

    def experts(issue_dma):
        lane = lax.broadcasted_iota(jnp.int32, (sub, META_LANES), 1)
        n_sub = ts // sub
        half = n_sub * 3 * EXPERTS_PER_GROUP // 2
        batch = -(-ts // half)
        done = [0, 0]
        dots = [0]

        def after_dot(r):
            if not issue_dma:
                return
            kind = 0 if dots[0] < half else 1
            dots[0] += 1
            lo = done[kind]
            if kind == 0:
                gather_rows(inv_nxt_ref, 1 - slot, lo, lo + batch)
            else:
                scatter_rows(inv_prv_ref, 1 - slot, lo, lo + batch)
            done[kind] = min(lo + batch, ts)

        for r in range(n_sub):
            rows = slice(r * sub, (r + 1) * sub)
            meta = xbuf[slot, rows, D_MODEL:]
            row_grp = jnp.sum(jnp.where(lane == LANE_GROUP, meta, 0.0), axis=-1, keepdims=True)
            member = row_grp == grp.astype(F32)
            hm = hm_ref[rows, :]
            y = jnp.zeros((sub, D_MODEL), F32)
            for e in range(EXPERTS_PER_GROUP):
                w = jnp.sum(jnp.where(lane == e, meta, 0.0), axis=-1, keepdims=True)
                w = jnp.where(member, w, 0.0)
                a = _dot(hm, wg_ref[e])
                after_dot(r)
                b = _dot(hm, wu_ref[e])
                after_dot(r)
                u = (a * _sigmoid(a)) * b * w
                y = y + _dot(u.astype(BF16), wd_ref[e])
                after_dot(r)
            yacc_ref[rows, :] += y
        if issue_dma:
            assert done == [ts, ts], done

    @pl.when(v == 0)
    def _():
        gather_rows(inv_cur_ref, 0, 0, ts)

    @pl.when((v == 0) | (grp != vg_ref[jnp.maximum(v - 1, 0)]))
    def _():
        wg_ref[...] = wg32_ref[...].astype(BF16)
        wu_ref[...] = wu32_ref[...].astype(BF16)
        wd_ref[...] = wd32_ref[...].astype(BF16)

    @pl.when(first)
    def _():
        gather_wait(slot)

        @pl.when(t == 0)
        def _():
            gather_rows(inv_nxt_ref, 1, 0, ts)

        hm_ref[...] = _rms(xbuf[slot, :, :D_MODEL], gm_ref[...]).astype(BF16)
        yacc_ref[...] = jnp.zeros_like(yacc_ref)

    @pl.when(overlapped)
    def _():
        experts(True)

    @pl.when(((flags & VISIT_VALID) != 0) & jnp.logical_not(overlapped))
    def _():
        experts(False)

    @pl.when((flags & VISIT_LAST) != 0)
    def _():
        @pl.when(t >= 2)
        def _():
            scatter_wait(slot)

        obuf[slot] = _rms(xbuf[slot, :, :D_MODEL] + yacc_ref[...], gf_ref[...])

    @pl.when(v == pl.num_programs(0) - 1)
    def _():
        last_slot = (nt - 1) % 2
        scatter_rows(inv_cur_ref, last_slot, 0, ts)
        scatter_wait(0)
        scatter_wait(1)
        gather_wait(1 - last_slot)


def _moe_sorted(xa, inv, vt, vg, vf, gm, wg, wu, wd, gf, *, ts, sub):
    N = xa.shape[0]
    nt = N // ts
    assert nt >= 2
    E, D, FF = EXPERTS_PER_GROUP, D_MODEL, EXPERT_FF
    grp = lambda v, vt, vg, vf: (vg[v], 0, 0)
    const = lambda v, vt, vg, vf: (0, 0)
    smem_row = lambda fn: pl.BlockSpec((None, 1, ts), fn, memory_space=pltpu.SMEM)
    grid_spec = pltpu.PrefetchScalarGridSpec(
        num_scalar_prefetch=3,
        grid=(vt.shape[0],),
        in_specs=[
            smem_row(lambda v, vt, vg, vf: (vt[v], 0, 0)),
            smem_row(lambda v, vt, vg, vf: (jnp.minimum(vt[v] + 1, nt - 1), 0, 0)),
            smem_row(lambda v, vt, vg, vf: (jnp.maximum(vt[v] - 1, 0), 0, 0)),
            pl.BlockSpec(memory_space=pl.ANY),
            pl.BlockSpec((1, D), const),
            pl.BlockSpec((E, D, FF), grp), pl.BlockSpec((E, D, FF), grp),
            pl.BlockSpec((E, FF, D), grp),
            pl.BlockSpec((1, D), const),
        ],
        out_specs=pl.BlockSpec(memory_space=pl.ANY),
        scratch_shapes=[
            pltpu.VMEM((2, ts, XA_WIDTH), F32), pltpu.VMEM((2, ts, D), F32),
            pltpu.VMEM((ts, D), BF16), pltpu.VMEM((ts, D), F32),
            pltpu.VMEM((E, D, FF), BF16), pltpu.VMEM((E, D, FF), BF16), pltpu.VMEM((E, FF, D), BF16),
            pltpu.SemaphoreType.DMA((2,)), pltpu.SemaphoreType.DMA((2,)),
        ],
    )
    return pl.pallas_call(
        functools.partial(_moe_body, ts=ts, nt=nt, sub=sub),
        grid_spec=grid_spec,
        out_shape=jax.ShapeDtypeStruct((N, D), F32),
        compiler_params=pltpu.CompilerParams(
            dimension_semantics=("arbitrary",), vmem_limit_bytes=VMEM_LIMIT),
        name="moe_sorted",
    )(vt, vg, vf, inv, inv, inv, xa, gm, wg, wu, wd, gf)


def kernel(x, mem, norm_mix_g, w_in, fox_bf, conv_dw_w, conv_dw_b, conv_ln_g, conv_ln_b, conv_pw_w, attn_branch_w, w_out, norm_xa_g, norm_mem_g, xa_wq, xa_wk, xa_wv, xa_wo, norm_moe_g, router_group_w, router_group_b, router_expert_w, router_expert_b, expert_w_gate, expert_w_up, expert_w_down, norm_final_g):
    B, T, D = x.shape
    depth = w_in.shape[0]
    assert depth == 1, "the final norm is fused into the last layer's MoE kernel"
    row = lambda v: v.reshape(1, -1).astype(F32)
    for l in range(depth):
        u, gc, ga, qkv_t, f_t = _proj_in(x, row(norm_mix_g[l]), jnp.swapaxes(w_in, 1, 2).astype(F32),
                                         layer=l, tm=512)
        qaux, kaux = _fox_prep(f_t, fox_bf[l].reshape(FOX_HEADS, 1).astype(F32))
        attn_t = _fox_attn(qkv_t, qaux, kaux, tq=512, heads=1)
        x = _mix_out(x, u, gc, ga, attn_t, conv_dw_w[l].astype(F32), row(conv_dw_b[l]),
                     row(conv_ln_g[l]), row(conv_ln_b[l]), conv_pw_w[l].astype(F32),
                     attn_branch_w[l].astype(F32), w_out[l].astype(F32), tc=512, rb=512)
        kx, vx_t = _mem_kv(mem, row(norm_mem_g[l]), xa_wk[l].astype(BF16),
                           xa_wv[l].T.astype(BF16))
        wr_t = jnp.zeros((ROUTE_ROWS, D), F32).at[:N_GROUPS].set(router_group_w[l].T)
        br_t = jnp.zeros((ROUTE_ROWS, 1), F32).at[:N_GROUPS, 0].set(router_group_b[l])
        for g in range(N_GROUPS):
            r0, e0 = SUBLANES * (g + 1), EXPERTS_PER_GROUP * g
            wr_t = wr_t.at[r0:r0 + EXPERTS_PER_GROUP].set(
                router_expert_w[l][:, e0:e0 + EXPERTS_PER_GROUP].T)
            br_t = br_t.at[r0:r0 + EXPERTS_PER_GROUP, 0].set(
                router_expert_b[l][e0:e0 + EXPERTS_PER_GROUP])
        xa, cnt, code = _xattn_route(x, row(norm_xa_g[l]), xa_wq[l].T.astype(BF16), kx, vx_t,
                                     xa_wo[l].astype(BF16), row(norm_moe_g[l]),
                                     wr_t.astype(BF16), br_t, tx=1024, row_blocks=1)
        xa = xa.reshape(B * T, XA_WIDTH)
        inv, vt, vg, vf = _moe_plan(code[:, :, LANE_CODE, :].reshape(B * T), cnt, ts=512)
        x = _moe_sorted(xa, inv, vt, vg, vf, row(norm_moe_g[l]), expert_w_gate[l].astype(F32),
                        expert_w_up[l].astype(F32), expert_w_down[l].astype(F32),
                        row(norm_final_g), ts=512, sub=512).reshape(B, T, D)
    return x
```

```python
import functools

import jax
import jax.numpy as jnp
from jax import lax
from jax.experimental import pallas as pl
from jax.experimental.pallas import tpu as pltpu

D_MODEL = 1024
MEM_LEN = 256
EPS = 1e-6
CONV_DIM = 512
CONV_WIDTH = 31
FOX_HEADS = 8
FOX_HEAD_DIM = 64
FOX_DIM = FOX_HEADS * FOX_HEAD_DIM
XA_HEADS = 4
XA_HEAD_DIM = 128
XA_DIM = XA_HEADS * XA_HEAD_DIM
N_GROUPS = 4
EXPERTS_PER_GROUP = 4
N_EXPERTS = N_GROUPS * EXPERTS_PER_GROUP
EXPERT_FF = 256

AUX_ROWS = 16
LOG2E = 1.4426950408889634
VMEM_LIMIT = 56 * 1024 * 1024

F32 = jnp.float32
BF16 = jnp.bfloat16


def _const_spec(shape):
    n = len(shape)
    return pl.BlockSpec(shape, lambda *_: (0,) * n)


def _rms(x, g):
    return x * lax.rsqrt(jnp.mean(x * x, axis=-1, keepdims=True) + EPS) * g


def _sigmoid(x):
    return 0.5 * jnp.tanh(0.5 * x) + 0.5


def _dot(a, b):
    return jnp.dot(a, b, preferred_element_type=F32)


def _dot_nt(a, b):
    return lax.dot_general(a, b, (((1,), (1,)), ((), ())), preferred_element_type=F32)


def _dot_tn(a, b):
    return lax.dot_general(a, b, (((0,), (0,)), ((), ())), preferred_element_type=F32)


C_GLU = 2 * CONV_DIM
C_F = C_GLU + 3 * FOX_DIM
C_GC = C_F + FOX_HEADS
C_GA = C_GC + D_MODEL


def _proj_in_body(x_ref, g_ref, w_ref, u_ref, gc_ref, ga_ref, qkv_t_ref, f_t_ref,
                  wglu_ref, wgc_ref, wga_ref, wqkv_t_ref, wf_t_ref):
    @pl.when((pl.program_id(0) == 0) & (pl.program_id(1) == 0))
    def _():
        for j in range(C_GLU // D_MODEL):
            rows = slice(j * D_MODEL, (j + 1) * D_MODEL)
            wglu_ref[:, rows] = w_ref[rows, :].T.astype(BF16)
        wgc_ref[...] = w_ref[C_GC:C_GA, :].T.astype(BF16)
        wga_ref[...] = w_ref[C_GA:, :].T.astype(BF16)
        wqkv_t_ref[...] = w_ref[C_GLU:C_F, :].astype(BF16)
        wf_t_ref[...] = w_ref[C_F:C_GC, :].astype(BF16)

    h = _rms(x_ref[...], g_ref[...]).astype(BF16)
    glu = _dot(h, wglu_ref[...])
    u_ref[...] = glu[:, :CONV_DIM] * _sigmoid(glu[:, CONV_DIM:])
    gc_ref[...] = _dot(h, wgc_ref[...]).astype(BF16)
    ga_ref[...] = _dot(h, wga_ref[...]).astype(BF16)
    qkv_t = _dot_nt(wqkv_t_ref[...], h)
    scale = FOX_HEAD_DIM ** -0.5 * LOG2E
    qkv_t_ref[:FOX_DIM, :] = (qkv_t[:FOX_DIM] * scale).astype(BF16)
    qkv_t_ref[FOX_DIM:, :] = qkv_t[FOX_DIM:].astype(BF16)
    f_t_ref[...] = _dot_nt(wf_t_ref[...], h)


def _proj_in(x, g, w, *, layer, tm):
    B, T, D = x.shape
    grid = (B, T // tm)
    row = lambda b, i: (b, i, 0)
    col = lambda b, i: (b, 0, i)
    return pl.pallas_call(
        _proj_in_body,
        grid=grid,
        in_specs=[
            pl.BlockSpec((None, tm, D), row),
            _const_spec((1, D)),
            pl.BlockSpec((None,) + w.shape[1:], lambda b, i: (layer, 0, 0),
                         pipeline_mode=pl.Buffered(1)),
        ],
        out_specs=[
            pl.BlockSpec((None, tm, CONV_DIM), row),
            pl.BlockSpec((None, tm, D), row),
            pl.BlockSpec((None, tm, D), row),
            pl.BlockSpec((None, 3 * FOX_DIM, tm), col),
            pl.BlockSpec((None, FOX_HEADS, tm), col),
        ],
        out_shape=[
            jax.ShapeDtypeStruct((B, T, CONV_DIM), F32),
            jax.ShapeDtypeStruct((B, T, D), BF16),
            jax.ShapeDtypeStruct((B, T, D), BF16),
            jax.ShapeDtypeStruct((B, 3 * FOX_DIM, T), BF16),
            jax.ShapeDtypeStruct((B, FOX_HEADS, T), F32),
        ],
        scratch_shapes=[pltpu.VMEM((D, C_GLU), BF16), pltpu.VMEM((D, D), BF16),
                        pltpu.VMEM((D, D), BF16), pltpu.VMEM((3 * FOX_DIM, D), BF16),
                        pltpu.VMEM((FOX_HEADS, D), BF16)],
        compiler_params=pltpu.CompilerParams(
            dimension_semantics=("arbitrary", "arbitrary"), vmem_limit_bytes=VMEM_LIMIT),
        name="proj_in",
    )(x, g, w)


def _split3(c):
    hi = c.astype(BF16)
    r = c - hi.astype(F32)
    mid = r.astype(BF16)
    lo = (r - mid.astype(F32)).astype(BF16)
    return hi, mid, lo


def _fox_prep_body(f_t_ref, bf_ref, qaux_ref, kaux_ref, c_ref, *, chunk):
    T = f_t_ref.shape[-1]
    z = f_t_ref[...] + bf_ref[...]
    logf = jnp.minimum(z, 0.0) - jnp.log1p(jnp.exp(-jnp.abs(z)))
    r_i = lax.broadcasted_iota(jnp.int32, (chunk, chunk), 0)
    c_i = lax.broadcasted_iota(jnp.int32, (chunk, chunk), 1)
    tri = jnp.where(r_i <= c_i, 1.0, 0.0).astype(BF16)
    carry = jnp.zeros((FOX_HEADS, 1), F32)
    for j in range(T // chunk):
        blk = logf[:, j * chunk:(j + 1) * chunk]
        hi, mid, lo = _split3(blk)
        cs = (_dot(lo, tri) + _dot(mid, tri)) + _dot(hi, tri) + carry
        c_ref[:, j * chunk:(j + 1) * chunk] = cs
        carry = cs[:, chunk - 1:chunk]
    row = lax.broadcasted_iota(jnp.int32, (AUX_ROWS, T), 0)
    for h in range(FOX_HEADS):
        hi, mid, lo = _split3(c_ref[h:h + 1, :] * LOG2E)
        pieces = [jnp.broadcast_to(p.astype(F32), (AUX_ROWS, T)) for p in (hi, mid, lo)]
        one = jnp.where(row < 3, 1.0, 0.0)
        val = jnp.where(row == 0, pieces[0], jnp.where(row == 1, pieces[1], pieces[2]))
        val_k = jnp.where(row == 3, pieces[0], jnp.where(row == 4, pieces[1], pieces[2]))
        qaux = jnp.where(row < 3, val, jnp.where(row < 6, 1.0, 0.0))
        kaux = jnp.where(row < 3, one, jnp.where(row < 6, -val_k, 0.0))
        qaux_ref[h] = qaux.astype(BF16)
        kaux_ref[h] = kaux.astype(BF16)


def _fox_prep(f_t, bf):
    B, H, T = f_t.shape
    aux = jax.ShapeDtypeStruct((B, H, AUX_ROWS, T), BF16)
    aux_spec = pl.BlockSpec((None, H, AUX_ROWS, T), lambda b: (b, 0, 0, 0))
    return pl.pallas_call(
        functools.partial(_fox_prep_body, chunk=512),
        grid=(B,),
        in_specs=[pl.BlockSpec((None, H, T), lambda b: (b, 0, 0)), _const_spec((H, 1))],
        out_specs=[aux_spec, aux_spec],
        out_shape=[aux, aux],
        scratch_shapes=[pltpu.VMEM((H, T), F32)],
        compiler_params=pltpu.CompilerParams(dimension_semantics=("parallel",)),
        name="fox_prep",
    )(f_t, bf)


MASK_BLOCK = 256


def _fox_attn_body(q_ref, k_ref, v_ref, qaux_ref, kaux_ref, o_ref, *scratch, tq, heads):
    T = q_ref.shape[-1]
    dh = FOX_HEAD_DIM
    mb = MASK_BLOCK
    nj = tq // mb
    kk = lax.broadcasted_iota(jnp.int32, (mb, mb), 0)
    qq = lax.broadcasted_iota(jnp.int32, (mb, mb), 1)
    causal = kk <= qq
    ones = jnp.where(lax.broadcasted_iota(jnp.int32, (AUX_ROWS, tq), 0) == 0, 1.0, 0.0).astype(BF16)
    sbufs = [scratch[2 * h:2 * h + 2] for h in range(heads)]
    pbufs = scratch[2 * heads:]

    tasks = []
    for qi in range(T // tq):
        tasks.append((qi, qi, True))
        tasks += [(qi, c, False) for c in range(qi)]

    def scores(h, task, j, sb, mc):
        qi, c, diag = task
        c0 = j * mb if diag else 0
        r0 = c * tq + j * mb
        hd = slice(h * dh, (h + 1) * dh)
        cols = slice(qi * tq + c0, (qi + 1) * tq)
        ka = jnp.concatenate([k_ref[hd, r0:r0 + mb], kaux_ref[h, :, r0:r0 + mb]], axis=0)
        qa = jnp.concatenate([q_ref[hd, cols], qaux_ref[h, :, cols]], axis=0)
        s = _dot_tn(ka, qa)
        if diag:
            head = jnp.where(causal, s[:, :mb], -jnp.inf)
            s = head if tq - c0 == mb else jnp.concatenate([head, s[:, mb:]], axis=1)
        sb[j * mb:(j + 1) * mb, c0:] = s
        cm = jnp.max(s, axis=0, keepdims=True)
        if c0:
            cm = jnp.concatenate([jnp.full((1, c0), -jnp.inf, F32), cm], axis=1)
        return cm if mc is None else jnp.maximum(mc, cm)

    def probs(h, task, j, sb, m_new):
        c0 = j * mb if task[2] else 0
        rows = slice(j * mb, (j + 1) * mb)
        pbufs[h][rows, c0:] = jnp.exp2(sb[rows, c0:] - m_new[:, c0:]).astype(BF16)

    def values(h, task):
        _, c, diag = task
        p_ref = pbufs[h]
        va = jnp.concatenate([v_ref[h * dh:(h + 1) * dh, c * tq:(c + 1) * tq], ones], axis=0)
        if not diag:
            return _dot(va, p_ref[...])
        return jnp.concatenate(
            [_dot(va[:, :(bj + 1) * mb], p_ref[:(bj + 1) * mb, bj * mb:(bj + 1) * mb])
             for bj in range(nj)], axis=1)

    hs = range(heads)
    mc = [None] * heads
    for j in range(nj):
        for h in hs:
            mc[h] = scores(h, tasks[0], j, sbufs[h][0], mc[h])
    m_run, acc = [None] * heads, [None] * heads
    for k, task in enumerate(tasks):
        qi, c, diag = task
        nxt = tasks[k + 1] if k + 1 < len(tasks) else None
        m_new = [mc[h] if diag else jnp.maximum(m_run[h], mc[h]) for h in hs]
        mc = [None] * heads
        for j in range(nj):
            for h in hs:
                if nxt is not None:
                    mc[h] = scores(h, nxt, j, sbufs[h][(k + 1) % 2], mc[h])
                probs(h, task, j, sbufs[h][k % 2], m_new[h])
        for h in hs:
            pv = values(h, task)
            acc[h] = pv if diag else jnp.exp2(m_run[h] - m_new[h]) * acc[h] + pv
            if nxt is None or nxt[0] != qi:
                o_ref[h * dh:(h + 1) * dh, qi * tq:(qi + 1) * tq] = (
                    acc[h][:dh] / acc[h][dh:dh + 1]).astype(o_ref.dtype)
        m_run = m_new


def _fox_attn(qkv_t, qaux, kaux, *, tq, heads):
    B, _, T = qkv_t.shape
    H, dh = FOX_HEADS, FOX_HEAD_DIM
    hg = H // heads
    part = lambda off: pl.BlockSpec((None, heads * dh, T), lambda b, g: (b, off * hg + g, 0))
    aux = pl.BlockSpec((None, heads, AUX_ROWS, T), lambda b, g: (b, g, 0, 0))
    return pl.pallas_call(
        functools.partial(_fox_attn_body, tq=tq, heads=heads),
        grid=(B, hg),
        in_specs=[part(0), part(1), part(2), aux, aux],
        out_specs=pl.BlockSpec((None, heads * dh, T), lambda b, g: (b, g, 0)),
        out_shape=jax.ShapeDtypeStruct((B, H * dh, T), BF16),
        scratch_shapes=([pltpu.VMEM((tq, tq), F32)] * (2 * heads)
                        + [pltpu.VMEM((tq, tq), BF16)] * heads),
        compiler_params=pltpu.CompilerParams(
            dimension_semantics=("parallel", "parallel"), vmem_limit_bytes=VMEM_LIMIT),
        name="fox_attn",
    )(qkv_t, qkv_t, qkv_t, qaux, kaux)


HALO = 32
SUBLANES = 8


def _mix_out_body(x_ref, u_ref, uprev_ref, gc_ref, ga_ref, attn_t_ref, dww_ref, dwb_ref,
                  lng_ref, lnb_ref, pw32_ref, abw32_ref, wout32_ref, o_ref, ubuf_ref, ushift_ref,
                  pw_ref, abw_ref, wout_ref, *, tc, rb):
    first = pl.program_id(1) == 0

    @pl.when(first & (pl.program_id(0) == 0))
    def _():
        pw_ref[...] = pw32_ref[...].astype(BF16)
        abw_ref[...] = abw32_ref[...].astype(BF16)
        wout_ref[...] = wout32_ref[...].astype(BF16)

    ubuf_ref[:HALO, :] = jnp.where(first, 0.0, uprev_ref[...])
    ubuf_ref[HALO:, :] = u_ref[...]
    span = tc + HALO - SUBLANES
    for s in range(1, SUBLANES):
        ushift_ref[s - 1, :, :] = ubuf_ref[s:s + span, :]

    def conv_branch(r0):
        acc = jnp.zeros((rb, CONV_DIM), F32)
        for k in range(CONV_WIDTH):
            off = r0 + HALO - (CONV_WIDTH - 1) + k
            s, base = off % SUBLANES, off - off % SUBLANES
            rows = ubuf_ref[base:base + rb, :] if s == 0 else ushift_ref[s - 1, base:base + rb, :]
            acc = acc + rows * dww_ref[k:k + 1, :]
        acc = acc + dwb_ref[...]
        mu = jnp.mean(acc, axis=-1, keepdims=True)
        d = acc - mu
        var = jnp.mean(d * d, axis=-1, keepdims=True)
        y = d * lax.rsqrt(var + EPS) * lng_ref[...] + lnb_ref[...]
        return (y * _sigmoid(y)).astype(BF16)

    def project(r0, y):
        rows = slice(r0, r0 + rb)
        conv_out = _dot(y, pw_ref[...])
        attn_out = _dot_tn(attn_t_ref[:, rows], abw_ref[...])
        merged = (_sigmoid(gc_ref[rows, :].astype(F32)) * conv_out
                  + _sigmoid(ga_ref[rows, :].astype(F32)) * attn_out)
        o_ref[rows, :] = x_ref[rows, :] + _dot(merged.astype(BF16), wout_ref[...])

    nb = tc // rb
    y = conv_branch(0)
    for j in range(nb):
        y_next = conv_branch((j + 1) * rb) if j + 1 < nb else None
        project(j * rb, y)
        y = y_next


def _mix_out(x, u, gc, ga, attn_t, dww, dwb, lng, lnb, pw, abw, wout, *, tc, rb):
    B, T, D = x.shape
    row = lambda b, i: (b, i, 0)
    halo_blocks = tc // HALO
    return pl.pallas_call(
        functools.partial(_mix_out_body, tc=tc, rb=rb),
        grid=(B, T // tc),
        in_specs=[
            pl.BlockSpec((None, tc, D), row),
            pl.BlockSpec((None, tc, CONV_DIM), row),
            pl.BlockSpec((None, HALO, CONV_DIM),
                         lambda b, i: (b, jnp.maximum(i * halo_blocks - 1, 0), 0)),
            pl.BlockSpec((None, tc, D), row),
            pl.BlockSpec((None, tc, D), row),
            pl.BlockSpec((None, FOX_DIM, tc), lambda b, i: (b, 0, i)),
            _const_spec(dww.shape), _const_spec(dwb.shape), _const_spec(lng.shape),
            _const_spec(lnb.shape), _const_spec(pw.shape), _const_spec(abw.shape),
            _const_spec(wout.shape),
        ],
        out_specs=pl.BlockSpec((None, tc, D), row),
        out_shape=jax.ShapeDtypeStruct((B, T, D), F32),
        scratch_shapes=[pltpu.VMEM((HALO + tc, CONV_DIM), F32),
                        pltpu.VMEM((SUBLANES - 1, HALO + tc - SUBLANES, CONV_DIM), F32),
                        pltpu.VMEM(pw.shape, BF16), pltpu.VMEM(abw.shape, BF16),
                        pltpu.VMEM(wout.shape, BF16)],
        compiler_params=pltpu.CompilerParams(
            dimension_semantics=("arbitrary", "arbitrary"), vmem_limit_bytes=VMEM_LIMIT),
        name="mix_out",
    )(x, u, u, gc, ga, attn_t, dww, dwb, lng, lnb, pw, abw, wout)


def _mem_kv_body(m_ref, g_ref, wk_ref, wv_t_ref, k_ref, v_t_ref):
    h = _rms(m_ref[...], g_ref[...]).astype(BF16)
    k_ref[...] = _dot(h, wk_ref[...]).astype(BF16)
    v_t_ref[...] = _dot_nt(wv_t_ref[...], h).astype(BF16)


def _mem_kv(mem, g, wk, wv_t):
    B, M, D = mem.shape
    return pl.pallas_call(
        _mem_kv_body,
        grid=(B,),
        in_specs=[pl.BlockSpec((None, M, D), lambda b: (b, 0, 0)), _const_spec((1, D)),
                  _const_spec(wk.shape), _const_spec(wv_t.shape)],
        out_specs=[pl.BlockSpec((None, M, XA_DIM), lambda b: (b, 0, 0)),
                   pl.BlockSpec((None, XA_DIM, M), lambda b: (b, 0, 0))],
        out_shape=[jax.ShapeDtypeStruct((B, M, XA_DIM), BF16),
                   jax.ShapeDtypeStruct((B, XA_DIM, M), BF16)],
        compiler_params=pltpu.CompilerParams(dimension_semantics=("parallel",)),
        name="mem_kv",
    )(mem, g, wk, wv_t)


META_LANES = 128
XA_WIDTH = D_MODEL + META_LANES
LANE_GROUP = EXPERTS_PER_GROUP
LANE_RANK = EXPERTS_PER_GROUP + 1
LANE_CODE = EXPERTS_PER_GROUP + 2
CODE_ROWS = SUBLANES
ROUTE_ROWS = SUBLANES * (N_GROUPS + 1)
CUMSUM_BLOCK = 256


def _first_max_row(v, valid, row):
    neg = jnp.where(valid, v, -jnp.inf)
    mx = jnp.max(neg, axis=0, keepdims=True)
    idx = jnp.min(jnp.where(valid & (neg == mx), row, SUBLANES), axis=0, keepdims=True)
    return mx, idx


def _route(logits_t, run):
    n = logits_t.shape[1]
    row = lax.broadcasted_iota(jnp.int32, (SUBLANES, n), 0)
    valid = row < N_GROUPS
    glog = logits_t[:SUBLANES]
    gmax, g_star = _first_max_row(glog, valid, row)
    p_gsel = 1.0 / jnp.sum(jnp.where(valid, jnp.exp(glog - gmax), 0.0), axis=0, keepdims=True)
    elog = logits_t[SUBLANES:2 * SUBLANES]
    for g in range(1, N_GROUPS):
        elog = jnp.where(g_star == g, logits_t[(g + 1) * SUBLANES:(g + 2) * SUBLANES], elog)
    emax, _ = _first_max_row(elog, valid, row)
    ex = jnp.where(valid, jnp.exp(elog - emax), 0.0)
    p_exp = ex / jnp.sum(ex, axis=0, keepdims=True)
    v1, i1 = _first_max_row(p_exp, valid, row)
    v2, i2 = _first_max_row(p_exp, valid & (row != i1), row)
    tot = v1 + v2
    w = jnp.where(row == i1, p_gsel * (v1 / tot), jnp.where(row == i2, p_gsel * (v2 / tot), 0.0))
    onehot = jnp.where(row == g_star, 1.0, 0.0).astype(BF16)
    cb = min(n, CUMSUM_BLOCK)
    r_i = lax.broadcasted_iota(jnp.int32, (cb, cb), 0)
    c_i = lax.broadcasted_iota(jnp.int32, (cb, cb), 1)
    tri = jnp.where(r_i <= c_i, 1.0, 0.0).astype(BF16)
    pieces = []
    for j in range(n // cb):
        piece = _dot(onehot[:, j * cb:(j + 1) * cb], tri) + run
        pieces.append(piece)
        run = piece[:, cb - 1:cb]
    cum = jnp.concatenate(pieces, axis=1)
    rank = jnp.sum(jnp.where(row == g_star, cum, 0.0), axis=0, keepdims=True) - 1.0
    g_f = g_star.astype(F32)
    meta_t = jnp.where(valid, w, jnp.where(row == LANE_GROUP, g_f, jnp.where(
        row == LANE_RANK, rank, jnp.where(row == LANE_CODE, rank * N_GROUPS + g_f, 0.0))))
    return meta_t, cum[:, n - 1:n]


def _xattn_body(x_ref, g_ref, wq_t_ref, k_ref, v_t_ref, wo_ref, gm_ref, wr_t_ref, br_t_ref,
                xa_ref, cnt_ref, code_ref, run_ref, *, row_blocks):
    @pl.when((pl.program_id(0) == 0) & (pl.program_id(1) == 0))
    def _():
        run_ref[...] = jnp.zeros_like(run_ref)

    tx = x_ref.shape[0]
    hb = tx // row_blocks
    scale = XA_HEAD_DIM ** -0.5

    def attend(r0):
        rows = slice(r0, r0 + hb)
        x = x_ref[rows, :]
        q_t = _dot_nt(wq_t_ref[...], _rms(x, g_ref[...]).astype(BF16)).astype(BF16)
        yield
        head_rows = [slice(h * XA_HEAD_DIM, (h + 1) * XA_HEAD_DIM) for h in range(XA_HEADS)]
        scores = [_dot(k_ref[:, sl], q_t[sl, :]) * scale for sl in head_rows]
        yield
        heads = []
        for sl, s_t in zip(head_rows, scores):
            p = jnp.exp(s_t - jnp.max(s_t, axis=0, keepdims=True))
            l = jnp.sum(p, axis=0, keepdims=True)
            heads.append((_dot(v_t_ref[sl, :], p.astype(BF16)) / l).astype(BF16))
            yield
        x2 = x + _dot_tn(jnp.concatenate(heads, axis=0), wo_ref[...])
        xa_ref[rows, :D_MODEL] = x2
        yield
        hm = _rms(x2, gm_ref[...]).astype(BF16)
        logits[r0] = _dot_nt(wr_t_ref[...], hm) + br_t_ref[...]
        yield

    logits = {}
    blocks = [attend(r0) for r0 in range(0, tx, hb)]
    for _ in range(XA_HEADS + 4):
        for blk in blocks:
            next(blk)
    run = run_ref[...]
    for r0 in range(0, tx, hb):
        meta_t, run = _route(logits[r0], run)
        code_ref[:, r0:r0 + hb] = meta_t.astype(jnp.int32)
        xa_ref[r0:r0 + hb, D_MODEL:] = jnp.concatenate(
            [meta_t, jnp.zeros((META_LANES - CODE_ROWS, hb), F32)], axis=0).T
    run_ref[...] = run
    cnt_ref[...] = jnp.broadcast_to(run, cnt_ref.shape).astype(jnp.int32)


def _xattn_route(x, g, wq_t, kx, vx_t, wo, gm, wr_t, br_t, *, tx, row_blocks):
    B, T, D = x.shape
    row = lambda b, i: (b, i, 0)
    return pl.pallas_call(
        functools.partial(_xattn_body, row_blocks=row_blocks),
        grid=(B, T // tx),
        in_specs=[pl.BlockSpec((None, tx, D), row), _const_spec((1, D)),
                  _const_spec(wq_t.shape),
                  pl.BlockSpec((None, MEM_LEN, XA_DIM), lambda b, i: (b, 0, 0)),
                  pl.BlockSpec((None, XA_DIM, MEM_LEN), lambda b, i: (b, 0, 0)),
                  _const_spec(wo.shape), _const_spec((1, D)),
                  _const_spec(wr_t.shape), _const_spec(br_t.shape)],
        out_specs=[pl.BlockSpec((None, tx, XA_WIDTH), row), _const_spec((CODE_ROWS, META_LANES)),
                   pl.BlockSpec((None, None, CODE_ROWS, tx), lambda b, i: (b, i, 0, 0))],
        out_shape=[jax.ShapeDtypeStruct((B, T, XA_WIDTH), F32),
                   jax.ShapeDtypeStruct((CODE_ROWS, META_LANES), jnp.int32),
                   jax.ShapeDtypeStruct((B, T // tx, CODE_ROWS, tx), jnp.int32)],
        scratch_shapes=[pltpu.VMEM((CODE_ROWS, 1), F32)],
        compiler_params=pltpu.CompilerParams(
            dimension_semantics=("arbitrary", "arbitrary"), vmem_limit_bytes=VMEM_LIMIT),
        name="xattn_route",
    )(x, g, wq_t, kx, vx_t, wo, gm, wr_t, br_t)


VISIT_FIRST, VISIT_LAST, VISIT_VALID = 1, 2, 4


def _moe_plan_body(code_ref, cnt_ref, inv_ref, vt_ref, vg_ref, vf_ref, offs_ref, *, ts):
    n = code_ref.shape[0]
    nt, nv = n // ts, vt_ref.shape[0]
    ends = [cnt_ref[0, 0]]
    for g in range(1, N_GROUPS - 1):
        ends.append(ends[-1] + cnt_ref[g, 0])
    offs_ref[0] = 0
    for g in range(1, N_GROUPS):
        offs_ref[g] = ends[g - 1]
    group_bits = N_GROUPS.bit_length() - 1
    assert 1 << group_bits == N_GROUPS

    def place(t, _):
        code = code_ref[t]
        slot = offs_ref[code & (N_GROUPS - 1)] + lax.shift_right_logical(code, group_bits)
        inv_ref[slot] = t
        return 0

    lax.fori_loop(0, n, place, 0, unroll=8)

    def group_of(slot):
        return sum(jnp.where(e <= slot, 1, 0) for e in ends)

    def tile(j, v):
        g_lo, g_hi = group_of(j * ts), group_of(j * ts + ts - 1)
        for k in range(N_GROUPS):
            @pl.when(k <= g_hi - g_lo)
            def _():
                vt_ref[v + k] = j
                vg_ref[v + k] = g_lo + k
                vf_ref[v + k] = (VISIT_VALID + (VISIT_FIRST if k == 0 else 0)
                                 + jnp.where(k == g_hi - g_lo, VISIT_LAST, 0))
        return v + g_hi - g_lo + 1

    used = lax.fori_loop(0, nt, tile, 0)
    g_last = group_of(n - 1)

    def pad(v, _):
        vt_ref[v] = nt - 1
        vg_ref[v] = g_last
        vf_ref[v] = 0
        return 0

    lax.fori_loop(used, nv, pad, 0)


def _moe_plan(code, cnt, *, ts):
    n = code.shape[0]
    nt = n // ts
    nv = nt + N_GROUPS - 1
    smem = pl.BlockSpec(memory_space=pltpu.SMEM)
    visits = jax.ShapeDtypeStruct((nv,), jnp.int32)
    inv, vt, vg, vf = pl.pallas_call(
        functools.partial(_moe_plan_body, ts=ts),
        in_specs=[smem, smem],
        out_specs=[smem, smem, smem, smem],
        out_shape=[jax.ShapeDtypeStruct((n,), jnp.int32), visits, visits, visits],
        scratch_shapes=[pltpu.SMEM((N_GROUPS,), jnp.int32)],
        name="moe_plan",
    )(code, cnt)
    return inv.reshape(nt, 1, ts), vt, vg, vf


def _moe_body(vt_ref, vg_ref, vf_ref, inv_cur_ref, inv_nxt_ref, inv_prv_ref, xa_hbm, gm_ref,
              wg32_ref, wu32_ref, wd32_ref, gf_ref, out_hbm, xbuf, obuf, hm_ref, yacc_ref,
              wg_ref, wu_ref, wd_ref, gsem, ssem, *, ts, nt, sub):
    v = pl.program_id(0)
    t, grp, flags = vt_ref[v], vg_ref[v], vf_ref[v]
    slot = lax.rem(t, 2)
    first = (flags & VISIT_FIRST) != 0
    overlapped = first & (t >= 1)

    def gather_rows(idx_ref, dst_slot, lo, hi):
        for r in range(lo, min(hi, ts)):
            pltpu.make_async_copy(xa_hbm.at[pl.ds(idx_ref[0, r], 1)], xbuf.at[dst_slot, pl.ds(r, 1)],
                                  gsem.at[dst_slot]).start(priority=r % 2)

    def gather_wait(dst_slot):
        pltpu.make_async_copy(xa_hbm.at[pl.ds(0, ts)], xbuf.at[dst_slot], gsem.at[dst_slot]).wait()

    def scatter_rows(idx_ref, src_slot, lo, hi):
        for r in range(lo, min(hi, ts)):
            pltpu.make_async_copy(obuf.at[src_slot, pl.ds(r, 1)], out_hbm.at[pl.ds(idx_ref[0, r], 1)],
                                  ssem.at[src_slot]).start(priority=r % 2)

    def scatter_wait(src_slot):
        pltpu.make_async_copy(obuf.at[src_slot], out_hbm.at[pl.ds(0, ts)], ssem.at[src_slot]).wait()

    def experts(issue_dma):
        lane = lax.broadcasted_iota(jnp.int32, (sub, META_LANES), 1)
        n_sub = ts // sub
        half = n_sub * 3 * EXPERTS_PER_GROUP // 2
        batch = -(-ts // half)
        done = [0, 0]
        dots = [0]

        def after_dot(r):
            if not issue_dma:
                return
            kind = 0 if dots[0] < half else 1
            dots[0] += 1
            lo = done[kind]
            if kind == 0:
                gather_rows(inv_nxt_ref, 1 - slot, lo, lo + batch)
            else:
                scatter_rows(inv_prv_ref, 1 - slot, lo, lo + batch)
            done[kind] = min(lo + batch, ts)

        for r in range(n_sub):
            rows = slice(r * sub, (r + 1) * sub)
            meta = xbuf[slot, rows, D_MODEL:]
            row_grp = jnp.sum(jnp.where(lane == LANE_GROUP, meta, 0.0), axis=-1, keepdims=True)
            member = row_grp == grp.astype(F32)
            hm = hm_ref[rows, :]
            y = jnp.zeros((sub, D_MODEL), F32)
            for e in range(EXPERTS_PER_GROUP):
                w = jnp.sum(jnp.where(lane == e, meta, 0.0), axis=-1, keepdims=True)
                w = jnp.where(member, w, 0.0)
                a = _dot(hm, wg_ref[e])
                after_dot(r)
                b = _dot(hm, wu_ref[e])
                after_dot(r)
                u = (a * _sigmoid(a)) * b * w
                y = y + _dot(u.astype(BF16), wd_ref[e])
                after_dot(r)
            yacc_ref[rows, :] += y
        if issue_dma:
            assert done == [ts, ts], done

    @pl.when(v == 0)
    def _():
        gather_rows(inv_cur_ref, 0, 0, ts)

    @pl.when((v == 0) | (grp != vg_ref[jnp.maximum(v - 1, 0)]))
    def _():
        wg_ref[...] = wg32_ref[...].astype(BF16)
        wu_ref[...] = wu32_ref[...].astype(BF16)
        wd_ref[...] = wd32_ref[...].astype(BF16)

    @pl.when(first)
    def _():
        gather_wait(slot)

        @pl.when(t == 0)
        def _():
            gather_rows(inv_nxt_ref, 1, 0, ts)

        hm_ref[...] = _rms(xbuf[slot, :, :D_MODEL], gm_ref[...]).astype(BF16)
        yacc_ref[...] = jnp.zeros_like(yacc_ref)

    @pl.when(overlapped)
    def _():
        experts(True)

    @pl.when(((flags & VISIT_VALID) != 0) & jnp.logical_not(overlapped))
    def _():
        experts(False)

    @pl.when((flags & VISIT_LAST) != 0)
    def _():
        @pl.when(t >= 2)
        def _():
            scatter_wait(slot)

        obuf[slot] = _rms(xbuf[slot, :, :D_MODEL] + yacc_ref[...], gf_ref[...])

    @pl.when(v == pl.num_programs(0) - 1)
    def _():
        last_slot = (nt - 1) % 2
        scatter_rows(inv_cur_ref, last_slot, 0, ts)
        scatter_wait(0)
        scatter_wait(1)
        gather_wait(1 - last_slot)


def _moe_sorted(xa, inv, vt, vg, vf, gm, wg, wu, wd, gf, *, ts, sub):
    N = xa.shape[0]
    nt = N // ts
    assert nt >= 2
    E, D, FF = EXPERTS_PER_GROUP, D_MODEL, EXPERT_FF
    grp = lambda v, vt, vg, vf: (vg[v], 0, 0)
    const = lambda v, vt, vg, vf: (0, 0)
    smem_row = lambda fn: pl.BlockSpec((None, 1, ts), fn, memory_space=pltpu.SMEM)
    grid_spec = pltpu.PrefetchScalarGridSpec(
        num_scalar_prefetch=3,
        grid=(vt.shape[0],),
        in_specs=[
            smem_row(lambda v, vt, vg, vf: (vt[v], 0, 0)),
            smem_row(lambda v, vt, vg, vf: (jnp.minimum(vt[v] + 1, nt - 1), 0, 0)),
            smem_row(lambda v, vt, vg, vf: (jnp.maximum(vt[v] - 1, 0), 0, 0)),
            pl.BlockSpec(memory_space=pl.ANY),
            pl.BlockSpec((1, D), const),
            pl.BlockSpec((E, D, FF), grp), pl.BlockSpec((E, D, FF), grp),
            pl.BlockSpec((E, FF, D), grp),
            pl.BlockSpec((1, D), const),
        ],
        out_specs=pl.BlockSpec(memory_space=pl.ANY),
        scratch_shapes=[
            pltpu.VMEM((2, ts, XA_WIDTH), F32), pltpu.VMEM((2, ts, D), F32),
            pltpu.VMEM((ts, D), BF16), pltpu.VMEM((ts, D), F32),
            pltpu.VMEM((E, D, FF), BF16), pltpu.VMEM((E, D, FF), BF16), pltpu.VMEM((E, FF, D), BF16),
            pltpu.SemaphoreType.DMA((2,)), pltpu.SemaphoreType.DMA((2,)),
        ],
    )
    return pl.pallas_call(
        functools.partial(_moe_body, ts=ts, nt=nt, sub=sub),
        grid_spec=grid_spec,
        out_shape=jax.ShapeDtypeStruct((N, D), F32),
        compiler_params=pltpu.CompilerParams(
            dimension_semantics=("arbitrary",), vmem_limit_bytes=VMEM_LIMIT),
        name="moe_sorted",
    )(vt, vg, vf, inv, inv, inv, xa, gm, wg, wu, wd, gf)


def kernel(x, mem, norm_mix_g, w_in, fox_bf, conv_dw_w, conv_dw_b, conv_ln_g, conv_ln_b, conv_pw_w, attn_branch_w, w_out, norm_xa_g, norm_mem_g, xa_wq, xa_wk, xa_wv, xa_wo, norm_moe_g, router_group_w, router_group_b, router_expert_w, router_expert_b, expert_w_gate, expert_w_up, expert_w_down, norm_final_g):
    B, T, D = x.shape
    depth = w_in.shape[0]
    assert depth == 1, "the final norm is fused into the last layer's MoE kernel"
    row = lambda v: v.reshape(1, -1).astype(F32)
    for l in range(depth):
        u, gc, ga, qkv_t, f_t = _proj_in(x, row(norm_mix_g[l]), jnp.swapaxes(w_in, 1, 2).astype(F32),
                                         layer=l, tm=512)
        qaux, kaux = _fox_prep(f_t, fox_bf[l].reshape(FOX_HEADS, 1).astype(F32))
        attn_t = _fox_attn(qkv_t, qaux, kaux, tq=512, heads=1)
        x = _mix_out(x, u, gc, ga, attn_t, conv_dw_w[l].astype(F32), row(conv_dw_b[l]),
                     row(conv_ln_g[l]), row(conv_ln_b[l]), conv_pw_w[l].astype(F32),
                     attn_branch_w[l].astype(F32), w_out[l].astype(F32), tc=512, rb=512)
        kx, vx_t = _mem_kv(mem, row(norm_mem_g[l]), xa_wk[l].astype(BF16),
                           xa_wv[l].T.astype(BF16))
        wr_t = jnp.zeros((ROUTE_ROWS, D), F32).at[:N_GROUPS].set(router_group_w[l].T)
        br_t = jnp.zeros((ROUTE_ROWS, 1), F32).at[:N_GROUPS, 0].set(router_group_b[l])
        for g in range(N_GROUPS):
            r0, e0 = SUBLANES * (g + 1), EXPERTS_PER_GROUP * g
            wr_t = wr_t.at[r0:r0 + EXPERTS_PER_GROUP].set(
                router_expert_w[l][:, e0:e0 + EXPERTS_PER_GROUP].T)
            br_t = br_t.at[r0:r0 + EXPERTS_PER_GROUP, 0].set(
                router_expert_b[l][e0:e0 + EXPERTS_PER_GROUP])
        xa, cnt, code = _xattn_route(x, row(norm_xa_g[l]), xa_wq[l].T.astype(BF16), kx, vx_t,
                                     xa_wo[l].astype(BF16), row(norm_moe_g[l]),
                                     wr_t.astype(BF16), br_t, tx=1024, row_blocks=1)
        xa = xa.reshape(B * T, XA_WIDTH)
        inv, vt, vg, vf = _moe_plan(code[:, :, LANE_CODE, :].reshape(B * T), cnt, ts=512)
        x = _moe_sorted(xa, inv, vt, vg, vf, row(norm_moe_g[l]), expert_w_gate[l].astype(F32),
                        expert_w_up[l].astype(F32), expert_w_down[l].astype(F32),
                        row(norm_final_g), ts=512, sub=512).reshape(B, T, D)
    return x
```

```python
import functools

import jax
import jax.numpy as jnp
from jax import lax
from jax.experimental import pallas as pl
from jax.experimental.pallas import tpu as pltpu

D_MODEL = 1024
MEM_LEN = 256
EPS = 1e-6
CONV_DIM = 512
CONV_WIDTH = 31
FOX_HEADS = 8
FOX_HEAD_DIM = 64
FOX_DIM = FOX_HEADS * FOX_HEAD_DIM
XA_HEADS = 4
XA_HEAD_DIM = 128
XA_DIM = XA_HEADS * XA_HEAD_DIM
N_GROUPS = 4
EXPERTS_PER_GROUP = 4
N_EXPERTS = N_GROUPS * EXPERTS_PER_GROUP
EXPERT_FF = 256

AUX_ROWS = 16
LOG2E = 1.4426950408889634
VMEM_LIMIT = 56 * 1024 * 1024

F32 = jnp.float32
BF16 = jnp.bfloat16


def _const_spec(shape):
    n = len(shape)
    return pl.BlockSpec(shape, lambda *_: (0,) * n)


def _rms(x, g):
    return x * lax.rsqrt(jnp.mean(x * x, axis=-1, keepdims=True) + EPS) * g


def _sigmoid(x):
    return 0.5 * jnp.tanh(0.5 * x) + 0.5


def _dot(a, b):
    return jnp.dot(a, b, preferred_element_type=F32)


def _dot_nt(a, b):
    return lax.dot_general(a, b, (((1,), (1,)), ((), ())), preferred_element_type=F32)


def _dot_tn(a, b):
    return lax.dot_general(a, b, (((0,), (0,)), ((), ())), preferred_element_type=F32)


C_GLU = 2 * CONV_DIM
C_F = C_GLU + 3 * FOX_DIM
C_GC = C_F + FOX_HEADS
C_GA = C_GC + D_MODEL


def _proj_in_body(x_ref, g_ref, w_ref, u_ref, gc_ref, ga_ref, qkv_t_ref, f_t_ref,
                  wglu_ref, wgc_ref, wga_ref, wqkv_t_ref, wf_t_ref):
    @pl.when((pl.program_id(0) == 0) & (pl.program_id(1) == 0))
    def _():
        for j in range(C_GLU // D_MODEL):
            rows = slice(j * D_MODEL, (j + 1) * D_MODEL)
            wglu_ref[:, rows] = w_ref[rows, :].T.astype(BF16)
        wgc_ref[...] = w_ref[C_GC:C_GA, :].T.astype(BF16)
        wga_ref[...] = w_ref[C_GA:, :].T.astype(BF16)
        wqkv_t_ref[...] = w_ref[C_GLU:C_F, :].astype(BF16)
        wf_t_ref[...] = w_ref[C_F:C_GC, :].astype(BF16)

    h = _rms(x_ref[...], g_ref[...]).astype(BF16)
    glu = _dot(h, wglu_ref[...])
    u_ref[...] = glu[:, :CONV_DIM] * _sigmoid(glu[:, CONV_DIM:])
    gc_ref[...] = _sigmoid(_dot(h, wgc_ref[...])).astype(BF16)
    ga_ref[...] = _sigmoid(_dot(h, wga_ref[...])).astype(BF16)
    qkv_t = _dot_nt(wqkv_t_ref[...], h)
    scale = FOX_HEAD_DIM ** -0.5 * LOG2E
    qkv_t_ref[:FOX_DIM, :] = (qkv_t[:FOX_DIM] * scale).astype(BF16)
    qkv_t_ref[FOX_DIM:, :] = qkv_t[FOX_DIM:].astype(BF16)
    f_t_ref[...] = _dot_nt(wf_t_ref[...], h)


def _proj_in(x, g, w, *, layer, tm):
    B, T, D = x.shape
    grid = (B, T // tm)
    row = lambda b, i: (b, i, 0)
    col = lambda b, i: (b, 0, i)
    return pl.pallas_call(
        _proj_in_body,
        grid=grid,
        in_specs=[
            pl.BlockSpec((None, tm, D), row),
            _const_spec((1, D)),
            pl.BlockSpec((None,) + w.shape[1:], lambda b, i: (layer, 0, 0),
                         pipeline_mode=pl.Buffered(1)),
        ],
        out_specs=[
            pl.BlockSpec((None, tm, CONV_DIM), row),
            pl.BlockSpec((None, tm, D), row),
            pl.BlockSpec((None, tm, D), row),
            pl.BlockSpec((None, 3 * FOX_DIM, tm), col),
            pl.BlockSpec((None, FOX_HEADS, tm), col),
        ],
        out_shape=[
            jax.ShapeDtypeStruct((B, T, CONV_DIM), F32),
            jax.ShapeDtypeStruct((B, T, D), BF16),
            jax.ShapeDtypeStruct((B, T, D), BF16),
            jax.ShapeDtypeStruct((B, 3 * FOX_DIM, T), BF16),
            jax.ShapeDtypeStruct((B, FOX_HEADS, T), F32),
        ],
        scratch_shapes=[pltpu.VMEM((D, C_GLU), BF16), pltpu.VMEM((D, D), BF16),
                        pltpu.VMEM((D, D), BF16), pltpu.VMEM((3 * FOX_DIM, D), BF16),
                        pltpu.VMEM((FOX_HEADS, D), BF16)],
        compiler_params=pltpu.CompilerParams(
            dimension_semantics=("arbitrary", "arbitrary"), vmem_limit_bytes=VMEM_LIMIT),
        name="proj_in",
    )(x, g, w)


def _split3(c):
    hi = c.astype(BF16)
    r = c - hi.astype(F32)
    mid = r.astype(BF16)
    lo = (r - mid.astype(F32)).astype(BF16)
    return hi, mid, lo


def _fox_prep_body(f_t_ref, bf_ref, qaux_ref, kaux_ref, c_ref, *, chunk):
    T = f_t_ref.shape[-1]
    z = f_t_ref[...] + bf_ref[...]
    logf = jnp.minimum(z, 0.0) - jnp.log1p(jnp.exp(-jnp.abs(z)))
    r_i = lax.broadcasted_iota(jnp.int32, (chunk, chunk), 0)
    c_i = lax.broadcasted_iota(jnp.int32, (chunk, chunk), 1)
    tri = jnp.where(r_i <= c_i, 1.0, 0.0).astype(BF16)
    carry = jnp.zeros((FOX_HEADS, 1), F32)
    for j in range(T // chunk):
        blk = logf[:, j * chunk:(j + 1) * chunk]
        hi, mid, lo = _split3(blk)
        cs = (_dot(lo, tri) + _dot(mid, tri)) + _dot(hi, tri) + carry
        c_ref[:, j * chunk:(j + 1) * chunk] = cs
        carry = cs[:, chunk - 1:chunk]
    row = lax.broadcasted_iota(jnp.int32, (AUX_ROWS, T), 0)
    for h in range(FOX_HEADS):
        hi, mid, lo = _split3(c_ref[h:h + 1, :] * LOG2E)
        pieces = [jnp.broadcast_to(p.astype(F32), (AUX_ROWS, T)) for p in (hi, mid, lo)]
        one = jnp.where(row < 3, 1.0, 0.0)
        val = jnp.where(row == 0, pieces[0], jnp.where(row == 1, pieces[1], pieces[2]))
        val_k = jnp.where(row == 3, pieces[0], jnp.where(row == 4, pieces[1], pieces[2]))
        qaux = jnp.where(row < 3, val, jnp.where(row < 6, 1.0, 0.0))
        kaux = jnp.where(row < 3, one, jnp.where(row < 6, -val_k, 0.0))
        qaux_ref[h] = qaux.astype(BF16)
        kaux_ref[h] = kaux.astype(BF16)


def _fox_prep(f_t, bf):
    B, H, T = f_t.shape
    aux = jax.ShapeDtypeStruct((B, H, AUX_ROWS, T), BF16)
    aux_spec = pl.BlockSpec((None, H, AUX_ROWS, T), lambda b: (b, 0, 0, 0))
    return pl.pallas_call(
        functools.partial(_fox_prep_body, chunk=512),
        grid=(B,),
        in_specs=[pl.BlockSpec((None, H, T), lambda b: (b, 0, 0)), _const_spec((H, 1))],
        out_specs=[aux_spec, aux_spec],
        out_shape=[aux, aux],
        scratch_shapes=[pltpu.VMEM((H, T), F32)],
        compiler_params=pltpu.CompilerParams(dimension_semantics=("parallel",)),
        name="fox_prep",
    )(f_t, bf)


MASK_BLOCK = 256


def _fox_attn_body(q_ref, k_ref, v_ref, qaux_ref, kaux_ref, o_ref, *scratch, tq, heads):
    T = q_ref.shape[-1]
    dh = FOX_HEAD_DIM
    mb = MASK_BLOCK
    nj = tq // mb
    kk = lax.broadcasted_iota(jnp.int32, (mb, mb), 0)
    qq = lax.broadcasted_iota(jnp.int32, (mb, mb), 1)
    causal = kk <= qq
    ones = jnp.where(lax.broadcasted_iota(jnp.int32, (AUX_ROWS, tq), 0) == 0, 1.0, 0.0).astype(BF16)
    sbufs = [scratch[2 * h:2 * h + 2] for h in range(heads)]
    pbufs = scratch[2 * heads:]

    tasks = []
    for qi in range(T // tq):
        tasks.append((qi, qi, True))
        tasks += [(qi, c, False) for c in range(qi)]

    def scores(h, task, j, sb, mc):
        qi, c, diag = task
        c0 = j * mb if diag else 0
        r0 = c * tq + j * mb
        hd = slice(h * dh, (h + 1) * dh)
        cols = slice(qi * tq + c0, (qi + 1) * tq)
        ka = jnp.concatenate([k_ref[hd, r0:r0 + mb], kaux_ref[h, :, r0:r0 + mb]], axis=0)
        qa = jnp.concatenate([q_ref[hd, cols], qaux_ref[h, :, cols]], axis=0)
        s = _dot_tn(ka, qa)
        if diag:
            head = jnp.where(causal, s[:, :mb], -jnp.inf)
            s = head if tq - c0 == mb else jnp.concatenate([head, s[:, mb:]], axis=1)
        sb[j * mb:(j + 1) * mb, c0:] = s
        cm = jnp.max(s, axis=0, keepdims=True)
        if c0:
            cm = jnp.concatenate([jnp.full((1, c0), -jnp.inf, F32), cm], axis=1)
        return cm if mc is None else jnp.maximum(mc, cm)

    def probs(h, task, j, sb, m_new):
        c0 = j * mb if task[2] else 0
        rows = slice(j * mb, (j + 1) * mb)
        pbufs[h][rows, c0:] = jnp.exp2(sb[rows, c0:] - m_new[:, c0:]).astype(BF16)

    def values(h, task):
        _, c, diag = task
        p_ref = pbufs[h]
        va = jnp.concatenate([v_ref[h * dh:(h + 1) * dh, c * tq:(c + 1) * tq], ones], axis=0)
        if not diag:
            return _dot(va, p_ref[...])
        return jnp.concatenate(
            [_dot(va[:, :(bj + 1) * mb], p_ref[:(bj + 1) * mb, bj * mb:(bj + 1) * mb])
             for bj in range(nj)], axis=1)

    hs = range(heads)
    mc = [None] * heads
    for j in range(nj):
        for h in hs:
            mc[h] = scores(h, tasks[0], j, sbufs[h][0], mc[h])
    m_run, acc = [None] * heads, [None] * heads
    for k, task in enumerate(tasks):
        qi, c, diag = task
        nxt = tasks[k + 1] if k + 1 < len(tasks) else None
        m_new = [mc[h] if diag else jnp.maximum(m_run[h], mc[h]) for h in hs]
        mc = [None] * heads
        for j in range(nj):
            for h in hs:
                if nxt is not None:
                    mc[h] = scores(h, nxt, j, sbufs[h][(k + 1) % 2], mc[h])
                probs(h, task, j, sbufs[h][k % 2], m_new[h])
        for h in hs:
            pv = values(h, task)
            acc[h] = pv if diag else jnp.exp2(m_run[h] - m_new[h]) * acc[h] + pv
            if nxt is None or nxt[0] != qi:
                o_ref[h * dh:(h + 1) * dh, qi * tq:(qi + 1) * tq] = (
                    acc[h][:dh] / acc[h][dh:dh + 1]).astype(o_ref.dtype)
        m_run = m_new


def _fox_attn(qkv_t, qaux, kaux, *, tq, heads):
    B, _, T = qkv_t.shape
    H, dh = FOX_HEADS, FOX_HEAD_DIM
    assert T % tq == 0 and tq % MASK_BLOCK == 0 and H % heads == 0
    hg = H // heads
    part = lambda off: pl.BlockSpec((None, heads * dh, T), lambda b, g: (b, off * hg + g, 0))
    aux = pl.BlockSpec((None, heads, AUX_ROWS, T), lambda b, g: (b, g, 0, 0))
    return pl.pallas_call(
        functools.partial(_fox_attn_body, tq=tq, heads=heads),
        grid=(B, hg),
        in_specs=[part(0), part(1), part(2), aux, aux],
        out_specs=pl.BlockSpec((None, heads * dh, T), lambda b, g: (b, g, 0)),
        out_shape=jax.ShapeDtypeStruct((B, H * dh, T), BF16),
        scratch_shapes=([pltpu.VMEM((tq, tq), F32)] * (2 * heads)
                        + [pltpu.VMEM((tq, tq), BF16)] * heads),
        compiler_params=pltpu.CompilerParams(
            dimension_semantics=("parallel", "parallel"), vmem_limit_bytes=VMEM_LIMIT),
        name="fox_attn",
    )(qkv_t, qkv_t, qkv_t, qaux, kaux)


HALO = 32
SUBLANES = 8
MIX_ROW_BLOCK = 256


def _mix_out_body(x_ref, u_ref, uprev_ref, gc_ref, ga_ref, attn_t_ref, dww_ref, dwb_ref,
                  lng_ref, lnb_ref, pw32_ref, abw32_ref, wout32_ref, o_ref, ubuf_ref, ushift_ref,
                  pw_ref, abw_ref, wout_ref, *, tc):
    first = pl.program_id(1) == 0

    @pl.when(first & (pl.program_id(0) == 0))
    def _():
        pw_ref[...] = pw32_ref[...].astype(BF16)
        abw_ref[...] = abw32_ref[...].astype(BF16)
        wout_ref[...] = wout32_ref[...].astype(BF16)

    ubuf_ref[:HALO, :] = jnp.where(first, 0.0, uprev_ref[...])
    ubuf_ref[HALO:, :] = u_ref[...]
    span = tc + HALO - SUBLANES
    for s in range(1, SUBLANES):
        ushift_ref[s - 1, :, :] = ubuf_ref[s:s + span, :]

    rb = MIX_ROW_BLOCK

    def conv_branch(r0):
        acc = jnp.zeros((rb, CONV_DIM), F32)
        for k in range(CONV_WIDTH):
            off = r0 + HALO - (CONV_WIDTH - 1) + k
            s, base = off % SUBLANES, off - off % SUBLANES
            rows = ubuf_ref[base:base + rb, :] if s == 0 else ushift_ref[s - 1, base:base + rb, :]
            acc = acc + rows * dww_ref[k:k + 1, :]
        acc = acc + dwb_ref[...]
        mu = jnp.mean(acc, axis=-1, keepdims=True)
        d = acc - mu
        var = jnp.mean(d * d, axis=-1, keepdims=True)
        y = d * lax.rsqrt(var + EPS) * lng_ref[...] + lnb_ref[...]
        return (y * _sigmoid(y)).astype(BF16)

    def project(r0, y):
        rows = slice(r0, r0 + rb)
        conv_out = _dot(y, pw_ref[...])
        attn_out = _dot_tn(attn_t_ref[:, rows], abw_ref[...])
        merged = gc_ref[rows, :].astype(F32) * conv_out + ga_ref[rows, :].astype(F32) * attn_out
        o_ref[rows, :] = x_ref[rows, :] + _dot(merged.astype(BF16), wout_ref[...])

    y = conv_branch(0)
    for r0 in range(0, tc, rb):
        y_next = conv_branch(r0 + rb) if r0 + rb < tc else None
        project(r0, y)
        y = y_next


def _mix_out(x, u, gc, ga, attn_t, dww, dwb, lng, lnb, pw, abw, wout, *, tc):
    B, T, D = x.shape
    assert T % tc == 0 and tc % MIX_ROW_BLOCK == 0 and tc % HALO == 0 and HALO >= CONV_WIDTH - 1
    row = lambda b, i: (b, i, 0)
    halo_blocks = tc // HALO
    return pl.pallas_call(
        functools.partial(_mix_out_body, tc=tc),
        grid=(B, T // tc),
        in_specs=[
            pl.BlockSpec((None, tc, D), row),
            pl.BlockSpec((None, tc, CONV_DIM), row),
            pl.BlockSpec((None, HALO, CONV_DIM),
                         lambda b, i: (b, jnp.maximum(i * halo_blocks - 1, 0), 0)),
            pl.BlockSpec((None, tc, D), row),
            pl.BlockSpec((None, tc, D), row),
            pl.BlockSpec((None, FOX_DIM, tc), lambda b, i: (b, 0, i)),
            _const_spec(dww.shape), _const_spec(dwb.shape), _const_spec(lng.shape),
            _const_spec(lnb.shape), _const_spec(pw.shape), _const_spec(abw.shape),
            _const_spec(wout.shape),
        ],
        out_specs=pl.BlockSpec((None, tc, D), row),
        out_shape=jax.ShapeDtypeStruct((B, T, D), F32),
        scratch_shapes=[pltpu.VMEM((HALO + tc, CONV_DIM), F32),
                        pltpu.VMEM((SUBLANES - 1, HALO + tc - SUBLANES, CONV_DIM), F32),
                        pltpu.VMEM(pw.shape, BF16), pltpu.VMEM(abw.shape, BF16),
                        pltpu.VMEM(wout.shape, BF16)],
        compiler_params=pltpu.CompilerParams(
            dimension_semantics=("arbitrary", "arbitrary"), vmem_limit_bytes=VMEM_LIMIT),
        name="mix_out",
    )(x, u, u, gc, ga, attn_t, dww, dwb, lng, lnb, pw, abw, wout)


META_LANES = 128
XA_WIDTH = D_MODEL + META_LANES
LANE_GROUP = EXPERTS_PER_GROUP
LANE_RANK = EXPERTS_PER_GROUP + 1
LANE_CODE = EXPERTS_PER_GROUP + 2
CODE_ROWS = SUBLANES
ROUTE_ROWS = SUBLANES * (N_GROUPS + 1)
CUMSUM_BLOCK = 256


def _first_max_row(v, valid, row):
    neg = jnp.where(valid, v, -jnp.inf)
    mx = jnp.max(neg, axis=0, keepdims=True)
    idx = jnp.min(jnp.where(valid & (neg == mx), row, SUBLANES), axis=0, keepdims=True)
    return mx, idx


def _route(logits_t, run):
    assert N_GROUPS == EXPERTS_PER_GROUP <= SUBLANES
    n = logits_t.shape[1]
    row = lax.broadcasted_iota(jnp.int32, (SUBLANES, n), 0)
    valid = row < N_GROUPS
    glog = logits_t[:SUBLANES]
    gmax, g_star = _first_max_row(glog, valid, row)
    p_gsel = 1.0 / jnp.sum(jnp.where(valid, jnp.exp(glog - gmax), 0.0), axis=0, keepdims=True)
    elog = logits_t[SUBLANES:2 * SUBLANES]
    for g in range(1, N_GROUPS):
        elog = jnp.where(g_star == g, logits_t[(g + 1) * SUBLANES:(g + 2) * SUBLANES], elog)
    emax, _ = _first_max_row(elog, valid, row)
    ex = jnp.where(valid, jnp.exp(elog - emax), 0.0)
    p_exp = ex / jnp.sum(ex, axis=0, keepdims=True)
    v1, i1 = _first_max_row(p_exp, valid, row)
    v2, i2 = _first_max_row(p_exp, valid & (row != i1), row)
    tot = v1 + v2
    w = jnp.where(row == i1, p_gsel * (v1 / tot), jnp.where(row == i2, p_gsel * (v2 / tot), 0.0))
    onehot = jnp.where(row == g_star, 1.0, 0.0).astype(BF16)
    cb = min(n, CUMSUM_BLOCK)
    r_i = lax.broadcasted_iota(jnp.int32, (cb, cb), 0)
    c_i = lax.broadcasted_iota(jnp.int32, (cb, cb), 1)
    tri = jnp.where(r_i <= c_i, 1.0, 0.0).astype(BF16)
    pieces = []
    for j in range(n // cb):
        piece = _dot(onehot[:, j * cb:(j + 1) * cb], tri) + run
        pieces.append(piece)
        run = piece[:, cb - 1:cb]
    cum = jnp.concatenate(pieces, axis=1)
    rank = jnp.sum(jnp.where(row == g_star, cum, 0.0), axis=0, keepdims=True) - 1.0
    g_f = g_star.astype(F32)
    meta_t = jnp.where(valid, w, jnp.where(row == LANE_GROUP, g_f, jnp.where(
        row == LANE_RANK, rank, jnp.where(row == LANE_CODE, rank * N_GROUPS + g_f, 0.0))))
    return meta_t, cum[:, n - 1:n]


def _xattn_body(x_ref, g_ref, wq_t_ref, mem_ref, gmem_ref, wk_ref, wv_t_ref, wo_ref, gm_ref,
                wr_t_ref, br_t_ref, xa_ref, cnt_ref, code_ref, run_ref, k_ref, v_t_ref):
    @pl.when((pl.program_id(0) == 0) & (pl.program_id(1) == 0))
    def _():
        run_ref[...] = jnp.zeros_like(run_ref)

    @pl.when(pl.program_id(1) == 0)
    def _():
        hmem = _rms(mem_ref[...], gmem_ref[...]).astype(BF16)
        k_ref[...] = _dot(hmem, wk_ref[...]).astype(BF16)
        v_t_ref[...] = _dot_nt(wv_t_ref[...], hmem).astype(BF16)

    x = x_ref[...]
    tx = x.shape[0]
    scale = XA_HEAD_DIM ** -0.5
    q_t = _dot_nt(wq_t_ref[...], _rms(x, g_ref[...]).astype(BF16)).astype(BF16)
    head_rows = [slice(h * XA_HEAD_DIM, (h + 1) * XA_HEAD_DIM) for h in range(XA_HEADS)]
    scores = [_dot(k_ref[:, sl], q_t[sl, :]) * scale for sl in head_rows]
    heads = []
    for sl, s_t in zip(head_rows, scores):
        p = jnp.exp(s_t - jnp.max(s_t, axis=0, keepdims=True))
        l = jnp.sum(p, axis=0, keepdims=True)
        heads.append((_dot(v_t_ref[sl, :], p.astype(BF16)) / l).astype(BF16))
    x2 = x + _dot_tn(jnp.concatenate(heads, axis=0), wo_ref[...])
    xa_ref[:, :D_MODEL] = x2
    hm = _rms(x2, gm_ref[...]).astype(BF16)
    meta_t, run = _route(_dot_nt(wr_t_ref[...], hm) + br_t_ref[...], run_ref[...])
    code_ref[...] = meta_t.astype(jnp.int32)
    xa_ref[:, D_MODEL:] = jnp.concatenate(
        [meta_t, jnp.zeros((META_LANES - CODE_ROWS, tx), F32)], axis=0).T
    run_ref[...] = run
    cnt_ref[...] = jnp.broadcast_to(run, cnt_ref.shape).astype(jnp.int32)


def _xattn_route(x, g, wq_t, mem, gmem, wk, wv_t, wo, gm, wr_t, br_t, *, tx):
    B, T, D = x.shape
    row = lambda b, i: (b, i, 0)
    return pl.pallas_call(
        _xattn_body,
        grid=(B, T // tx),
        in_specs=[pl.BlockSpec((None, tx, D), row), _const_spec((1, D)),
                  _const_spec(wq_t.shape),
                  pl.BlockSpec((None, MEM_LEN, D), lambda b, i: (b, 0, 0)), _const_spec((1, D)),
                  _const_spec(wk.shape), _const_spec(wv_t.shape),
                  _const_spec(wo.shape), _const_spec((1, D)),
                  _const_spec(wr_t.shape), _const_spec(br_t.shape)],
        out_specs=[pl.BlockSpec((None, tx, XA_WIDTH), row), _const_spec((CODE_ROWS, META_LANES)),
                   pl.BlockSpec((None, None, CODE_ROWS, tx), lambda b, i: (b, i, 0, 0))],
        out_shape=[jax.ShapeDtypeStruct((B, T, XA_WIDTH), F32),
                   jax.ShapeDtypeStruct((CODE_ROWS, META_LANES), jnp.int32),
                   jax.ShapeDtypeStruct((B, T // tx, CODE_ROWS, tx), jnp.int32)],
        scratch_shapes=[pltpu.VMEM((CODE_ROWS, 1), F32), pltpu.VMEM((MEM_LEN, XA_DIM), BF16),
                        pltpu.VMEM((XA_DIM, MEM_LEN), BF16)],
        compiler_params=pltpu.CompilerParams(
            dimension_semantics=("arbitrary", "arbitrary"), vmem_limit_bytes=VMEM_LIMIT),
        name="xattn_route",
    )(x, g, wq_t, mem, gmem, wk, wv_t, wo, gm, wr_t, br_t)


VISIT_FIRST, VISIT_LAST, VISIT_VALID = 1, 2, 4


def _moe_plan_body(code_ref, cnt_ref, inv_ref, vt_ref, vg_ref, vf_ref, slot_vmem, slot_smem, *, ts):
    rows, lanes = code_ref.shape
    n = rows * lanes
    nt, nv = n // ts, vt_ref.shape[0]
    ends = [cnt_ref[0, 0]]
    for g in range(1, N_GROUPS - 1):
        ends.append(ends[-1] + cnt_ref[g, 0])
    group_bits = N_GROUPS.bit_length() - 1
    assert 1 << group_bits == N_GROUPS
    code = code_ref[...]
    grp = code & (N_GROUPS - 1)
    slot = lax.shift_right_logical(code, group_bits)
    for g in range(1, N_GROUPS):
        slot = slot + jnp.where(grp >= g, cnt_ref[g - 1, 0], 0)
    slot_vmem[...] = slot
    pltpu.sync_copy(slot_vmem, slot_smem)

    def place(i, _):
        for j in range(lanes):
            inv_ref[slot_smem[i, j]] = i * lanes + j
        return 0

    lax.fori_loop(0, rows, place, 0)

    def group_of(slot):
        return sum(jnp.where(e <= slot, 1, 0) for e in ends)

    def tile(j, v):
        g_lo, g_hi = group_of(j * ts), group_of(j * ts + ts - 1)
        for k in range(N_GROUPS):
            @pl.when(k <= g_hi - g_lo)
            def _():
                vt_ref[v + k] = j
                vg_ref[v + k] = g_lo + k
                vf_ref[v + k] = (VISIT_VALID + (VISIT_FIRST if k == 0 else 0)
                                 + jnp.where(k == g_hi - g_lo, VISIT_LAST, 0))
        return v + g_hi - g_lo + 1

    used = lax.fori_loop(0, nt, tile, 0)
    g_last = group_of(n - 1)

    def pad(v, _):
        vt_ref[v] = nt - 1
        vg_ref[v] = g_last
        vf_ref[v] = 0
        return 0

    lax.fori_loop(used, nv, pad, 0)


def _moe_plan(code, cnt, *, ts):
    n = code.shape[0]
    nt = n // ts
    nv = nt + N_GROUPS - 1
    code2d = code.reshape(n // META_LANES, META_LANES)
    smem = pl.BlockSpec(memory_space=pltpu.SMEM)
    visits = jax.ShapeDtypeStruct((nv,), jnp.int32)
    inv, vt, vg, vf = pl.pallas_call(
        functools.partial(_moe_plan_body, ts=ts),
        in_specs=[pl.BlockSpec(memory_space=pltpu.VMEM), smem],
        out_specs=[smem, smem, smem, smem],
        out_shape=[jax.ShapeDtypeStruct((n,), jnp.int32), visits, visits, visits],
        scratch_shapes=[pltpu.VMEM(code2d.shape, jnp.int32), pltpu.SMEM(code2d.shape, jnp.int32)],
        name="moe_plan",
    )(code2d, cnt)
    return inv.reshape(nt, 1, ts), vt, vg, vf


def _moe_body(vt_ref, vg_ref, vf_ref, inv_cur_ref, inv_nxt_ref, inv_prv_ref, xa_hbm, gm_ref,
              wg32_ref, wu32_ref, wd32_ref, gf_ref, out_hbm, xbuf, obuf, hm_ref, yacc_ref,
              wg_ref, wu_ref, wd_ref, gsem, ssem, *, ts, nt, sub):
    v = pl.program_id(0)
    t, grp, flags = vt_ref[v], vg_ref[v], vf_ref[v]
    slot = lax.rem(t, 2)
    first = (flags & VISIT_FIRST) != 0
    overlapped = first & (t >= 1)

    def gather_rows(idx_ref, dst_slot, lo, hi):
        for r in range(lo, min(hi, ts)):
            pltpu.make_async_copy(xa_hbm.at[pl.ds(idx_ref[0, r], 1)],
                                  xbuf.at[dst_slot, pl.ds(r, 1)], gsem.at[dst_slot]).start()

    def gather_wait(dst_slot):
        pltpu.make_async_copy(xa_hbm.at[pl.ds(0, ts)], xbuf.at[dst_slot], gsem.at[dst_slot]).wait()

    def scatter_rows(idx_ref, src_slot, lo, hi):
        for r in range(lo, min(hi, ts)):
            pltpu.make_async_copy(obuf.at[src_slot, pl.ds(r, 1)],
                                  out_hbm.at[pl.ds(idx_ref[0, r], 1)], ssem.at[src_slot]).start()

    def scatter_wait(src_slot):
        pltpu.make_async_copy(obuf.at[src_slot], out_hbm.at[pl.ds(0, ts)], ssem.at[src_slot]).wait()

    def experts(issue_dma):
        lane = lax.broadcasted_iota(jnp.int32, (sub, META_LANES), 1)
        n_sub = ts // sub
        half = n_sub * 3 * EXPERTS_PER_GROUP // 2
        batch = -(-ts // half)
        done = [0, 0]
        dots = [0]

        def after_dot(r):
            if not issue_dma:
                return
            kind = 0 if dots[0] < half else 1
            dots[0] += 1
            lo = done[kind]
            if kind == 0:
                gather_rows(inv_nxt_ref, 1 - slot, lo, lo + batch)
            else:
                scatter_rows(inv_prv_ref, 1 - slot, lo, lo + batch)
            done[kind] = min(lo + batch, ts)

        for r in range(n_sub):
            rows = slice(r * sub, (r + 1) * sub)
            meta = xbuf[slot, rows, D_MODEL:]
            row_grp = jnp.sum(jnp.where(lane == LANE_GROUP, meta, 0.0), axis=-1, keepdims=True)
            member = row_grp == grp.astype(F32)
            hm = hm_ref[rows, :]
            y = jnp.zeros((sub, D_MODEL), F32)
            for e in range(EXPERTS_PER_GROUP):
                w = jnp.sum(jnp.where(lane == e, meta, 0.0), axis=-1, keepdims=True)
                w = jnp.where(member, w, 0.0)
                a = _dot(hm, wg_ref[e])
                after_dot(r)
                b = _dot(hm, wu_ref[e])
                after_dot(r)
                u = (a * _sigmoid(a)) * b * w
                y = y + _dot(u.astype(BF16), wd_ref[e])
                after_dot(r)
            yacc_ref[rows, :] += y
        if issue_dma:
            assert done == [ts, ts], done

    @pl.when(v == 0)
    def _():
        gather_rows(inv_cur_ref, 0, 0, ts)

    @pl.when((v == 0) | (grp != vg_ref[jnp.maximum(v - 1, 0)]))
    def _():
        wg_ref[...] = wg32_ref[...].astype(BF16)
        wu_ref[...] = wu32_ref[...].astype(BF16)
        wd_ref[...] = wd32_ref[...].astype(BF16)

    @pl.when(first)
    def _():
        gather_wait(slot)

        @pl.when(t == 0)
        def _():
            gather_rows(inv_nxt_ref, 1, 0, ts)

        hm_ref[...] = _rms(xbuf[slot, :, :D_MODEL], gm_ref[...]).astype(BF16)
        yacc_ref[...] = jnp.zeros_like(yacc_ref)

    @pl.when(overlapped)
    def _():
        experts(True)

    @pl.when(((flags & VISIT_VALID) != 0) & jnp.logical_not(overlapped))
    def _():
        experts(False)

    @pl.when((flags & VISIT_LAST) != 0)
    def _():
        @pl.when(t >= 2)
        def _():
            scatter_wait(slot)

        obuf[slot] = _rms(xbuf[slot, :, :D_MODEL] + yacc_ref[...], gf_ref[...])

    @pl.when(v == pl.num_programs(0) - 1)
    def _():
        last_slot = (nt - 1) % 2
        scatter_rows(inv_cur_ref, last_slot, 0, ts)
        scatter_wait(0)
        scatter_wait(1)
        gather_wait(1 - last_slot)


def _moe_sorted(xa, inv, vt, vg, vf, gm, wg, wu, wd, gf, *, ts, sub):
    N = xa.shape[0]
    nt = N // ts
    assert N % ts == 0 and ts % sub == 0 and nt >= 2
    E, D, FF = EXPERTS_PER_GROUP, D_MODEL, EXPERT_FF
    grp = lambda v, vt, vg, vf: (vg[v], 0, 0)
    const = lambda v, vt, vg, vf: (0, 0)
    smem_row = lambda fn: pl.BlockSpec((None, 1, ts), fn, memory_space=pltpu.SMEM)
    grid_spec = pltpu.PrefetchScalarGridSpec(
        num_scalar_prefetch=3,
        grid=(vt.shape[0],),
        in_specs=[
            smem_row(lambda v, vt, vg, vf: (vt[v], 0, 0)),
            smem_row(lambda v, vt, vg, vf: (jnp.minimum(vt[v] + 1, nt - 1), 0, 0)),
            smem_row(lambda v, vt, vg, vf: (jnp.maximum(vt[v] - 1, 0), 0, 0)),
            pl.BlockSpec(memory_space=pl.ANY),
            pl.BlockSpec((1, D), const),
            pl.BlockSpec((E, D, FF), grp), pl.BlockSpec((E, D, FF), grp),
            pl.BlockSpec((E, FF, D), grp),
            pl.BlockSpec((1, D), const),
        ],
        out_specs=pl.BlockSpec(memory_space=pl.ANY),
        scratch_shapes=[
            pltpu.VMEM((2, ts, XA_WIDTH), F32), pltpu.VMEM((2, ts, D), F32),
            pltpu.VMEM((ts, D), BF16), pltpu.VMEM((ts, D), F32),
            pltpu.VMEM((E, D, FF), BF16), pltpu.VMEM((E, D, FF), BF16), pltpu.VMEM((E, FF, D), BF16),
            pltpu.SemaphoreType.DMA((2,)), pltpu.SemaphoreType.DMA((2,)),
        ],
    )
    return pl.pallas_call(
        functools.partial(_moe_body, ts=ts, nt=nt, sub=sub),
        grid_spec=grid_spec,
        out_shape=jax.ShapeDtypeStruct((N, D), F32),
        compiler_params=pltpu.CompilerParams(
            dimension_semantics=("arbitrary",), vmem_limit_bytes=VMEM_LIMIT),
        name="moe_sorted",
    )(vt, vg, vf, inv, inv, inv, xa, gm, wg, wu, wd, gf)


def kernel(x, mem, norm_mix_g, w_in, fox_bf, conv_dw_w, conv_dw_b, conv_ln_g, conv_ln_b, conv_pw_w, attn_branch_w, w_out, norm_xa_g, norm_mem_g, xa_wq, xa_wk, xa_wv, xa_wo, norm_moe_g, router_group_w, router_group_b, router_expert_w, router_expert_b, expert_w_gate, expert_w_up, expert_w_down, norm_final_g):
    B, T, D = x.shape
    depth = w_in.shape[0]
    assert depth == 1, "the final norm is fused into the last layer's MoE kernel"
    row = lambda v: v.reshape(1, -1).astype(F32)
    for l in range(depth):
        u, gc, ga, qkv_t, f_t = _proj_in(x, row(norm_mix_g[l]), jnp.swapaxes(w_in, 1, 2).astype(F32),
                                         layer=l, tm=512)
        qaux, kaux = _fox_prep(f_t, fox_bf[l].reshape(FOX_HEADS, 1).astype(F32))
        attn_t = _fox_attn(qkv_t, qaux, kaux, tq=512, heads=1)
        x = _mix_out(x, u, gc, ga, attn_t, conv_dw_w[l].astype(F32), row(conv_dw_b[l]),
                     row(conv_ln_g[l]), row(conv_ln_b[l]), conv_pw_w[l].astype(F32),
                     attn_branch_w[l].astype(F32), w_out[l].astype(F32), tc=512)
        wr_t = jnp.zeros((ROUTE_ROWS, D), F32).at[:N_GROUPS].set(router_group_w[l].T)
        br_t = jnp.zeros((ROUTE_ROWS, 1), F32).at[:N_GROUPS, 0].set(router_group_b[l])
        for g in range(N_GROUPS):
            r0, e0 = SUBLANES * (g + 1), EXPERTS_PER_GROUP * g
            wr_t = wr_t.at[r0:r0 + EXPERTS_PER_GROUP].set(
                router_expert_w[l][:, e0:e0 + EXPERTS_PER_GROUP].T)
            br_t = br_t.at[r0:r0 + EXPERTS_PER_GROUP, 0].set(
                router_expert_b[l][e0:e0 + EXPERTS_PER_GROUP])
        xa, cnt, code = _xattn_route(x, row(norm_xa_g[l]), xa_wq[l].T.astype(BF16), mem,
                                     row(norm_mem_g[l]), xa_wk[l].astype(BF16),
                                     xa_wv[l].T.astype(BF16),
                                     xa_wo[l].astype(BF16), row(norm_moe_g[l]),
                                     wr_t.astype(BF16), br_t, tx=1024)
        xa = xa.reshape(B * T, XA_WIDTH)
        inv, vt, vg, vf = _moe_plan(code[:, :, LANE_CODE, :].reshape(B * T), cnt, ts=512)
        x = _moe_sorted(xa, inv, vt, vg, vf, row(norm_moe_g[l]), expert_w_gate[l].astype(F32),
                        expert_w_up[l].astype(F32), expert_w_down[l].astype(F32),
                        row(norm_final_g), ts=512, sub=512).reshape(B, T, D)
    return x
```

```python
import functools

import jax
import jax.numpy as jnp
from jax import lax
from jax.experimental import pallas as pl
from jax.experimental.pallas import tpu as pltpu

D_MODEL = 1024
MEM_LEN = 256
EPS = 1e-6
CONV_DIM = 512
CONV_WIDTH = 31
FOX_HEADS = 8
FOX_HEAD_DIM = 64
FOX_DIM = FOX_HEADS * FOX_HEAD_DIM
XA_HEADS = 4
XA_HEAD_DIM = 128
XA_DIM = XA_HEADS * XA_HEAD_DIM
N_GROUPS = 4
EXPERTS_PER_GROUP = 4
N_EXPERTS = N_GROUPS * EXPERTS_PER_GROUP
EXPERT_FF = 256

AUX_ROWS = 16
LOG2E = 1.4426950408889634
VMEM_LIMIT = 56 * 1024 * 1024

F32 = jnp.float32
BF16 = jnp.bfloat16


def _const_spec(shape):
    n = len(shape)
    return pl.BlockSpec(shape, lambda *_: (0,) * n)


def _rms(x, g):
    return x * lax.rsqrt(jnp.mean(x * x, axis=-1, keepdims=True) + EPS) * g


def _sigmoid(x):
    return 0.5 * jnp.tanh(0.5 * x) + 0.5


def _dot(a, b):
    return jnp.dot(a, b, preferred_element_type=F32)


def _dot_nt(a, b):
    return lax.dot_general(a, b, (((1,), (1,)), ((), ())), preferred_element_type=F32)


def _dot_tn(a, b):
    return lax.dot_general(a, b, (((0,), (0,)), ((), ())), preferred_element_type=F32)


C_GLU = 2 * CONV_DIM
C_F = C_GLU + 3 * FOX_DIM
C_GC = C_F + FOX_HEADS
C_GA = C_GC + D_MODEL
QKVF_PAD = 16


def _proj_in_body(x_ref, g_ref, w_ref, u_ref, gc_ref, ga_ref, qkv_t_ref, f_t_ref,
                  wglu_ref, wgc_ref, wga_ref, wqkv_t_ref):
    @pl.when((pl.program_id(0) == 0) & (pl.program_id(1) == 0))
    def _():
        for j in range(C_GLU // D_MODEL):
            rows = slice(j * D_MODEL, (j + 1) * D_MODEL)
            wglu_ref[:, rows] = w_ref[rows, :].T.astype(BF16)
        wgc_ref[...] = w_ref[C_GC:C_GA, :].T.astype(BF16)
        wga_ref[...] = w_ref[C_GA:, :].T.astype(BF16)
        wqkv_t_ref[:C_F - C_GLU, :] = w_ref[C_GLU:C_F, :].astype(BF16)
        wqkv_t_ref[C_F - C_GLU:, :] = jnp.concatenate(
            [w_ref[C_F:C_GC, :], jnp.zeros((QKVF_PAD - FOX_HEADS, D_MODEL), F32)], axis=0).astype(BF16)

    h = _rms(x_ref[...], g_ref[...]).astype(BF16)
    glu = _dot(h, wglu_ref[...])
    u_ref[...] = glu[:, :CONV_DIM] * _sigmoid(glu[:, CONV_DIM:])
    gc_ref[...] = _sigmoid(_dot(h, wgc_ref[...])).astype(BF16)
    ga_ref[...] = _sigmoid(_dot(h, wga_ref[...])).astype(BF16)
    qkv_t = _dot_nt(wqkv_t_ref[...], h)
    scale = FOX_HEAD_DIM ** -0.5 * LOG2E
    qkv_t_ref[:FOX_DIM, :] = (qkv_t[:FOX_DIM] * scale).astype(BF16)
    qkv_t_ref[FOX_DIM:, :] = qkv_t[FOX_DIM:3 * FOX_DIM].astype(BF16)
    f_t_ref[...] = qkv_t[3 * FOX_DIM:3 * FOX_DIM + FOX_HEADS]


def _proj_in(x, g, w, *, layer, tm):
    B, T, D = x.shape
    grid = (B, T // tm)
    row = lambda b, i: (b, i, 0)
    col = lambda b, i: (b, 0, i)
    return pl.pallas_call(
        _proj_in_body,
        grid=grid,
        in_specs=[
            pl.BlockSpec((None, tm, D), row),
            _const_spec((1, D)),
            pl.BlockSpec((None,) + w.shape[1:], lambda b, i: (layer, 0, 0),
                         pipeline_mode=pl.Buffered(1)),
        ],
        out_specs=[
            pl.BlockSpec((None, tm, CONV_DIM), row),
            pl.BlockSpec((None, tm, D), row),
            pl.BlockSpec((None, tm, D), row),
            pl.BlockSpec((None, 3 * FOX_DIM, tm), col),
            pl.BlockSpec((None, FOX_HEADS, tm), col),
        ],
        out_shape=[
            jax.ShapeDtypeStruct((B, T, CONV_DIM), F32),
            jax.ShapeDtypeStruct((B, T, D), BF16),
            jax.ShapeDtypeStruct((B, T, D), BF16),
            jax.ShapeDtypeStruct((B, 3 * FOX_DIM, T), BF16),
            jax.ShapeDtypeStruct((B, FOX_HEADS, T), F32),
        ],
        scratch_shapes=[pltpu.VMEM((D, C_GLU), BF16), pltpu.VMEM((D, D), BF16),
                        pltpu.VMEM((D, D), BF16), pltpu.VMEM((3 * FOX_DIM + QKVF_PAD, D), BF16)],
        compiler_params=pltpu.CompilerParams(
            dimension_semantics=("arbitrary", "arbitrary"), vmem_limit_bytes=VMEM_LIMIT),
        name="proj_in",
    )(x, g, w)


def _split3(c):
    hi = c.astype(BF16)
    r = c - hi.astype(F32)
    mid = r.astype(BF16)
    lo = (r - mid.astype(F32)).astype(BF16)
    return hi, mid, lo


def _fox_prep_body(f_t_ref, bf_ref, qaux_ref, kaux_ref, c_ref, *, chunk):
    T = f_t_ref.shape[-1]
    z = f_t_ref[...] + bf_ref[...]
    logf = jnp.minimum(z, 0.0) - jnp.log1p(jnp.exp(-jnp.abs(z)))
    r_i = lax.broadcasted_iota(jnp.int32, (chunk, chunk), 0)
    c_i = lax.broadcasted_iota(jnp.int32, (chunk, chunk), 1)
    tri = jnp.where(r_i <= c_i, 1.0, 0.0).astype(BF16)
    carry = jnp.zeros((FOX_HEADS, 1), F32)
    for j in range(T // chunk):
        blk = logf[:, j * chunk:(j + 1) * chunk]
        hi, mid, lo = _split3(blk)
        cs = (_dot(lo, tri) + _dot(mid, tri)) + _dot(hi, tri) + carry
        c_ref[:, j * chunk:(j + 1) * chunk] = cs
        carry = cs[:, chunk - 1:chunk]
    row = lax.broadcasted_iota(jnp.int32, (AUX_ROWS, T), 0)
    for h in range(FOX_HEADS):
        hi, mid, lo = _split3(c_ref[h:h + 1, :] * LOG2E)
        pieces = [jnp.broadcast_to(p.astype(F32), (AUX_ROWS, T)) for p in (hi, mid, lo)]
        one = jnp.where(row < 3, 1.0, 0.0)
        val = jnp.where(row == 0, pieces[0], jnp.where(row == 1, pieces[1], pieces[2]))
        val_k = jnp.where(row == 3, pieces[0], jnp.where(row == 4, pieces[1], pieces[2]))
        qaux = jnp.where(row < 3, val, jnp.where(row < 6, 1.0, 0.0))
        kaux = jnp.where(row < 3, one, jnp.where(row < 6, -val_k, 0.0))
        qaux_ref[h] = qaux.astype(BF16)
        kaux_ref[h] = kaux.astype(BF16)


def _fox_prep(f_t, bf):
    B, H, T = f_t.shape
    aux = jax.ShapeDtypeStruct((B, H, AUX_ROWS, T), BF16)
    aux_spec = pl.BlockSpec((None, H, AUX_ROWS, T), lambda b: (b, 0, 0, 0))
    return pl.pallas_call(
        functools.partial(_fox_prep_body, chunk=512),
        grid=(B,),
        in_specs=[pl.BlockSpec((None, H, T), lambda b: (b, 0, 0)), _const_spec((H, 1))],
        out_specs=[aux_spec, aux_spec],
        out_shape=[aux, aux],
        scratch_shapes=[pltpu.VMEM((H, T), F32)],
        compiler_params=pltpu.CompilerParams(dimension_semantics=("parallel",)),
        name="fox_prep",
    )(f_t, bf)


MASK_BLOCK = 256


def _fox_attn_body(q_ref, k_ref, v_ref, qaux_ref, kaux_ref, o_ref, *scratch, tq, heads):
    T = q_ref.shape[-1]
    dh = FOX_HEAD_DIM
    mb = MASK_BLOCK
    nj = tq // mb
    kk = lax.broadcasted_iota(jnp.int32, (mb, mb), 0)
    qq = lax.broadcasted_iota(jnp.int32, (mb, mb), 1)
    causal = kk <= qq
    ones = jnp.where(lax.broadcasted_iota(jnp.int32, (AUX_ROWS, tq), 0) == 0, 1.0, 0.0).astype(BF16)
    sbufs = [scratch[2 * h:2 * h + 2] for h in range(heads)]
    pbufs = scratch[2 * heads:]

    tasks = []
    for qi in range(T // tq):
        tasks.append((qi, qi, True))
        tasks += [(qi, c, False) for c in range(qi)]

    def scores(h, task, j, sb, mc):
        qi, c, diag = task
        c0 = j * mb if diag else 0
        r0 = c * tq + j * mb
        hd = slice(h * dh, (h + 1) * dh)
        cols = slice(qi * tq + c0, (qi + 1) * tq)
        ka = jnp.concatenate([k_ref[hd, r0:r0 + mb], kaux_ref[h, :, r0:r0 + mb]], axis=0)
        qa = jnp.concatenate([q_ref[hd, cols], qaux_ref[h, :, cols]], axis=0)
        s = _dot_tn(ka, qa)
        if diag:
            head = jnp.where(causal, s[:, :mb], -jnp.inf)
            s = head if tq - c0 == mb else jnp.concatenate([head, s[:, mb:]], axis=1)
        sb[j * mb:(j + 1) * mb, c0:] = s
        cm = jnp.max(s, axis=0, keepdims=True)
        if c0:
            cm = jnp.concatenate([jnp.full((1, c0), -jnp.inf, F32), cm], axis=1)
        return cm if mc is None else jnp.maximum(mc, cm)

    def probs(h, task, j, sb, m_new):
        c0 = j * mb if task[2] else 0
        rows = slice(j * mb, (j + 1) * mb)
        pbufs[h][rows, c0:] = jnp.exp2(sb[rows, c0:] - m_new[:, c0:]).astype(BF16)

    def values(h, task):
        _, c, diag = task
        p_ref = pbufs[h]
        va = jnp.concatenate([v_ref[h * dh:(h + 1) * dh, c * tq:(c + 1) * tq], ones], axis=0)
        if not diag:
            return _dot(va, p_ref[...])
        return jnp.concatenate(
            [_dot(va[:, :(bj + 1) * mb], p_ref[:(bj + 1) * mb, bj * mb:(bj + 1) * mb])
             for bj in range(nj)], axis=1)

    hs = range(heads)
    mc = [None] * heads
    for j in range(nj):
        for h in hs:
            mc[h] = scores(h, tasks[0], j, sbufs[h][0], mc[h])
    m_run, acc = [None] * heads, [None] * heads
    for k, task in enumerate(tasks):
        qi, c, diag = task
        nxt = tasks[k + 1] if k + 1 < len(tasks) else None
        m_new = [mc[h] if diag else jnp.maximum(m_run[h], mc[h]) for h in hs]
        mc = [None] * heads
        for j in range(nj):
            for h in hs:
                if nxt is not None:
                    mc[h] = scores(h, nxt, j, sbufs[h][(k + 1) % 2], mc[h])
                probs(h, task, j, sbufs[h][k % 2], m_new[h])
        for h in hs:
            pv = values(h, task)
            acc[h] = pv if diag else jnp.exp2(m_run[h] - m_new[h]) * acc[h] + pv
            if nxt is None or nxt[0] != qi:
                o_ref[h * dh:(h + 1) * dh, qi * tq:(qi + 1) * tq] = (
                    acc[h][:dh] / acc[h][dh:dh + 1]).astype(o_ref.dtype)
        m_run = m_new


def _fox_attn(qkv_t, qaux, kaux, *, tq, heads):
    B, _, T = qkv_t.shape
    H, dh = FOX_HEADS, FOX_HEAD_DIM
    assert T % tq == 0 and tq % MASK_BLOCK == 0 and H % heads == 0
    hg = H // heads
    part = lambda off: pl.BlockSpec((None, heads * dh, T), lambda b, g: (b, off * hg + g, 0))
    aux = pl.BlockSpec((None, heads, AUX_ROWS, T), lambda b, g: (b, g, 0, 0))
    return pl.pallas_call(
        functools.partial(_fox_attn_body, tq=tq, heads=heads),
        grid=(B, hg),
        in_specs=[part(0), part(1), part(2), aux, aux],
        out_specs=pl.BlockSpec((None, heads * dh, T), lambda b, g: (b, g, 0)),
        out_shape=jax.ShapeDtypeStruct((B, H * dh, T), BF16),
        scratch_shapes=([pltpu.VMEM((tq, tq), F32)] * (2 * heads)
                        + [pltpu.VMEM((tq, tq), BF16)] * heads),
        compiler_params=pltpu.CompilerParams(
            dimension_semantics=("parallel", "parallel"), vmem_limit_bytes=VMEM_LIMIT),
        name="fox_attn",
    )(qkv_t, qkv_t, qkv_t, qaux, kaux)


HALO = 32
SUBLANES = 8
MIX_ROW_BLOCK = 256


def _mix_out_body(x_ref, u_ref, uprev_ref, gc_ref, ga_ref, attn_t_ref, dww_ref, dwb_ref,
                  lng_ref, lnb_ref, pw32_ref, abw32_ref, wout32_ref, o_ref, ubuf_ref, ushift_ref,
                  pw_ref, abw_ref, wout_ref, *, tc):
    first = pl.program_id(1) == 0

    @pl.when(first & (pl.program_id(0) == 0))
    def _():
        pw_ref[...] = pw32_ref[...].astype(BF16)
        abw_ref[...] = abw32_ref[...].astype(BF16)
        wout_ref[...] = wout32_ref[...].astype(BF16)

    ubuf_ref[:HALO, :] = jnp.where(first, 0.0, uprev_ref[...])
    ubuf_ref[HALO:, :] = u_ref[...]
    span = tc + HALO - SUBLANES
    for s in range(1, SUBLANES):
        ushift_ref[s - 1, :, :] = ubuf_ref[s:s + span, :]

    rb = MIX_ROW_BLOCK

    def conv_branch(r0):
        acc = jnp.zeros((rb, CONV_DIM), F32)
        for k in range(CONV_WIDTH):
            off = r0 + HALO - (CONV_WIDTH - 1) + k
            s, base = off % SUBLANES, off - off % SUBLANES
            rows = ubuf_ref[base:base + rb, :] if s == 0 else ushift_ref[s - 1, base:base + rb, :]
            acc = acc + rows * dww_ref[k:k + 1, :]
        acc = acc + dwb_ref[...]
        mu = jnp.mean(acc, axis=-1, keepdims=True)
        d = acc - mu
        var = jnp.mean(d * d, axis=-1, keepdims=True)
        y = d * lax.rsqrt(var + EPS) * lng_ref[...] + lnb_ref[...]
        return (y * _sigmoid(y)).astype(BF16)

    def project(r0, y):
        rows = slice(r0, r0 + rb)
        conv_out = _dot(y, pw_ref[...])
        attn_out = _dot_tn(attn_t_ref[:, rows], abw_ref[...])
        merged = gc_ref[rows, :].astype(F32) * conv_out + ga_ref[rows, :].astype(F32) * attn_out
        o_ref[rows, :] = x_ref[rows, :] + _dot(merged.astype(BF16), wout_ref[...])

    y = conv_branch(0)
    for r0 in range(0, tc, rb):
        y_next = conv_branch(r0 + rb) if r0 + rb < tc else None
        project(r0, y)
        y = y_next


def _mix_out(x, u, gc, ga, attn_t, dww, dwb, lng, lnb, pw, abw, wout, *, tc):
    B, T, D = x.shape
    assert T % tc == 0 and tc % MIX_ROW_BLOCK == 0 and tc % HALO == 0 and HALO >= CONV_WIDTH - 1
    row = lambda b, i: (b, i, 0)
    halo_blocks = tc // HALO
    return pl.pallas_call(
        functools.partial(_mix_out_body, tc=tc),
        grid=(B, T // tc),
        in_specs=[
            pl.BlockSpec((None, tc, D), row),
            pl.BlockSpec((None, tc, CONV_DIM), row),
            pl.BlockSpec((None, HALO, CONV_DIM),
                         lambda b, i: (b, jnp.maximum(i * halo_blocks - 1, 0), 0)),
            pl.BlockSpec((None, tc, D), row),
            pl.BlockSpec((None, tc, D), row),
            pl.BlockSpec((None, FOX_DIM, tc), lambda b, i: (b, 0, i)),
            _const_spec(dww.shape), _const_spec(dwb.shape), _const_spec(lng.shape),
            _const_spec(lnb.shape), _const_spec(pw.shape), _const_spec(abw.shape),
            _const_spec(wout.shape),
        ],
        out_specs=pl.BlockSpec((None, tc, D), row),
        out_shape=jax.ShapeDtypeStruct((B, T, D), F32),
        scratch_shapes=[pltpu.VMEM((HALO + tc, CONV_DIM), F32),
                        pltpu.VMEM((SUBLANES - 1, HALO + tc - SUBLANES, CONV_DIM), F32),
                        pltpu.VMEM(pw.shape, BF16), pltpu.VMEM(abw.shape, BF16),
                        pltpu.VMEM(wout.shape, BF16)],
        compiler_params=pltpu.CompilerParams(
            dimension_semantics=("arbitrary", "arbitrary"), vmem_limit_bytes=VMEM_LIMIT),
        name="mix_out",
    )(x, u, u, gc, ga, attn_t, dww, dwb, lng, lnb, pw, abw, wout)


META_LANES = 128
XA_WIDTH = D_MODEL + META_LANES
LANE_GROUP = EXPERTS_PER_GROUP
LANE_RANK = EXPERTS_PER_GROUP + 1
LANE_CODE = EXPERTS_PER_GROUP + 2
CODE_ROWS = SUBLANES
ROUTE_ROWS = SUBLANES * (N_GROUPS + 1)
CUMSUM_BLOCK = 256


def _first_max_row(v, valid, row):
    neg = jnp.where(valid, v, -jnp.inf)
    mx = jnp.max(neg, axis=0, keepdims=True)
    idx = jnp.min(jnp.where(valid & (neg == mx), row, SUBLANES), axis=0, keepdims=True)
    return mx, idx


def _route(logits_t, run):
    assert N_GROUPS == EXPERTS_PER_GROUP <= SUBLANES
    n = logits_t.shape[1]
    row = lax.broadcasted_iota(jnp.int32, (SUBLANES, n), 0)
    valid = row < N_GROUPS
    glog = logits_t[:SUBLANES]
    gmax, g_star = _first_max_row(glog, valid, row)
    p_gsel = 1.0 / jnp.sum(jnp.where(valid, jnp.exp(glog - gmax), 0.0), axis=0, keepdims=True)
    elog = logits_t[SUBLANES:2 * SUBLANES]
    for g in range(1, N_GROUPS):
        elog = jnp.where(g_star == g, logits_t[(g + 1) * SUBLANES:(g + 2) * SUBLANES], elog)
    emax, _ = _first_max_row(elog, valid, row)
    ex = jnp.where(valid, jnp.exp(elog - emax), 0.0)
    p_exp = ex / jnp.sum(ex, axis=0, keepdims=True)
    v1, i1 = _first_max_row(p_exp, valid, row)
    v2, i2 = _first_max_row(p_exp, valid & (row != i1), row)
    tot = v1 + v2
    w = jnp.where(row == i1, p_gsel * (v1 / tot), jnp.where(row == i2, p_gsel * (v2 / tot), 0.0))
    onehot = jnp.where(row == g_star, 1.0, 0.0).astype(BF16)
    cb = min(n, CUMSUM_BLOCK)
    r_i = lax.broadcasted_iota(jnp.int32, (cb, cb), 0)
    c_i = lax.broadcasted_iota(jnp.int32, (cb, cb), 1)
    tri = jnp.where(r_i <= c_i, 1.0, 0.0).astype(BF16)
    pieces = []
    for j in range(n // cb):
        piece = _dot(onehot[:, j * cb:(j + 1) * cb], tri) + run
        pieces.append(piece)
        run = piece[:, cb - 1:cb]
    cum = jnp.concatenate(pieces, axis=1)
    rank = jnp.sum(jnp.where(row == g_star, cum, 0.0), axis=0, keepdims=True) - 1.0
    g_f = g_star.astype(F32)
    meta_t = jnp.where(valid, w, jnp.where(row == LANE_GROUP, g_f, jnp.where(
        row == LANE_RANK, rank, jnp.where(row == LANE_CODE, rank * N_GROUPS + g_f, 0.0))))
    return meta_t, cum[:, n - 1:n]


def _xattn_body(x_ref, g_ref, wq_t_ref, mem_ref, gmem_ref, wk_ref, wv_t_ref, wo_ref, gm_ref,
                wr_t_ref, br_t_ref, xa_ref, cnt_ref, code_ref, run_ref, k_ref, v_t_ref):
    @pl.when((pl.program_id(0) == 0) & (pl.program_id(1) == 0))
    def _():
        run_ref[...] = jnp.zeros_like(run_ref)

    @pl.when(pl.program_id(1) == 0)
    def _():
        hmem = _rms(mem_ref[...], gmem_ref[...]).astype(BF16)
        k_ref[...] = _dot(hmem, wk_ref[...]).astype(BF16)
        v_t_ref[...] = _dot_nt(wv_t_ref[...], hmem).astype(BF16)

    x = x_ref[...]
    tx = x.shape[0]
    scale = XA_HEAD_DIM ** -0.5
    q_t = _dot_nt(wq_t_ref[...], _rms(x, g_ref[...]).astype(BF16)).astype(BF16)
    head_rows = [slice(h * XA_HEAD_DIM, (h + 1) * XA_HEAD_DIM) for h in range(XA_HEADS)]
    scores = [_dot(k_ref[:, sl], q_t[sl, :]) * scale for sl in head_rows]
    heads = []
    for sl, s_t in zip(head_rows, scores):
        p = jnp.exp(s_t - jnp.max(s_t, axis=0, keepdims=True))
        l = jnp.sum(p, axis=0, keepdims=True)
        heads.append((_dot(v_t_ref[sl, :], p.astype(BF16)) / l).astype(BF16))
    x2 = x + _dot_tn(jnp.concatenate(heads, axis=0), wo_ref[...])
    xa_ref[:, :D_MODEL] = x2
    hm = _rms(x2, gm_ref[...]).astype(BF16)
    meta_t, run = _route(_dot_nt(wr_t_ref[...], hm) + br_t_ref[...], run_ref[...])
    code_ref[...] = meta_t.astype(jnp.int32)
    xa_ref[:, D_MODEL:] = jnp.concatenate(
        [meta_t, jnp.zeros((META_LANES - CODE_ROWS, tx), F32)], axis=0).T
    run_ref[...] = run
    cnt_ref[...] = jnp.broadcast_to(run, cnt_ref.shape).astype(jnp.int32)


def _xattn_route(x, g, wq_t, mem, gmem, wk, wv_t, wo, gm, wr_t, br_t, *, tx):
    B, T, D = x.shape
    row = lambda b, i: (b, i, 0)
    return pl.pallas_call(
        _xattn_body,
        grid=(B, T // tx),
        in_specs=[pl.BlockSpec((None, tx, D), row), _const_spec((1, D)),
                  _const_spec(wq_t.shape),
                  pl.BlockSpec((None, MEM_LEN, D), lambda b, i: (b, 0, 0)), _const_spec((1, D)),
                  _const_spec(wk.shape), _const_spec(wv_t.shape),
                  _const_spec(wo.shape), _const_spec((1, D)),
                  _const_spec(wr_t.shape), _const_spec(br_t.shape)],
        out_specs=[pl.BlockSpec((None, tx, XA_WIDTH), row), _const_spec((CODE_ROWS, META_LANES)),
                   pl.BlockSpec((None, None, CODE_ROWS, tx), lambda b, i: (b, i, 0, 0))],
        out_shape=[jax.ShapeDtypeStruct((B, T, XA_WIDTH), F32),
                   jax.ShapeDtypeStruct((CODE_ROWS, META_LANES), jnp.int32),
                   jax.ShapeDtypeStruct((B, T // tx, CODE_ROWS, tx), jnp.int32)],
        scratch_shapes=[pltpu.VMEM((CODE_ROWS, 1), F32), pltpu.VMEM((MEM_LEN, XA_DIM), BF16),
                        pltpu.VMEM((XA_DIM, MEM_LEN), BF16)],
        compiler_params=pltpu.CompilerParams(
            dimension_semantics=("arbitrary", "arbitrary"), vmem_limit_bytes=VMEM_LIMIT),
        name="xattn_route",
    )(x, g, wq_t, mem, gmem, wk, wv_t, wo, gm, wr_t, br_t)


VISIT_FIRST, VISIT_LAST, VISIT_VALID = 1, 2, 4


def _moe_plan_body(code_ref, cnt_ref, inv_ref, vt_ref, vg_ref, vf_ref, slot_vmem, slot_smem, *, ts):
    rows, lanes = code_ref.shape
    n = rows * lanes
    nt, nv = n // ts, vt_ref.shape[0]
    ends = [cnt_ref[0, 0]]
    for g in range(1, N_GROUPS - 1):
        ends.append(ends[-1] + cnt_ref[g, 0])
    group_bits = N_GROUPS.bit_length() - 1
    assert 1 << group_bits == N_GROUPS
    code = code_ref[...]
    grp = code & (N_GROUPS - 1)
    slot = lax.shift_right_logical(code, group_bits)
    for g in range(1, N_GROUPS):
        slot = slot + jnp.where(grp >= g, cnt_ref[g - 1, 0], 0)
    slot_vmem[...] = slot
    pltpu.sync_copy(slot_vmem, slot_smem)

    def place(i, _):
        for j in range(lanes):
            inv_ref[slot_smem[i, j]] = i * lanes + j
        return 0

    lax.fori_loop(0, rows, place, 0)

    def group_of(slot):
        return sum(jnp.where(e <= slot, 1, 0) for e in ends)

    def tile(j, v):
        g_lo, g_hi = group_of(j * ts), group_of(j * ts + ts - 1)
        for k in range(N_GROUPS):
            @pl.when(k <= g_hi - g_lo)
            def _():
                vt_ref[v + k] = j
                vg_ref[v + k] = g_lo + k
                vf_ref[v + k] = (VISIT_VALID + (VISIT_FIRST if k == 0 else 0)
                                 + jnp.where(k == g_hi - g_lo, VISIT_LAST, 0))
        return v + g_hi - g_lo + 1

    used = lax.fori_loop(0, nt, tile, 0)
    g_last = group_of(n - 1)

    def pad(v, _):
        vt_ref[v] = nt - 1
        vg_ref[v] = g_last
        vf_ref[v] = 0
        return 0

    lax.fori_loop(used, nv, pad, 0)


def _moe_plan(code, cnt, *, ts):
    n = code.shape[0]
    nt = n // ts
    nv = nt + N_GROUPS - 1
    code2d = code.reshape(n // META_LANES, META_LANES)
    smem = pl.BlockSpec(memory_space=pltpu.SMEM)
    visits = jax.ShapeDtypeStruct((nv,), jnp.int32)
    inv, vt, vg, vf = pl.pallas_call(
        functools.partial(_moe_plan_body, ts=ts),
        in_specs=[pl.BlockSpec(memory_space=pltpu.VMEM), smem],
        out_specs=[smem, smem, smem, smem],
        out_shape=[jax.ShapeDtypeStruct((n,), jnp.int32), visits, visits, visits],
        scratch_shapes=[pltpu.VMEM(code2d.shape, jnp.int32), pltpu.SMEM(code2d.shape, jnp.int32)],
        name="moe_plan",
    )(code2d, cnt)
    return inv.reshape(nt, 1, ts), vt, vg, vf


def _moe_body(vt_ref, vg_ref, vf_ref, inv_cur_ref, inv_nxt_ref, inv_prv_ref, xa_hbm, gm_ref,
              wg32_ref, wu32_ref, wd32_ref, gf_ref, out_hbm, xbuf, obuf, hm_ref, yacc_ref,
              wg_ref, wu_ref, wd_ref, gsem, ssem, *, ts, nt, sub):
    v = pl.program_id(0)
    t, grp, flags = vt_ref[v], vg_ref[v], vf_ref[v]
    slot = lax.rem(t, 2)
    first = (flags & VISIT_FIRST) != 0
    overlapped = first & (t >= 1)

    def gather_rows(idx_ref, dst_slot, lo, hi):
        for r in range(lo, min(hi, ts)):
            pltpu.make_async_copy(xa_hbm.at[pl.ds(idx_ref[0, r], 1)],
                                  xbuf.at[dst_slot, pl.ds(r, 1)], gsem.at[dst_slot]).start()

    def gather_wait(dst_slot):
        pltpu.make_async_copy(xa_hbm.at[pl.ds(0, ts)], xbuf.at[dst_slot], gsem.at[dst_slot]).wait()

    def scatter_rows(idx_ref, src_slot, lo, hi):
        for r in range(lo, min(hi, ts)):
            pltpu.make_async_copy(obuf.at[src_slot, pl.ds(r, 1)],
                                  out_hbm.at[pl.ds(idx_ref[0, r], 1)], ssem.at[src_slot]).start()

    def scatter_wait(src_slot):
        pltpu.make_async_copy(obuf.at[src_slot], out_hbm.at[pl.ds(0, ts)], ssem.at[src_slot]).wait()

    def experts(issue_dma):
        lane = lax.broadcasted_iota(jnp.int32, (sub, META_LANES), 1)
        n_sub = ts // sub
        half = n_sub * 3 * EXPERTS_PER_GROUP // 2
        batch = -(-ts // half)
        done = [0, 0]
        dots = [0]

        def after_dot(r):
            if not issue_dma:
                return
            kind = 0 if dots[0] < half else 1
            dots[0] += 1
            lo = done[kind]
            if kind == 0:
                gather_rows(inv_nxt_ref, 1 - slot, lo, lo + batch)
            else:
                scatter_rows(inv_prv_ref, 1 - slot, lo, lo + batch)
            done[kind] = min(lo + batch, ts)

        for r in range(n_sub):
            rows = slice(r * sub, (r + 1) * sub)
            meta = xbuf[slot, rows, D_MODEL:]
            row_grp = jnp.sum(jnp.where(lane == LANE_GROUP, meta, 0.0), axis=-1, keepdims=True)
            member = row_grp == grp.astype(F32)
            hm = hm_ref[rows, :]
            y = jnp.zeros((sub, D_MODEL), F32)
            for e in range(EXPERTS_PER_GROUP):
                w = jnp.sum(jnp.where(lane == e, meta, 0.0), axis=-1, keepdims=True)
                w = jnp.where(member, w, 0.0)
                a = _dot(hm, wg_ref[e])
                after_dot(r)
                b = _dot(hm, wu_ref[e])
                after_dot(r)
                u = (a * _sigmoid(a)) * b * w
                y = y + _dot(u.astype(BF16), wd_ref[e])
                after_dot(r)
            yacc_ref[rows, :] += y
        if issue_dma:
            assert done == [ts, ts], done

    @pl.when(v == 0)
    def _():
        gather_rows(inv_cur_ref, 0, 0, ts)

    @pl.when((v == 0) | (grp != vg_ref[jnp.maximum(v - 1, 0)]))
    def _():
        wg_ref[...] = wg32_ref[...].astype(BF16)
        wu_ref[...] = wu32_ref[...].astype(BF16)
        wd_ref[...] = wd32_ref[...].astype(BF16)

    @pl.when(first)
    def _():
        gather_wait(slot)

        @pl.when(t == 0)
        def _():
            gather_rows(inv_nxt_ref, 1, 0, ts)

        hm_ref[...] = _rms(xbuf[slot, :, :D_MODEL], gm_ref[...]).astype(BF16)
        yacc_ref[...] = jnp.zeros_like(yacc_ref)

    @pl.when(overlapped)
    def _():
        experts(True)

    @pl.when(((flags & VISIT_VALID) != 0) & jnp.logical_not(overlapped))
    def _():
        experts(False)

    @pl.when((flags & VISIT_LAST) != 0)
    def _():
        @pl.when(t >= 2)
        def _():
            scatter_wait(slot)

        obuf[slot] = _rms(xbuf[slot, :, :D_MODEL] + yacc_ref[...], gf_ref[...])

    @pl.when(v == pl.num_programs(0) - 1)
    def _():
        last_slot = (nt - 1) % 2
        scatter_rows(inv_cur_ref, last_slot, 0, ts)
        scatter_wait(0)
        scatter_wait(1)
        gather_wait(1 - last_slot)


def _moe_sorted(xa, inv, vt, vg, vf, gm, wg, wu, wd, gf, *, ts, sub):
    N = xa.shape[0]
    nt = N // ts
    assert N % ts == 0 and ts % sub == 0 and nt >= 2
    E, D, FF = EXPERTS_PER_GROUP, D_MODEL, EXPERT_FF
    grp = lambda v, vt, vg, vf: (vg[v], 0, 0)
    const = lambda v, vt, vg, vf: (0, 0)
    smem_row = lambda fn: pl.BlockSpec((None, 1, ts), fn, memory_space=pltpu.SMEM)
    grid_spec = pltpu.PrefetchScalarGridSpec(
        num_scalar_prefetch=3,
        grid=(vt.shape[0],),
        in_specs=[
            smem_row(lambda v, vt, vg, vf: (vt[v], 0, 0)),
            smem_row(lambda v, vt, vg, vf: (jnp.minimum(vt[v] + 1, nt - 1), 0, 0)),
            smem_row(lambda v, vt, vg, vf: (jnp.maximum(vt[v] - 1, 0), 0, 0)),
            pl.BlockSpec(memory_space=pl.ANY),
            pl.BlockSpec((1, D), const),
            pl.BlockSpec((E, D, FF), grp), pl.BlockSpec((E, D, FF), grp),
            pl.BlockSpec((E, FF, D), grp),
            pl.BlockSpec((1, D), const),
        ],
        out_specs=pl.BlockSpec(memory_space=pl.ANY),
        scratch_shapes=[
            pltpu.VMEM((2, ts, XA_WIDTH), F32), pltpu.VMEM((2, ts, D), F32),
            pltpu.VMEM((ts, D), BF16), pltpu.VMEM((ts, D), F32),
            pltpu.VMEM((E, D, FF), BF16), pltpu.VMEM((E, D, FF), BF16), pltpu.VMEM((E, FF, D), BF16),
            pltpu.SemaphoreType.DMA((2,)), pltpu.SemaphoreType.DMA((2,)),
        ],
    )
    return pl.pallas_call(
        functools.partial(_moe_body, ts=ts, nt=nt, sub=sub),
        grid_spec=grid_spec,
        out_shape=jax.ShapeDtypeStruct((N, D), F32),
        compiler_params=pltpu.CompilerParams(
            dimension_semantics=("arbitrary",), vmem_limit_bytes=VMEM_LIMIT),
        name="moe_sorted",
    )(vt, vg, vf, inv, inv, inv, xa, gm, wg, wu, wd, gf)


def kernel(x, mem, norm_mix_g, w_in, fox_bf, conv_dw_w, conv_dw_b, conv_ln_g, conv_ln_b, conv_pw_w, attn_branch_w, w_out, norm_xa_g, norm_mem_g, xa_wq, xa_wk, xa_wv, xa_wo, norm_moe_g, router_group_w, router_group_b, router_expert_w, router_expert_b, expert_w_gate, expert_w_up, expert_w_down, norm_final_g):
    B, T, D = x.shape
    depth = w_in.shape[0]
    assert depth == 1, "the final norm is fused into the last layer's MoE kernel"
    row = lambda v: v.reshape(1, -1).astype(F32)
    for l in range(depth):
        u, gc, ga, qkv_t, f_t = _proj_in(x, row(norm_mix_g[l]), jnp.swapaxes(w_in, 1, 2).astype(F32),
                                         layer=l, tm=512)
        qaux, kaux = _fox_prep(f_t, fox_bf[l].reshape(FOX_HEADS, 1).astype(F32))
        attn_t = _fox_attn(qkv_t, qaux, kaux, tq=512, heads=1)
        x = _mix_out(x, u, gc, ga, attn_t, conv_dw_w[l].astype(F32), row(conv_dw_b[l]),
                     row(conv_ln_g[l]), row(conv_ln_b[l]), conv_pw_w[l].astype(F32),
                     attn_branch_w[l].astype(F32), w_out[l].astype(F32), tc=512)
        wr_t = jnp.zeros((ROUTE_ROWS, D), F32).at[:N_GROUPS].set(router_group_w[l].T)
        br_t = jnp.zeros((ROUTE_ROWS, 1), F32).at[:N_GROUPS, 0].set(router_group_b[l])
        for g in range(N_GROUPS):
            r0, e0 = SUBLANES * (g + 1), EXPERTS_PER_GROUP * g
            wr_t = wr_t.at[r0:r0 + EXPERTS_PER_GROUP].set(
                router_expert_w[l][:, e0:e0 + EXPERTS_PER_GROUP].T)
            br_t = br_t.at[r0:r0 + EXPERTS_PER_GROUP, 0].set(
                router_expert_b[l][e0:e0 + EXPERTS_PER_GROUP])
        xa, cnt, code = _xattn_route(x, row(norm_xa_g[l]), xa_wq[l].T.astype(BF16), mem,
                                     row(norm_mem_g[l]), xa_wk[l].astype(BF16),
                                     xa_wv[l].T.astype(BF16),
                                     xa_wo[l].astype(BF16), row(norm_moe_g[l]),
                                     wr_t.astype(BF16), br_t, tx=1024)
        xa = xa.reshape(B * T, XA_WIDTH)
        inv, vt, vg, vf = _moe_plan(code[:, :, LANE_CODE, :].reshape(B * T), cnt, ts=512)
        x = _moe_sorted(xa, inv, vt, vg, vf, row(norm_moe_g[l]), expert_w_gate[l].astype(F32),
                        expert_w_up[l].astype(F32), expert_w_down[l].astype(F32),
                        row(norm_final_g), ts=512, sub=512).reshape(B, T, D)
    return x
```
